```python
import math
import jax, jax.numpy as jnp
from jax import lax
import numpy as np

D_MODEL = 1024
BATCH = 16
SEQ = 2048
DEPTH = 2
DEC_BATCH = 32
DEC_SEQ = 1
PAST_LEN = 16384
PAGE_SIZE = 128

N_AB_LAYERS = (DEPTH + 1) // 2
N_C_LAYERS = DEPTH // 2
DH_A = 64
H_A = (D_MODEL // 2) // DH_A
D_A = H_A * DH_A
SB_BLOCK = 128
N_B = 64
H_B = (D_MODEL // 2) // N_B
D_B = H_B * N_B
R_W = 64
R_A = 64
R_G = 128
NB_COLS = 3 * D_B + R_W + R_A + R_G
N_IN_EVEN = 3 * D_A + NB_COLS
DK_C = 128
DV_C = 128
H_C = D_MODEL // DK_C
DQK_C = H_C * DK_C
DVV_C = H_C * DV_C
DN_CONV_W = 4
DN_CONV_DIM = 2 * DQK_C + DVV_C
N_IN_ODD = DN_CONV_DIM + DVV_C + 2 * H_C
GDN_CHUNK = 64
D_FF = ((8 * D_MODEL // 3 + 127) // 128) * 128
FFN_CONV_W = 3
RMS_EPS = 1e-6
GN_EPS = 64e-5
POOL_SLACK_NUM = 5
POOL_SLACK_DEN = 4

kernel_name = 'hybrid_stickbreak_rwkv7_gdn_convffn_step'


def rmsnorm(x, g):
    xf = x.astype(jnp.float32)
    y = xf * lax.rsqrt(jnp.mean(xf * xf, axis=-1, keepdims=True) + RMS_EPS)
    return (y * g.astype(jnp.float32)).astype(x.dtype)


def l2norm(x):
    xf = x.astype(jnp.float32)
    return xf * lax.rsqrt(jnp.sum(xf * xf, axis=-1, keepdims=True) + 1e-6)


def causal_dwconv(x, buf, w):
    width = w.shape[0]
    t = x.shape[1]
    xp = jnp.concatenate([buf.astype(x.dtype), x], axis=1)
    y = sum(xp[:, i:i + t] * w[i] for i in range(width))
    return y, xp[:, t:]


def gather_pages(pool, layer, page_table):
    b, n = page_table.shape
    rows = pool[layer, page_table]
    return rows.reshape(b, n * pool.shape[2], pool.shape[3], pool.shape[4])


def sb_weights(z, q_pos, k_pos):
    causal = k_pos[None, :] < q_pos[:, None]
    u = jnp.where(causal, jax.nn.log_sigmoid(-z), 0.0)
    after = lax.cumsum(u, axis=z.ndim - 1, reverse=True) - u
    return jnp.where(causal, jnp.exp(jax.nn.log_sigmoid(z) + after), 0.0)


def stick_breaking(q, k, v, past_k, past_v, bias):
    t = q.shape[1]
    p_len = 0 if past_k is None else past_k.shape[1]
    scale = DH_A ** -0.5
    bias_h = bias.astype(jnp.float32)[None, :, None, None]

    def score(qb, kb):
        return jnp.einsum('bthd,bshd->bhts', qb, kb, preferred_element_type=jnp.float32) * scale + bias_h

    def read(wb, vb):
        return jnp.einsum('bhts,bshd->bthd', wb.astype(vb.dtype), vb, preferred_element_type=jnp.float32)

    outs = []
    for start in range(0, t, SB_BLOCK):
        end = min(start + SB_BLOCK, t)
        qb = q[:, start:end]
        z = score(qb, k[:, :end])
        if p_len:
            z = jnp.concatenate([score(qb, past_k), z], axis=-1)
        w = sb_weights(z, p_len + jnp.arange(start, end), jnp.arange(p_len + end))
        o = read(w[..., p_len:], v[:, :end])
        if p_len:
            o = o + read(w[..., :p_len], past_v)
        outs.append(o)
    return jnp.concatenate(outs, axis=1).astype(q.dtype)


def rwkv7_mix(pb, shift0, s0, mu, w0, w_up, a0, a_up, g_up, k_k, k_a, r_k, ln_w, ln_b):
    f32 = jnp.float32
    b, t, _ = pb.shape
    prev = jnp.concatenate([shift0[:, None].astype(pb.dtype), pb[:, :-1]], axis=1)
    xs = pb + (prev - pb) * mu
    r, k, v, xw, xa, xg = jnp.split(
        xs, [D_B, 2 * D_B, 3 * D_B, 3 * D_B + R_W, 3 * D_B + R_W + R_A], axis=-1)
    w_log = -jax.nn.softplus(-(w0 + jnp.tanh(xw) @ w_up).astype(f32)) - 0.5
    decay = jnp.exp(-jnp.exp(w_log))
    a = jax.nn.sigmoid((a0 + xa @ a_up).astype(f32))
    g = jax.nn.sigmoid(xg) @ g_up

    def heads(x):
        return x.astype(f32).reshape(b, t, H_B, N_B)

    kk = l2norm(heads(k * k_k))
    k_mod = k.astype(f32) * (1.0 + (a - 1.0) * k_a)
    r_h, k_h, v_h, w_h, a_h = heads(r), heads(k_mod), heads(v), heads(decay), heads(a)

    def step(s, inp):
        r_t, w_t, k_t, v_t, kk_t, a_t = inp
        sa = jnp.einsum('bhvk,bhk->bhv', s, kk_t)
        s = (s * w_t[:, :, None, :] - sa[..., None] * (kk_t * a_t)[:, :, None, :]
             + v_t[..., None] * k_t[:, :, None, :])
        return s, jnp.einsum('bhvk,bhk->bhv', s, r_t)

    seq = tuple(jnp.moveaxis(x, 1, 0) for x in (r_h, w_h, k_h, v_h, kk, a_h))
    s_fin, o = lax.scan(step, s0.astype(f32), seq)
    o = jnp.moveaxis(o, 0, 1)
    mean = jnp.mean(o, axis=-1, keepdims=True)
    var = jnp.mean(jnp.square(o - mean), axis=-1, keepdims=True)
    o = ((o - mean) * lax.rsqrt(var + GN_EPS)).reshape(b, t, D_B) * ln_w + ln_b
    bonus = jnp.sum(r_h * k_h * r_k.reshape(H_B, N_B), axis=-1, keepdims=True) * v_h
    o = (o + bonus.reshape(b, t, D_B)) * g
    return o.astype(pb.dtype), s_fin, pb[:, -1]


def gated_delta(q, k, v, beta, g, s0):
    b, t, h, _ = q.shape
    c = min(GDN_CHUNK, t)
    n = -(-t // c)
    pad = n * c - t

    def chunks(x):
        x = jnp.pad(x, [(0, 0), (0, pad)] + [(0, 0)] * (x.ndim - 2))
        x = x.reshape(b, n, c, *x.shape[2:])
        return jnp.moveaxis(x, (1, 3), (0, 2))

    qc, kc, vc, bc, gc = (chunks(x) for x in (q, k, v, beta, g))
    gcum = jnp.cumsum(gc, axis=-1)
    incl = jnp.tril(jnp.ones((c, c), bool))
    strict = jnp.tril(jnp.ones((c, c), bool), -1)
    diff = gcum[..., :, None] - gcum[..., None, :]
    lmask = jnp.where(incl, jnp.exp(jnp.where(incl, diff, 0.0)), 0.0)
    kb = kc * bc[..., None]
    m = jnp.where(strict, jnp.einsum('nbhik,nbhjk->nbhij', kb, kc) * lmask, 0.0)
    eye = jnp.broadcast_to(jnp.eye(c, dtype=m.dtype), m.shape)
    tinv = lax.linalg.triangular_solve(eye + m, eye, left_side=True, lower=True)
    u = tinv @ (vc * bc[..., None])
    wmat = tinv @ (kb * jnp.exp(gcum)[..., None])
    aqk = jnp.where(incl, jnp.einsum('nbhik,nbhjk->nbhij', qc, kc) * lmask, 0.0)

    def step(s, inp):
        q_i, k_i, u_i, w_i, g_i, a_i = inp
        v_new = u_i - jnp.einsum('bhck,bhkv->bhcv', w_i, s)
        o = (jnp.einsum('bhck,bhkv->bhcv', q_i * jnp.exp(g_i)[..., None], s)
             + jnp.einsum('bhij,bhjv->bhiv', a_i, v_new))
        g_last = g_i[..., -1:]
        s = (s * jnp.exp(g_last)[..., None]
             + jnp.einsum('bhck,bhcv->bhkv', k_i * jnp.exp(g_last - g_i)[..., None], v_new))
        return s, o

    s_fin, o = lax.scan(step, s0, (qc, kc, u, wmat, gcum, aqk))
    o = jnp.moveaxis(o, (0, 2), (1, 3)).reshape(b, n * c, h, -1)[:, :t]
    return o, s_fin


def mix_even(h, past_kv, wkv0, shift0, p, i):
    b, t, _ = h.shape
    proj = h @ p['w_in_even'][i]
    q, k, v, pb = jnp.split(proj, [D_A, 2 * D_A, 3 * D_A], axis=-1)
    q = rmsnorm(q.reshape(b, t, H_A, DH_A), p['qnorm_a'][i])
    k = rmsnorm(k.reshape(b, t, H_A, DH_A), p['knorm_a'][i])
    v = v.reshape(b, t, H_A, DH_A)
    if past_kv is None:
        o_a = stick_breaking(q, k, v, None, None, p['sb_bias'][i])
    else:
        o_a = stick_breaking(q, k, v, past_kv[0].astype(k.dtype), past_kv[1].astype(v.dtype),
                             p['sb_bias'][i])
    o_a = o_a.reshape(b, t, D_A)
    o_b, wkv, shift = rwkv7_mix(pb, shift0, wkv0, p['mu_b'][i], p['w0_b'][i], p['w_up_b'][i],
                                p['a0_b'][i], p['a_up_b'][i], p['g_up_b'][i], p['kk_b'][i],
                                p['ka_b'][i], p['rk_b'][i], p['lnx_w_b'][i], p['lnx_b_b'][i])
    y = jnp.concatenate([o_a, o_b.astype(o_a.dtype)], axis=-1) @ p['w_out_even'][i]
    return y, k, v, wkv, shift


def mix_odd(h, s0, conv0, p, i):
    f32 = jnp.float32
    b, t, _ = h.shape
    proj = h @ p['w_in_odd'][i]
    qkv, z, b_raw, a_raw = jnp.split(
        proj, [DN_CONV_DIM, DN_CONV_DIM + DVV_C, DN_CONV_DIM + DVV_C + H_C], axis=-1)
    qkv_c, conv_new = causal_dwconv(qkv, conv0, p['conv_dn'][i])
    qkv_c = jax.nn.silu(qkv_c)
    q, k, v = jnp.split(qkv_c, [DQK_C, 2 * DQK_C], axis=-1)
    q = l2norm(q.reshape(b, t, H_C, DK_C)) * (DK_C ** -0.5)
    k = l2norm(k.reshape(b, t, H_C, DK_C))
    v = v.astype(f32).reshape(b, t, H_C, DV_C)
    beta = jax.nn.sigmoid(b_raw.astype(f32))
    g = -jnp.exp(p['a_log_dn'][i].astype(f32)) * jax.nn.softplus(a_raw.astype(f32) + p['dt_bias_dn'][i])
    o, s_fin = gated_delta(q, k, v, beta, g, s0.astype(f32))
    o = rmsnorm(o, p['onorm_dn'][i]) * jax.nn.silu(z.astype(f32)).reshape(b, t, H_C, DV_C)
    y = o.reshape(b, t, DVV_C).astype(h.dtype) @ p['w_out_odd'][i]
    return y, s_fin, conv_new


def conv_ffn(h, buf, p, l):
    gate = h @ p['w_gate'][l]
    up = h @ p['w_up'][l]
    gate, buf_new = causal_dwconv(gate, buf, p['conv_ffn'][l])
    y = (jax.nn.silu(gate + p['conv_ffn_b'][l]) * up) @ p['w_down'][l]
    return y, buf_new


def trunk(x, kv, wkv0, shift0, dn0, dnconv0, ffnconv0, p):
    new_k, new_v, new_wkv, new_shift, new_dn, new_dnconv, new_ffn = [], [], [], [], [], [], []
    for l in range(DEPTH):
        i = l // 2
        if l % 2 == 0:
            h = rmsnorm(x, p['norm_mix_even'][i])
            if kv is None:
                past_kv = None
            else:
                past_kv = (gather_pages(kv[0], i, kv[2]), gather_pages(kv[1], i, kv[2]))
            y, k_rows, v_rows, wkv, shift = mix_even(h, past_kv, wkv0[i], shift0[i], p, i)
            new_k.append(k_rows)
            new_v.append(v_rows)
            new_wkv.append(wkv)
            new_shift.append(shift)
        else:
            h = rmsnorm(x, p['norm_mix_odd'][i])
            y, s_fin, cbuf = mix_odd(h, dn0[i], dnconv0[i], p, i)
            new_dn.append(s_fin)
            new_dnconv.append(cbuf)
        x = x + y
        h = rmsnorm(x, p['norm_ffn'][l])
        y, fbuf = conv_ffn(h, ffnconv0[l], p, l)
        new_ffn.append(fbuf)
        x = x + y
    dt = x.dtype
    states = (jnp.stack(new_k).astype(dt), jnp.stack(new_v).astype(dt), jnp.stack(new_wkv).astype(dt),
              jnp.stack(new_shift).astype(dt), jnp.stack(new_dn).astype(dt),
              jnp.stack(new_dnconv).astype(dt), jnp.stack(new_ffn).astype(dt))
    return x, states


def setup_inputs(seed: int = 0) -> dict:
    key = jax.random.key(seed)
    ks = iter(jax.random.split(key, 64))
    f32 = jnp.float32

    def nrm(shape, scale):
        return jax.random.normal(next(ks), shape, f32) * scale

    def uni(shape, lo, hi):
        return jax.random.uniform(next(ks), shape, f32, lo, hi)

    n_pages = PAST_LEN // PAGE_SIZE
    n_phys = (POOL_SLACK_NUM * DEC_BATCH * n_pages) // POOL_SLACK_DEN
    perm = jax.random.permutation(next(ks), n_phys)
    page_table = perm[:DEC_BATCH * n_pages].reshape(DEC_BATCH, n_pages).astype(jnp.int32)
    dt = jnp.exp(uni((N_C_LAYERS, H_C), math.log(1e-3), math.log(1e-1)))
    dt_bias = dt + jnp.log(-jnp.expm1(-dt))
    return {
        'x_prompt': nrm((BATCH, SEQ, D_MODEL), 1.0),
        'x_sample': nrm((DEC_BATCH, DEC_SEQ, D_MODEL), 1.0),
        'cache_k': nrm((N_AB_LAYERS, n_phys, PAGE_SIZE, H_A, DH_A), 1.0),
        'cache_v': nrm((N_AB_LAYERS, n_phys, PAGE_SIZE, H_A, DH_A), 1.0),
        'page_table': page_table,
        'state_wkv': nrm((N_AB_LAYERS, DEC_BATCH, H_B, N_B, N_B), 0.3),
        'state_shift': nrm((N_AB_LAYERS, DEC_BATCH, NB_COLS), 1.0),
        'state_dn': nrm((N_C_LAYERS, DEC_BATCH, H_C, DK_C, DV_C), 0.1),
        'state_dn_conv': nrm((N_C_LAYERS, DEC_BATCH, DN_CONV_W - 1, DN_CONV_DIM), 1.0),
        'state_ffn_conv': nrm((DEPTH, DEC_BATCH, FFN_CONV_W - 1, D_FF), 1.0),
        'norm_mix_even': 1.0 + nrm((N_AB_LAYERS, D_MODEL), 0.02),
        'w_in_even': nrm((N_AB_LAYERS, D_MODEL, N_IN_EVEN), D_MODEL ** -0.5),
        'qnorm_a': 1.0 + nrm((N_AB_LAYERS, DH_A), 0.02),
        'knorm_a': 1.0 + nrm((N_AB_LAYERS, DH_A), 0.02),
        'sb_bias': uni((N_AB_LAYERS, H_A), -7.0, -6.0),
        'mu_b': uni((N_AB_LAYERS, NB_COLS), 0.0, 1.0),
        'w0_b': uni((N_AB_LAYERS, D_B), -6.0, -1.0),
        'w_up_b': nrm((N_AB_LAYERS, R_W, D_B), 0.1),
        'a0_b': nrm((N_AB_LAYERS, D_B), 0.1),
        'a_up_b': nrm((N_AB_LAYERS, R_A, D_B), 0.1),
        'g_up_b': nrm((N_AB_LAYERS, R_G, D_B), R_G ** -0.5),
        'kk_b': 0.85 + nrm((N_AB_LAYERS, D_B), 0.05),
        'ka_b': 1.0 + nrm((N_AB_LAYERS, D_B), 0.05),
        'rk_b': nrm((N_AB_LAYERS, D_B), 0.1),
        'lnx_w_b': 1.0 + nrm((N_AB_LAYERS, D_B), 0.02),
        'lnx_b_b': nrm((N_AB_LAYERS, D_B), 0.02),
        'w_out_even': nrm((N_AB_LAYERS, D_A + D_B, D_MODEL), (D_A + D_B) ** -0.5),
        'norm_mix_odd': 1.0 + nrm((N_C_LAYERS, D_MODEL), 0.02),
        'w_in_odd': nrm((N_C_LAYERS, D_MODEL, N_IN_ODD), D_MODEL ** -0.5),
        'conv_dn': nrm((N_C_LAYERS, DN_CONV_W, DN_CONV_DIM), DN_CONV_W ** -0.5),
        'a_log_dn': jnp.log(uni((N_C_LAYERS, H_C), 1.0, 16.0)),
        'dt_bias_dn': dt_bias,
        'onorm_dn': 1.0 + nrm((N_C_LAYERS, DV_C), 0.02),
        'w_out_odd': nrm((N_C_LAYERS, DVV_C, D_MODEL), DVV_C ** -0.5),
        'norm_ffn': 1.0 + nrm((DEPTH, D_MODEL), 0.02),
        'w_gate': nrm((DEPTH, D_MODEL, D_FF), D_MODEL ** -0.5),
        'w_up': nrm((DEPTH, D_MODEL, D_FF), D_MODEL ** -0.5),
        'conv_ffn': nrm((DEPTH, FFN_CONV_W, D_FF), FFN_CONV_W ** -0.5),
        'conv_ffn_b': nrm((DEPTH, D_FF), 0.02),
        'w_down': nrm((DEPTH, D_FF, D_MODEL), D_FF ** -0.5),
    }


def reference(x_prompt, x_sample, cache_k, cache_v, page_table, state_wkv, state_shift, state_dn,
              state_dn_conv, state_ffn_conv, norm_mix_even, w_in_even, qnorm_a, knorm_a, sb_bias,
              mu_b, w0_b, w_up_b, a0_b, a_up_b, g_up_b, kk_b, ka_b, rk_b, lnx_w_b, lnx_b_b,
              w_out_even, norm_mix_odd, w_in_odd, conv_dn, a_log_dn, dt_bias_dn, onorm_dn, w_out_odd,
              norm_ffn, w_gate, w_up, conv_ffn, conv_ffn_b, w_down):
    p = dict(norm_mix_even=norm_mix_even, w_in_even=w_in_even, qnorm_a=qnorm_a, knorm_a=knorm_a,
             sb_bias=sb_bias, mu_b=mu_b, w0_b=w0_b, w_up_b=w_up_b, a0_b=a0_b, a_up_b=a_up_b,
             g_up_b=g_up_b, kk_b=kk_b, ka_b=ka_b, rk_b=rk_b, lnx_w_b=lnx_w_b, lnx_b_b=lnx_b_b,
             w_out_even=w_out_even, norm_mix_odd=norm_mix_odd, w_in_odd=w_in_odd, conv_dn=conv_dn,
             a_log_dn=a_log_dn, dt_bias_dn=dt_bias_dn, onorm_dn=onorm_dn, w_out_odd=w_out_odd,
             norm_ffn=norm_ffn, w_gate=w_gate, w_up=w_up, conv_ffn=conv_ffn,
             conv_ffn_b=conv_ffn_b, w_down=w_down)
    f32 = jnp.float32
    b = x_prompt.shape[0]
    y_prompt, prompt_states = trunk(
        x_prompt, None,
        jnp.zeros((N_AB_LAYERS, b, H_B, N_B, N_B), f32),
        jnp.zeros((N_AB_LAYERS, b, NB_COLS), x_prompt.dtype),
        jnp.zeros((N_C_LAYERS, b, H_C, DK_C, DV_C), f32),
        jnp.zeros((N_C_LAYERS, b, DN_CONV_W - 1, DN_CONV_DIM), x_prompt.dtype),
        jnp.zeros((DEPTH, b, FFN_CONV_W - 1, D_FF), x_prompt.dtype), p)
    k_prompt, v_prompt, wkv_prompt, shift_prompt, dn_prompt, dnconv_prompt, ffnconv_prompt = prompt_states
    y_sample, sample_states = trunk(
        x_sample, (cache_k, cache_v, page_table), state_wkv, state_shift, state_dn,
        state_dn_conv, state_ffn_conv, p)
    k_sample, v_sample, wkv_sample, shift_sample, dn_sample, dnconv_sample, ffnconv_sample = sample_states
    return (y_prompt, y_sample, k_prompt, v_prompt, k_sample, v_sample, wkv_prompt, wkv_sample,
            shift_prompt, shift_sample, dn_prompt, dn_sample, dnconv_prompt, dnconv_sample,
            ffnconv_prompt, ffnconv_sample)
```

```python
import functools

import jax
import jax.numpy as jnp
from jax import lax
from jax.experimental import pallas as pl
from jax.experimental.pallas import tpu as pltpu

F32 = jnp.float32
BF16 = jnp.bfloat16

RMS_EPS = 1e-6
L2_EPS = 1e-6
GN_EPS = 64e-5

DH_A = 64
N_B = 64
R_W, R_A, R_G = 64, 64, 128
DK_C = 128
DV_C = 128
DN_CONV_W = 4
FFN_CONV_W = 3

SB_BLOCK = 128
CHUNK = 64
SUBLANES = 8
LANES = 128
VMEM_LIMIT_BYTES = 56 * 1024 * 1024


def _cparams(*sem):
    return pltpu.CompilerParams(dimension_semantics=sem, vmem_limit_bytes=VMEM_LIMIT_BYTES)


def _dot(a, b):
    return jnp.dot(a.astype(BF16), b.astype(BF16), preferred_element_type=F32)


def _dot_nt(a, b):
    return lax.dot_general(a.astype(BF16), b.astype(BF16), (((1,), (1,)), ((), ())),
                           preferred_element_type=F32)


def _dot_tn(a, b):
    return lax.dot_general(a.astype(BF16), b.astype(BF16), (((0,), (0,)), ((), ())),
                           preferred_element_type=F32)


def _split2(a):
    hi = a.astype(BF16)
    lo = (a - hi.astype(F32)).astype(BF16)
    return hi, lo


def _split3(a):
    hi = a.astype(BF16)
    r1 = a - hi.astype(F32)
    mid = r1.astype(BF16)
    lo = (r1 - mid.astype(F32)).astype(BF16)
    return hi, mid, lo


def _dot_hi(a, b):
    ah, al = _split2(a)
    bh, bl = _split2(b)
    d = functools.partial(jnp.dot, preferred_element_type=F32)
    return d(ah, bh) + (d(ah, bl) + d(al, bh))


def _dot_mask_lhs(mask_bf16, x):
    d = functools.partial(jnp.dot, preferred_element_type=F32)
    h, m, l = _split3(x)
    return d(mask_bf16, h) + (d(mask_bf16, m) + d(mask_bf16, l))


def _dot_mask_rhs(x, mask_bf16):
    d = functools.partial(jnp.dot, preferred_element_type=F32)
    h, m, l = _split3(x)
    return d(h, mask_bf16) + (d(m, mask_bf16) + d(l, mask_bf16))


def _softplus(x):
    return jnp.maximum(x, 0.0) + jnp.log1p(jnp.exp(-jnp.abs(x)))


def _sigmoid(x):
    return 1.0 / (1.0 + jnp.exp(-x))


def _silu(x):
    return x * _sigmoid(x)


def _rms(x, g):
    return x * lax.rsqrt(jnp.mean(x * x, axis=-1, keepdims=True) + RMS_EPS) * g


def _l2(x):
    return x * lax.rsqrt(jnp.sum(x * x, axis=-1, keepdims=True) + L2_EPS)


def _iota2(shape, dim):
    return lax.broadcasted_iota(jnp.int32, shape, dim)


def _shift_rows(cur, carry, d):
    rolled = pltpu.roll(cur, d, 0)
    crolled = pltpu.roll(carry, d, 0)
    first = jnp.where(_iota2(carry.shape, 0) < d, crolled, rolled[:SUBLANES])
    if cur.shape[0] == SUBLANES:
        return first
    return jnp.concatenate([first, rolled[SUBLANES:]], axis=0)


def _tri_inv(a):
    n = a.shape[0]
    row = _iota2((n, n), 0)
    col = _iota2((n, n), 1)
    eye = (row == col).astype(F32)
    t = eye - jnp.where((row >> 1) == (col >> 1), a, 0.0)
    s = 2
    while s < n:
        sh = s.bit_length() - 1
        e = jnp.where(((row >> (sh + 1)) == (col >> (sh + 1))) & ((row >> sh) != (col >> sh)), a, 0.0)
        t = t - _dot_hi(_dot_hi(t, e), t)
        s *= 2
    return t


def _norm_matmul_kernel(x_ref, g_ref, w_ref, *o_refs, splits):
    h = _rms(x_ref[...], g_ref[...]).astype(BF16)
    y = jnp.dot(h, w_ref[...], preferred_element_type=F32)
    off = 0
    for o_ref, n in zip(o_refs, splits):
        o_ref[...] = y[:, off:off + n]
        off += n


def norm_matmul(x, g, w, splits, tm):
    m, k = x.shape
    n = w.shape[1]
    assert sum(splits) == n and m % tm == 0
    return pl.pallas_call(
        functools.partial(_norm_matmul_kernel, splits=tuple(splits)),
        grid=(m // tm,),
        in_specs=[pl.BlockSpec((tm, k), lambda i: (i, 0)),
                  pl.BlockSpec((1, k), lambda i: (0, 0)),
                  pl.BlockSpec((k, n), lambda i: (0, 0))],
        out_specs=[pl.BlockSpec((tm, s), lambda i: (i, 0)) for s in splits],
        out_shape=[jax.ShapeDtypeStruct((m, s), F32) for s in splits],
        compiler_params=_cparams("arbitrary"),
        name="norm_matmul",
    )(x, g.reshape(1, k), w)


def _matmul_res_kernel(*refs, n_in):
    a_refs = refs[:n_in]
    w_refs = refs[n_in:2 * n_in]
    res_ref = refs[2 * n_in]
    o_ref = refs[2 * n_in + 1]
    acc = res_ref[...]
    for a_ref, w_ref in zip(a_refs, w_refs):
        acc = acc + jnp.dot(a_ref[...].astype(BF16), w_ref[...], preferred_element_type=F32)
    o_ref[...] = acc


def matmul_res(a_list, w_list, res, tm):
    m, n = res.shape
    assert m % tm == 0
    n_in = len(a_list)
    in_specs = ([pl.BlockSpec((tm, a.shape[1]), lambda i: (i, 0)) for a in a_list]
                + [pl.BlockSpec(w.shape, lambda i: (0, 0)) for w in w_list]
                + [pl.BlockSpec((tm, n), lambda i: (i, 0))])
    return pl.pallas_call(
        functools.partial(_matmul_res_kernel, n_in=n_in),
        grid=(m // tm,),
        in_specs=in_specs,
        out_specs=pl.BlockSpec((tm, n), lambda i: (i, 0)),
        out_shape=jax.ShapeDtypeStruct((m, n), F32),
        compiler_params=_cparams("arbitrary"),
        name="matmul_res",
    )(*a_list, *w_list, res)


def _ffn_kernel(x_ref, buf_ref, g_ref, wg_ref, wu_ref, cw_ref, cb_ref, wd_ref,
                y_ref, tail_ref, carry_scr):
    ti = pl.program_id(1)
    x = x_ref[0]
    tm = x.shape[0]
    h = _rms(x, g_ref[...]).astype(BF16)
    gate = jnp.dot(h, wg_ref[...], preferred_element_type=F32)
    up = jnp.dot(h, wu_ref[...], preferred_element_type=F32)

    @pl.when(ti == 0)
    def _():
        carry_scr[...] = jnp.zeros_like(carry_scr)
        carry_scr[SUBLANES - 2:SUBLANES, :] = buf_ref[0]

    carry = carry_scr[...]
    conv = (cw_ref[2:3, :] * gate + cw_ref[1:2, :] * _shift_rows(gate, carry, 1)
            + cw_ref[0:1, :] * _shift_rows(gate, carry, 2))
    carry_scr[...] = gate[tm - SUBLANES:tm]
    tail_ref[0] = gate[tm - 2:tm]
    act = (_silu(conv + cb_ref[...]) * up).astype(BF16)
    y_ref[0] = x + jnp.dot(act, wd_ref[...], preferred_element_type=F32)


def conv_ffn(x, buf, g, wg, wu, cw, cb, wd, tm):
    b, t, d = x.shape
    f = wg.shape[1]
    assert t % tm == 0 and tm % SUBLANES == 0
    const = lambda bi, ti: (0, 0)
    return pl.pallas_call(
        _ffn_kernel,
        grid=(b, t // tm),
        in_specs=[pl.BlockSpec((1, tm, d), lambda bi, ti: (bi, ti, 0)),
                  pl.BlockSpec((1, 2, f), lambda bi, ti: (bi, 0, 0)),
                  pl.BlockSpec((1, d), const),
                  pl.BlockSpec((d, f), const),
                  pl.BlockSpec((d, f), const),
                  pl.BlockSpec((FFN_CONV_W, f), const),
                  pl.BlockSpec((1, f), const),
                  pl.BlockSpec((f, d), const)],
        out_specs=[pl.BlockSpec((1, tm, d), lambda bi, ti: (bi, ti, 0)),
                   pl.BlockSpec((1, 2, f), lambda bi, ti: (bi, 0, 0))],
        out_shape=[jax.ShapeDtypeStruct((b, t, d), F32),
                   jax.ShapeDtypeStruct((b, 2, f), F32)],
        scratch_shapes=[pltpu.VMEM((SUBLANES, f), F32)],
        compiler_params=_cparams("arbitrary", "arbitrary"),
        name="conv_ffn",
    )(x, buf, g.reshape(1, d), wg, wu, cw, cb.reshape(1, f), wd)


def _ffn_step_kernel(x_ref, buf0_ref, buf1_ref, g_ref, wg_ref, wu_ref, cw_ref, cb_ref, wd_ref,
                     y_ref, gate_ref):
    x = x_ref[...]
    h = _rms(x, g_ref[...]).astype(BF16)
    gate = jnp.dot(h, wg_ref[...], preferred_element_type=F32)
    up = jnp.dot(h, wu_ref[...], preferred_element_type=F32)
    conv = cw_ref[2:3, :] * gate + cw_ref[1:2, :] * buf1_ref[...] + cw_ref[0:1, :] * buf0_ref[...]
    act = (_silu(conv + cb_ref[...]) * up).astype(BF16)
    y_ref[...] = x + jnp.dot(act, wd_ref[...], preferred_element_type=F32)
    gate_ref[...] = gate


def conv_ffn_step(x, buf, g, wg, wu, cw, cb, wd):
    b, d = x.shape
    f = wg.shape[1]
    full = lambda shape: pl.BlockSpec(shape, lambda i: (0,) * len(shape))
    y, gate = pl.pallas_call(
        _ffn_step_kernel,
        grid=(1,),
        in_specs=[full((b, d)), full((b, f)), full((b, f)), full((1, d)), full((d, f)), full((d, f)),
                  full((FFN_CONV_W, f)), full((1, f)), full((f, d))],
        out_specs=[full((b, d)), full((b, f))],
        out_shape=[jax.ShapeDtypeStruct((b, d), F32), jax.ShapeDtypeStruct((b, f), F32)],
        compiler_params=_cparams("arbitrary"),
        name="conv_ffn_step",
    )(x, buf[:, 0], buf[:, 1], g.reshape(1, d), wg, wu, cw, cb.reshape(1, f), wd)
    return y, jnp.stack([buf[:, 1], gate], axis=1)


def _sb_prompt_kernel(bias_ref, q_ref, k_ref, v_ref, qw_ref, kw_ref, o_ref, kn_ref,
                      kn_scr, v_scr, *, n_heads):
    qi = pl.program_id(1)
    blk = q_ref.shape[0]
    scale = DH_A ** -0.5
    row0 = pl.multiple_of(qi * blk, blk)

    k = k_ref[...]
    kn = jnp.concatenate(
        [_rms(k[:, h * DH_A:(h + 1) * DH_A], kw_ref[...]) for h in range(n_heads)], axis=-1)
    kn_ref[...] = kn
    kn_scr[pl.ds(row0, blk), :] = kn.astype(BF16)
    v_scr[pl.ds(row0, blk), :] = v_ref[...].astype(BF16)

    q = q_ref[...]
    row = _iota2((blk, blk), 0)
    col = _iota2((blk, blk), 1)
    later = (row > col).astype(BF16)
    for h in range(n_heads):
        lanes = slice(h * DH_A, (h + 1) * DH_A)
        qh = _rms(q[:, lanes], qw_ref[...]).astype(BF16)
        bias = bias_ref[h]

        def body(i, carry, qh=qh, bias=bias, lanes=lanes):
            o_acc, tail = carry
            off = pl.multiple_of((qi - i) * blk, blk)
            kj = kn_scr[pl.ds(off, blk), lanes]
            vj = v_scr[pl.ds(off, blk), lanes]
            z = lax.dot_general(qh, kj, (((1,), (1,)), ((), ())),
                                preferred_element_type=F32) * scale + bias
            causal = (off + col) < (row0 + row)
            sp = _softplus(z)
            u = jnp.where(causal, -sp, 0.0)
            after = _dot_mask_rhs(u, later) + tail
            w = jnp.where(causal, jnp.exp((z - sp) + after), 0.0)
            o_acc = o_acc + jnp.dot(w.astype(BF16), vj, preferred_element_type=F32)
            tail = tail + jnp.sum(u, axis=-1, keepdims=True)
            return o_acc, tail

        o_acc, _ = lax.fori_loop(0, qi + 1, body,
                                 (jnp.zeros((blk, DH_A), F32), jnp.zeros((blk, 1), F32)))
        o_ref[:, lanes] = o_acc


def sb_prompt(qkv, qw, kw, bias, batch, seq):
    m, n3 = qkv.shape
    d_a = n3 // 3
    n_heads = d_a // DH_A
    blk = SB_BLOCK
    assert seq % blk == 0
    nq = seq // blk
    rowblk = lambda col: (lambda b, qi: (b * nq + qi, col))
    return pl.pallas_call(
        functools.partial(_sb_prompt_kernel, n_heads=n_heads),
        grid=(batch, nq),
        in_specs=[pl.BlockSpec(memory_space=pltpu.SMEM),
                  pl.BlockSpec((blk, d_a), rowblk(0)),
                  pl.BlockSpec((blk, d_a), rowblk(1)),
                  pl.BlockSpec((blk, d_a), rowblk(2)),
                  pl.BlockSpec((1, DH_A), lambda b, qi: (0, 0)),
                  pl.BlockSpec((1, DH_A), lambda b, qi: (0, 0))],
        out_specs=[pl.BlockSpec((blk, d_a), rowblk(0)),
                   pl.BlockSpec((blk, d_a), rowblk(0))],
        out_shape=[jax.ShapeDtypeStruct((m, d_a), F32), jax.ShapeDtypeStruct((m, d_a), F32)],
        scratch_shapes=[pltpu.VMEM((seq, d_a), BF16), pltpu.VMEM((seq, d_a), BF16)],
        compiler_params=_cparams("arbitrary", "arbitrary"),
        name="sb_prompt",
    )(bias, qkv, qkv, qkv, qw.reshape(1, DH_A), kw.reshape(1, DH_A))


def _sb_decode_kernel(pt_ref, q_ref, k_ref, kp_ref, vp_ref, qw_ref, kw_ref, bias_ref,
                      o_ref, kn_ref, qexp_scr, acc_scr, tail_scr, *, n_heads):
    p = pl.program_id(1)
    n_pages = pl.num_programs(1)
    d_a = n_heads * DH_A
    scale = DH_A ** -0.5
    page = kp_ref.shape[1]
    head_of_lane = _iota2((n_heads, d_a), 1) // DH_A
    own = head_of_lane == _iota2((n_heads, d_a), 0)

    def expand_norm(x_row, w_row):
        xe = jnp.where(own, jnp.broadcast_to(x_row, (n_heads, d_a)), 0.0)
        ms = jnp.sum(xe * xe, axis=-1, keepdims=True) * (1.0 / DH_A)
        return xe * lax.rsqrt(ms + RMS_EPS) * w_row

    @pl.when(p == 0)
    def _():
        qexp_scr[...] = expand_norm(q_ref[0], qw_ref[...])
        kn_ref[0] = jnp.sum(expand_norm(k_ref[0], kw_ref[...]), axis=0, keepdims=True)
        acc_scr[...] = jnp.zeros_like(acc_scr)
        tail_scr[...] = jnp.zeros_like(tail_scr)

    kp = kp_ref[0].astype(BF16)
    vp = vp_ref[0].astype(BF16)
    z = lax.dot_general(qexp_scr[...].astype(BF16), kp, (((1,), (1,)), ((), ())),
                        preferred_element_type=F32) * scale + bias_ref[...]
    row = _iota2((page, page), 0)
    col = _iota2((page, page), 1)
    later = (row > col).astype(BF16)
    sp = _softplus(z)
    u = -sp
    after = _dot_mask_rhs(u, later) + tail_scr[...]
    w = jnp.exp((z - sp) + after)
    acc_scr[...] += jnp.dot(w.astype(BF16), vp, preferred_element_type=F32)
    tail_scr[...] += jnp.sum(u, axis=-1, keepdims=True)

    @pl.when(p == n_pages - 1)
    def _():
        o_ref[0] = jnp.sum(jnp.where(own, acc_scr[...], 0.0), axis=0, keepdims=True)


def sb_decode(qkv, cache_k, cache_v, page_table, layer, qw, kw, bias):
    b, n3 = qkv.shape
    d_a = n3 // 3
    n_heads = d_a // DH_A
    n_layers, n_phys, page = cache_k.shape[:3]
    n_pages = page_table.shape[1]
    ck = cache_k.reshape(n_layers * n_phys, page, d_a)
    cv = cache_v.reshape(n_layers * n_phys, page, d_a)
    qkv3 = qkv.reshape(b, 1, n3)
    base = layer * n_phys
    page_idx = lambda bi, p, pt: (base + pt[bi, n_pages - 1 - p], 0, 0)
    grid_spec = pltpu.PrefetchScalarGridSpec(
        num_scalar_prefetch=1,
        grid=(b, n_pages),
        in_specs=[pl.BlockSpec((1, 1, d_a), lambda bi, p, pt: (bi, 0, 0)),
                  pl.BlockSpec((1, 1, d_a), lambda bi, p, pt: (bi, 0, 1)),
                  pl.BlockSpec((1, page, d_a), page_idx),
                  pl.BlockSpec((1, page, d_a), page_idx),
                  pl.BlockSpec((1, d_a), lambda bi, p, pt: (0, 0)),
                  pl.BlockSpec((1, d_a), lambda bi, p, pt: (0, 0)),
                  pl.BlockSpec((n_heads, 1), lambda bi, p, pt: (0, 0))],
        out_specs=[pl.BlockSpec((1, 1, d_a), lambda bi, p, pt: (bi, 0, 0)),
                   pl.BlockSpec((1, 1, d_a), lambda bi, p, pt: (bi, 0, 0))],
        scratch_shapes=[pltpu.VMEM((n_heads, d_a), F32), pltpu.VMEM((n_heads, d_a), F32),
                        pltpu.VMEM((n_heads, 1), F32)],
    )
    o, kn = pl.pallas_call(
        functools.partial(_sb_decode_kernel, n_heads=n_heads),
        grid_spec=grid_spec,
        out_shape=[jax.ShapeDtypeStruct((b, 1, d_a), F32), jax.ShapeDtypeStruct((b, 1, d_a), F32)],
        compiler_params=_cparams("arbitrary", "arbitrary"),
        name="sb_decode",
    )(page_table, qkv3, qkv3, ck, cv, jnp.tile(qw, n_heads).reshape(1, d_a),
      jnp.tile(kw, n_heads).reshape(1, d_a), bias.reshape(n_heads, 1))
    return o.reshape(b, d_a), kn.reshape(b, d_a)


def _rwkv_tokens(pb, prev, mu, w0, w_up, a0, a_up, g_up, k_k, k_a, d_b):
    xs = pb + (prev - pb) * mu
    r = xs[:, 0:d_b]
    k = xs[:, d_b:2 * d_b]
    v = xs[:, 2 * d_b:3 * d_b]
    xw = xs[:, 3 * d_b:3 * d_b + R_W]
    xa = xs[:, 3 * d_b + R_W:3 * d_b + R_W + R_A]
    xg = xs[:, 3 * d_b + R_W + R_A:3 * d_b + R_W + R_A + R_G]
    w_log = -_softplus(-(w0 + _dot(jnp.tanh(xw), w_up))) - 0.5
    log_decay = -jnp.exp(w_log)
    a = _sigmoid(a0 + _dot(xa, a_up))
    g = _dot(_sigmoid(xg), g_up)
    kk_raw = k * k_k
    k_mod = k * (1.0 + (a - 1.0) * k_a)
    return r, k_mod, v, kk_raw, a, g, log_decay


def _rwkv_finish(o, r, k_mod, v, g, r_k, ln_w, ln_b):
    mean = jnp.mean(o, axis=-1, keepdims=True)
    var = jnp.mean(jnp.square(o - mean), axis=-1, keepdims=True)
    on = (o - mean) * lax.rsqrt(var + GN_EPS) * ln_w + ln_b
    bonus = jnp.sum(r * k_mod * r_k, axis=-1, keepdims=True) * v
    return (on + bonus) * g


def _rwkv_chunk_kernel(pb_ref, shift0_ref, s0_ref, mu_ref, w0_ref, wup_ref, a0_ref, aup_ref, gup_ref,
                       kk_ref, ka_ref, rk_ref, lnw_ref, lnb_ref, o_ref, sfin_ref,
                       carry_scr, s_scr, *, n_heads):
    c = pl.program_id(1)
    d_b = n_heads * N_B
    pb = pb_ref[...]
    t = pb.shape[0]

    @pl.when(c == 0)
    def _():
        carry_scr[...] = jnp.zeros_like(carry_scr)
        carry_scr[SUBLANES - 1:SUBLANES, :] = shift0_ref[0]
        s_scr[...] = s0_ref[0]

    prev = _shift_rows(pb, carry_scr[...], 1)
    carry_scr[...] = pb[t - SUBLANES:t]
    r, k_mod, v, kk_raw, a, g, log_decay = _rwkv_tokens(
        pb, prev, mu_ref[...], w0_ref[...], wup_ref[...], a0_ref[...], aup_ref[...], gup_ref[...],
        kk_ref[...], ka_ref[...], d_b)

    row = _iota2((t, t), 0)
    col = _iota2((t, t), 1)
    incl = row >= col
    strict = row > col
    cum = _dot_mask_lhs(incl.astype(BF16), log_decay)
    cum_prev = cum - log_decay
    cum_last = cum[t - 1:t, :]
    e_cum = jnp.exp(cum)
    e_prev = jnp.exp(cum_prev)
    e_neg = jnp.exp(-cum)
    e_rest = jnp.exp(cum_last - cum)
    e_last = jnp.exp(cum_last)

    for h in range(n_heads):
        lanes = slice(h * N_B, (h + 1) * N_B)
        kk = _l2(kk_raw[:, lanes])
        ka = kk * a[:, lanes]
        kap = kk * e_prev[:, lanes]
        rt = r[:, lanes] * e_cum[:, lanes]
        khat = k_mod[:, lanes] * e_neg[:, lanes]
        bhat = ka * e_neg[:, lanes]
        vh = v[:, lanes]
        s = s_scr[h]
        a_bk = jnp.where(strict, _dot_nt(kap, bhat), 0.0)
        a_kk = jnp.where(strict, _dot_nt(kap, khat), 0.0)
        a_br = jnp.where(incl, _dot_nt(rt, bhat), 0.0)
        a_kr = jnp.where(incl, _dot_nt(rt, khat), 0.0)
        u = _dot_hi(_tri_inv(a_bk), _dot_nt(kap, s) + _dot(a_kk, vh))
        o = _dot_nt(rt, s) + _dot(a_kr, vh) - _dot(a_br, u)
        s_scr[h] = (s * e_last[:, lanes] + _dot_tn(vh, k_mod[:, lanes] * e_rest[:, lanes])
                    - _dot_tn(u, ka * e_rest[:, lanes]))
        o_ref[:, lanes] = _rwkv_finish(o, r[:, lanes], k_mod[:, lanes], vh, g[:, lanes],
                                       rk_ref[:, lanes], lnw_ref[:, lanes], lnb_ref[:, lanes])

    @pl.when(c == pl.num_programs(1) - 1)
    def _():
        sfin_ref[0] = s_scr[...]


def _rwkv_params(p, d_b):
    row = lambda x: x.reshape(1, -1)
    return (row(p['mu']), row(p['w0']), p['w_up'], row(p['a0']), p['a_up'], p['g_up'],
            row(p['k_k']), row(p['k_a']), row(p['r_k']), row(p['ln_w']), row(p['ln_b']))


def rwkv_chunked(pb, shift0, s0, p, batch, seq):
    m, nb = pb.shape
    n_heads = s0.shape[1]
    d_b = n_heads * N_B
    t = CHUNK
    assert seq % t == 0
    nc = seq // t
    params = _rwkv_params(p, d_b)
    const = lambda b, c: (0, 0)
    return pl.pallas_call(
        functools.partial(_rwkv_chunk_kernel, n_heads=n_heads),
        grid=(batch, nc),
        in_specs=[pl.BlockSpec((t, nb), lambda b, c: (b * nc + c, 0)),
                  pl.BlockSpec((1, 1, nb), lambda b, c: (b, 0, 0)),
                  pl.BlockSpec((1, n_heads, N_B, N_B), lambda b, c: (b, 0, 0, 0))]
                 + [pl.BlockSpec(x.shape, const) for x in params],
        out_specs=[pl.BlockSpec((t, d_b), lambda b, c: (b * nc + c, 0)),
                   pl.BlockSpec((1, n_heads, N_B, N_B), lambda b, c: (b, 0, 0, 0))],
        out_shape=[jax.ShapeDtypeStruct((m, d_b), F32),
                   jax.ShapeDtypeStruct((batch, n_heads, N_B, N_B), F32)],
        scratch_shapes=[pltpu.VMEM((SUBLANES, nb), F32), pltpu.VMEM((n_heads, N_B, N_B), F32)],
        compiler_params=_cparams("arbitrary", "arbitrary"),
        name="rwkv_chunked",
    )(pb, shift0.reshape(batch, 1, nb), s0, *params)


def _col_from_row(x_row, eye):
    return jnp.sum(eye * x_row, axis=-1, keepdims=True)


def _row_from_col(x_col, eye):
    return jnp.sum(eye * x_col, axis=0, keepdims=True)


def _rwkv_step_kernel(pb_ref, shift0_ref, s0_ref, mu_ref, w0_ref, wup_ref, a0_ref, aup_ref, gup_ref,
                      kk_ref, ka_ref, rk_ref, lnw_ref, lnb_ref, o_ref, s_ref, *, n_heads):
    d_b = n_heads * N_B
    r, k_mod, v, kk_raw, a, g, log_decay = _rwkv_tokens(
        pb_ref[0], shift0_ref[0], mu_ref[...], w0_ref[...], wup_ref[...], a0_ref[...], aup_ref[...],
        gup_ref[...], kk_ref[...], ka_ref[...], d_b)
    decay = jnp.exp(log_decay)
    eye = (_iota2((N_B, N_B), 0) == _iota2((N_B, N_B), 1)).astype(F32)
    for h in range(n_heads):
        lanes = slice(h * N_B, (h + 1) * N_B)
        kk = _l2(kk_raw[:, lanes])
        s = s0_ref[0, h]
        sa = jnp.sum(s * kk, axis=-1, keepdims=True)
        v_col = _col_from_row(v[:, lanes], eye)
        s = s * decay[:, lanes] - sa * (kk * a[:, lanes]) + v_col * k_mod[:, lanes]
        s_ref[0, h] = s
        o = _row_from_col(jnp.sum(s * r[:, lanes], axis=-1, keepdims=True), eye)
        o_ref[0, :, lanes] = _rwkv_finish(o, r[:, lanes], k_mod[:, lanes], v[:, lanes], g[:, lanes],
                                          rk_ref[:, lanes], lnw_ref[:, lanes], lnb_ref[:, lanes])


def rwkv_step(pb, shift0, s0, p):
    b, nb = pb.shape
    n_heads = s0.shape[1]
    d_b = n_heads * N_B
    params = _rwkv_params(p, d_b)
    o, s = pl.pallas_call(
        functools.partial(_rwkv_step_kernel, n_heads=n_heads),
        grid=(b,),
        in_specs=[pl.BlockSpec((1, 1, nb), lambda i: (i, 0, 0)),
                  pl.BlockSpec((1, 1, nb), lambda i: (i, 0, 0)),
                  pl.BlockSpec((1, n_heads, N_B, N_B), lambda i: (i, 0, 0, 0))]
                 + [pl.BlockSpec(x.shape, lambda i: (0, 0)) for x in params],
        out_specs=[pl.BlockSpec((1, 1, d_b), lambda i: (i, 0, 0)),
                   pl.BlockSpec((1, n_heads, N_B, N_B), lambda i: (i, 0, 0, 0))],
        out_shape=[jax.ShapeDtypeStruct((b, 1, d_b), F32),
                   jax.ShapeDtypeStruct((b, n_heads, N_B, N_B), F32)],
        compiler_params=_cparams("arbitrary"),
        name="rwkv_step",
    )(pb.reshape(b, 1, nb), shift0.reshape(b, 1, nb), s0, *params)
    return o.reshape(b, d_b), s


def _gdn_gates(gates, alog_row, dtb_row):
    beta = _sigmoid(gates)
    g = -jnp.exp(alog_row) * _softplus(gates + dtb_row)
    return beta, g


def _gdn_chunk_kernel(qkv_ref, z_ref, gates_ref, conv0_ref, s0_ref, cw_ref, alog_ref, dtb_ref, onorm_ref,
                      o_ref, sfin_ref, carry_scr, s_scr, *, n_heads):
    c = pl.program_id(1)
    x = qkv_ref[...]
    t = x.shape[0]
    dqk = n_heads * DK_C

    @pl.when(c == 0)
    def _():
        carry_scr[...] = jnp.zeros_like(carry_scr)
        carry_scr[SUBLANES - (DN_CONV_W - 1):SUBLANES, :] = conv0_ref[0]
        s_scr[...] = s0_ref[0]

    carry = carry_scr[...]
    y = cw_ref[DN_CONV_W - 1:DN_CONV_W, :] * x
    for d in range(1, DN_CONV_W):
        y = y + cw_ref[DN_CONV_W - 1 - d:DN_CONV_W - d, :] * _shift_rows(x, carry, d)
    carry_scr[...] = x[t - SUBLANES:t]
    y = _silu(y)

    row = _iota2((t, t), 0)
    col = _iota2((t, t), 1)
    incl = row >= col
    strict = row > col
    beta_all, g_all = _gdn_gates(gates_ref[...], alog_ref[...], dtb_ref[...])
    gcum_all = _dot_mask_lhs(incl.astype(BF16), g_all)
    g3 = _split3(gcum_all)
    lane = _iota2((t, LANES), 1)
    z = z_ref[...]

    for h in range(n_heads):
        q = _l2(y[:, h * DK_C:(h + 1) * DK_C]) * (DK_C ** -0.5)
        k = _l2(y[:, dqk + h * DK_C:dqk + (h + 1) * DK_C])
        v = y[:, 2 * dqk + h * DV_C:2 * dqk + (h + 1) * DV_C]
        beta = beta_all[:, h:h + 1]
        gc = gcum_all[:, n_heads + h:n_heads + h + 1]
        pick = (lane == n_heads + h).astype(BF16)
        nt = lambda x3: lax.dot_general(pick, x3, (((1,), (1,)), ((), ())), preferred_element_type=F32)
        gc_row = nt(g3[0]) + (nt(g3[1]) + nt(g3[2]))
        lmask = jnp.where(incl, jnp.exp(jnp.where(incl, gc - gc_row, 0.0)), 0.0)
        kb = k * beta
        m = jnp.where(strict, _dot_nt(kb, k) * lmask, 0.0)
        tinv = _tri_inv(m)
        u = _dot_hi(tinv, v * beta)
        wmat = _dot_hi(tinv, kb * jnp.exp(gc))
        aqk = jnp.where(incl, _dot_nt(q, k) * lmask, 0.0)
        s = s_scr[h]
        v_new = u - _dot(wmat, s)
        o = _dot(q * jnp.exp(gc), s) + _dot(aqk, v_new)
        g_last = gc[t - 1:t, :]
        s_scr[h] = s * jnp.exp(g_last) + _dot_tn(k * jnp.exp(g_last - gc), v_new)
        o_ref[:, h * DV_C:(h + 1) * DV_C] = _rms(o, onorm_ref[...]) * _silu(z[:, h * DV_C:(h + 1) * DV_C])

    @pl.when(c == pl.num_programs(1) - 1)
    def _():
        sfin_ref[0] = s_scr[...]


def _gdn_gate_params(a_log, dt_bias, n_heads):
    pad = lambda x: jnp.zeros((1, LANES), F32).at[0, n_heads:2 * n_heads].set(x)
    return pad(a_log), pad(dt_bias)


def gdn_chunked(qkvz, gates, conv0, s0, cw, a_log, dt_bias, onorm, batch, seq):
    m = qkvz.shape[0]
    n_heads = s0.shape[1]
    dconv = cw.shape[1]
    dvv = n_heads * DV_C
    assert qkvz.shape[1] == dconv + dvv and dconv % dvv == 0
    t = CHUNK
    assert seq % t == 0
    nc = seq // t
    alog_row, dtb_row = _gdn_gate_params(a_log, dt_bias, n_heads)
    const = lambda b, c: (0, 0)
    return pl.pallas_call(
        functools.partial(_gdn_chunk_kernel, n_heads=n_heads),
        grid=(batch, nc),
        in_specs=[pl.BlockSpec((t, dconv), lambda b, c: (b * nc + c, 0)),
                  pl.BlockSpec((t, dvv), lambda b, c: (b * nc + c, dconv // dvv)),
                  pl.BlockSpec((t, LANES), lambda b, c: (b * nc + c, 0)),
                  pl.BlockSpec((1, DN_CONV_W - 1, dconv), lambda b, c: (b, 0, 0)),
                  pl.BlockSpec((1, n_heads, DK_C, DV_C), lambda b, c: (b, 0, 0, 0)),
                  pl.BlockSpec((DN_CONV_W, dconv), const),
                  pl.BlockSpec((1, LANES), const),
                  pl.BlockSpec((1, LANES), const),
                  pl.BlockSpec((1, DV_C), const)],
        out_specs=[pl.BlockSpec((t, dvv), lambda b, c: (b * nc + c, 0)),
                   pl.BlockSpec((1, n_heads, DK_C, DV_C), lambda b, c: (b, 0, 0, 0))],
        out_shape=[jax.ShapeDtypeStruct((m, dvv), F32),
                   jax.ShapeDtypeStruct((batch, n_heads, DK_C, DV_C), F32)],
        scratch_shapes=[pltpu.VMEM((SUBLANES, dconv), F32), pltpu.VMEM((n_heads, DK_C, DV_C), F32)],
        compiler_params=_cparams("arbitrary", "arbitrary"),
        name="gdn_chunked",
    )(qkvz, qkvz, gates, conv0, s0, cw, alog_row, dtb_row, onorm.reshape(1, DV_C))


def _gdn_step_kernel(qkv_ref, z_ref, gates_ref, conv0_ref, s0_ref, cw_ref, alog_ref, dtb_ref, onorm_ref,
                     o_ref, s_ref, *, n_heads):
    dqk = n_heads * DK_C
    x = qkv_ref[0]
    y = cw_ref[DN_CONV_W - 1:DN_CONV_W, :] * x
    for i in range(DN_CONV_W - 1):
        y = y + cw_ref[i:i + 1, :] * conv0_ref[0, i:i + 1, :]
    y = _silu(y)
    beta_all, g_all = _gdn_gates(gates_ref[0], alog_ref[...], dtb_ref[...])
    z = z_ref[0]
    eye = (_iota2((DK_C, DK_C), 0) == _iota2((DK_C, DK_C), 1)).astype(F32)
    for h in range(n_heads):
        q = _l2(y[:, h * DK_C:(h + 1) * DK_C]) * (DK_C ** -0.5)
        k = _l2(y[:, dqk + h * DK_C:dqk + (h + 1) * DK_C])
        v = y[:, 2 * dqk + h * DV_C:2 * dqk + (h + 1) * DV_C]
        beta = beta_all[:, h:h + 1]
        decay = jnp.exp(g_all[:, n_heads + h:n_heads + h + 1])
        s = s0_ref[0, h]
        k_col = _col_from_row(k, eye)
        q_col = _col_from_row(q, eye)
        v_new = beta * v - jnp.sum((k_col * (beta * decay)) * s, axis=0, keepdims=True)
        o = (jnp.sum((q_col * decay) * s, axis=0, keepdims=True)
             + jnp.sum(q * k, axis=-1, keepdims=True) * v_new)
        s_ref[0, h] = s * decay + k_col * v_new
        o_ref[0, :, h * DV_C:(h + 1) * DV_C] = (_rms(o, onorm_ref[...])
                                                * _silu(z[:, h * DV_C:(h + 1) * DV_C]))


def gdn_step(qkvz, gates, conv0, s0, cw, a_log, dt_bias, onorm):
    b = qkvz.shape[0]
    n_heads = s0.shape[1]
    dconv = cw.shape[1]
    dvv = n_heads * DV_C
    alog_row, dtb_row = _gdn_gate_params(a_log, dt_bias, n_heads)
    qkvz3 = qkvz.reshape(b, 1, dconv + dvv)
    const = lambda i: (0, 0)
    o, s = pl.pallas_call(
        functools.partial(_gdn_step_kernel, n_heads=n_heads),
        grid=(b,),
        in_specs=[pl.BlockSpec((1, 1, dconv), lambda i: (i, 0, 0)),
                  pl.BlockSpec((1, 1, dvv), lambda i: (i, 0, dconv // dvv)),
                  pl.BlockSpec((1, 1, LANES), lambda i: (i, 0, 0)),
                  pl.BlockSpec((1, DN_CONV_W - 1, dconv), lambda i: (i, 0, 0)),
                  pl.BlockSpec((1, n_heads, DK_C, DV_C), lambda i: (i, 0, 0, 0)),
                  pl.BlockSpec((DN_CONV_W, dconv), const),
                  pl.BlockSpec((1, LANES), const),
                  pl.BlockSpec((1, LANES), const),
                  pl.BlockSpec((1, DV_C), const)],
        out_specs=[pl.BlockSpec((1, 1, dvv), lambda i: (i, 0, 0)),
                   pl.BlockSpec((1, n_heads, DK_C, DV_C), lambda i: (i, 0, 0, 0))],
        out_shape=[jax.ShapeDtypeStruct((b, 1, dvv), F32),
                   jax.ShapeDtypeStruct((b, n_heads, DK_C, DV_C), F32)],
        compiler_params=_cparams("arbitrary"),
        name="gdn_step",
    )(qkvz3, qkvz3, gates.reshape(b, 1, LANES), conv0, s0, cw, alog_row, dtb_row,
      onorm.reshape(1, DV_C))
    return o.reshape(b, dvv), s


def _row_tile(m, want):
    return want if m % want == 0 else m


def _trunk(x, kv, wkv0, shift0, dn0, dnconv0, ffnconv0, p):
    b, t, d = x.shape
    m = b * t
    depth = p['norm_ffn'].shape[0]
    d_a = p['sb_bias'].shape[1] * DH_A
    nb = p['mu_b'].shape[1]
    dconv = p['conv_dn'].shape[2]
    n_heads_c = p['a_log_dn'].shape[1]
    dvv = n_heads_c * DV_C
    tm = _row_tile(m, 256)
    new_k, new_v, new_wkv, new_shift, new_dn, new_dnconv, new_ffn = [], [], [], [], [], [], []
    x2 = x.reshape(m, d)
    for l in range(depth):
        i = l // 2
        if l % 2 == 0:
            qkv, pb = norm_matmul(x2, p['norm_mix_even'][i], p['w_in_even'][i], (3 * d_a, nb), tm)
            rp = dict(mu=p['mu_b'][i], w0=p['w0_b'][i], w_up=p['w_up_b'][i], a0=p['a0_b'][i],
                      a_up=p['a_up_b'][i], g_up=p['g_up_b'][i], k_k=p['kk_b'][i], k_a=p['ka_b'][i],
                      r_k=p['rk_b'][i], ln_w=p['lnx_w_b'][i], ln_b=p['lnx_b_b'][i])
            if kv is None:
                o_a, kn = sb_prompt(qkv, p['qnorm_a'][i], p['knorm_a'][i], p['sb_bias'][i], b, t)
                o_b, wkv = rwkv_chunked(pb, shift0[i], wkv0[i], rp, b, t)
                shift = pb.reshape(b, t, nb)[:, t - 1]
            else:
                o_a, kn = sb_decode(qkv, kv[0], kv[1], kv[2], i, p['qnorm_a'][i], p['knorm_a'][i],
                                    p['sb_bias'][i])
                o_b, wkv = rwkv_step(pb, shift0[i], wkv0[i], rp)
                shift = pb
            w_out = p['w_out_even'][i]
            x2 = matmul_res([o_a, o_b], [w_out[:d_a], w_out[d_a:]], x2, tm)
            new_k.append(kn.reshape(b, t, d_a // DH_A, DH_A))
            new_v.append(qkv[:, 2 * d_a:].reshape(b, t, d_a // DH_A, DH_A))
            new_wkv.append(wkv)
            new_shift.append(shift)
        else:
            qkvz, gates = norm_matmul(x2, p['norm_mix_odd'][i], p['w_in_odd'][i], (dconv + dvv, LANES), tm)
            if kv is None:
                o_c, s_fin = gdn_chunked(qkvz, gates, dnconv0[i], dn0[i], p['conv_dn'][i], p['a_log_dn'][i],
                                         p['dt_bias_dn'][i], p['onorm_dn'][i], b, t)
                cbuf = qkvz.reshape(b, t, dconv + dvv)[:, t - (DN_CONV_W - 1):, :dconv]
            else:
                o_c, s_fin = gdn_step(qkvz, gates, dnconv0[i], dn0[i], p['conv_dn'][i], p['a_log_dn'][i],
                                      p['dt_bias_dn'][i], p['onorm_dn'][i])
                cbuf = jnp.concatenate([dnconv0[i][:, 1:], qkvz[:, None, :dconv]], axis=1)
            x2 = matmul_res([o_c], [p['w_out_odd'][i]], x2, tm)
            new_dn.append(s_fin)
            new_dnconv.append(cbuf)
        ffn_args = (p['norm_ffn'][l], p['w_gate'][l], p['w_up'][l], p['conv_ffn'][l], p['conv_ffn_b'][l],
                    p['w_down'][l])
        if kv is None:
            x3, fbuf = conv_ffn(x2.reshape(b, t, d), ffnconv0[l], *ffn_args, tm=_row_tile(t, 256))
            x2 = x3.reshape(m, d)
        else:
            x2, fbuf = conv_ffn_step(x2, ffnconv0[l], *ffn_args)
        new_ffn.append(fbuf)
    states = tuple(jnp.stack(s) for s in (new_k, new_v, new_wkv, new_shift, new_dn, new_dnconv, new_ffn))
    return x2.reshape(b, t, d), states


def kernel(x_prompt, x_sample, cache_k, cache_v, page_table, state_wkv, state_shift, state_dn,
           state_dn_conv, state_ffn_conv, norm_mix_even, w_in_even, qnorm_a, knorm_a, sb_bias,
           mu_b, w0_b, w_up_b, a0_b, a_up_b, g_up_b, kk_b, ka_b, rk_b, lnx_w_b, lnx_b_b,
           w_out_even, norm_mix_odd, w_in_odd, conv_dn, a_log_dn, dt_bias_dn, onorm_dn, w_out_odd,
           norm_ffn, w_gate, w_up, conv_ffn, conv_ffn_b, w_down):
    n_heads_c = a_log_dn.shape[1]
    n_gate_cols = 2 * n_heads_c
    assert x_sample.shape[1] == 1 and n_gate_cols <= LANES
    w_in_odd_p = jnp.pad(w_in_odd, ((0, 0), (0, 0), (0, LANES - n_gate_cols)))
    bf = lambda w: w.astype(BF16)
    p = dict(norm_mix_even=norm_mix_even, w_in_even=bf(w_in_even), qnorm_a=qnorm_a, knorm_a=knorm_a,
             sb_bias=sb_bias, mu_b=mu_b, w0_b=w0_b, w_up_b=bf(w_up_b), a0_b=a0_b, a_up_b=bf(a_up_b),
             g_up_b=bf(g_up_b), kk_b=kk_b, ka_b=ka_b, rk_b=rk_b, lnx_w_b=lnx_w_b, lnx_b_b=lnx_b_b,
             w_out_even=bf(w_out_even), norm_mix_odd=norm_mix_odd, w_in_odd=bf(w_in_odd_p),
             conv_dn=conv_dn, a_log_dn=a_log_dn, dt_bias_dn=dt_bias_dn, onorm_dn=onorm_dn,
             w_out_odd=bf(w_out_odd), norm_ffn=norm_ffn, w_gate=bf(w_gate), w_up=bf(w_up),
             conv_ffn=conv_ffn, conv_ffn_b=conv_ffn_b, w_down=bf(w_down))
    b = x_prompt.shape[0]
    n_ab = state_wkv.shape[0]
    n_c = state_dn.shape[0]
    depth = norm_ffn.shape[0]
    zeros = lambda like, lead: jnp.zeros((lead, b) + like.shape[2:], F32)
    y_prompt, ps = _trunk(x_prompt, None, zeros(state_wkv, n_ab), zeros(state_shift, n_ab),
                          zeros(state_dn, n_c), zeros(state_dn_conv, n_c), zeros(state_ffn_conv, depth), p)
    y_sample, ss = _trunk(x_sample, (cache_k, cache_v, page_table), state_wkv, state_shift, state_dn,
                          state_dn_conv, state_ffn_conv, p)
    return (y_prompt, y_sample, ps[0], ps[1], ss[0], ss[1], ps[2], ss[2], ps[3], ss[3],
            ps[4], ss[4], ps[5], ss[5], ps[6], ss[6])
```

```python
import functools

import jax
import jax.numpy as jnp
from jax import lax
from jax.experimental import pallas as pl
from jax.experimental.pallas import tpu as pltpu

F32 = jnp.float32
BF16 = jnp.bfloat16

RMS_EPS = 1e-6
L2_EPS = 1e-6
GN_EPS = 64e-5

DH_A = 64
N_B = 64
R_W, R_A, R_G = 64, 64, 128
DK_C = 128
DV_C = 128
DN_CONV_W = 4
FFN_CONV_W = 3

SB_BLOCK = 128
CHUNK = 64
SUBLANES = 8
LANES = 128
VMEM_LIMIT_BYTES = 56 * 1024 * 1024


def _cparams(*sem):
    return pltpu.CompilerParams(dimension_semantics=sem, vmem_limit_bytes=VMEM_LIMIT_BYTES)


def _dot(a, b):
    return jnp.dot(a.astype(BF16), b.astype(BF16), preferred_element_type=F32)


def _dot_nt(a, b):
    return lax.dot_general(a.astype(BF16), b.astype(BF16), (((1,), (1,)), ((), ())),
                           preferred_element_type=F32)


def _dot_tn(a, b):
    return lax.dot_general(a.astype(BF16), b.astype(BF16), (((0,), (0,)), ((), ())),
                           preferred_element_type=F32)


def _split2(a):
    hi = a.astype(BF16)
    lo = (a - hi.astype(F32)).astype(BF16)
    return hi, lo


def _split3(a):
    hi = a.astype(BF16)
    r1 = a - hi.astype(F32)
    mid = r1.astype(BF16)
    lo = (r1 - mid.astype(F32)).astype(BF16)
    return hi, mid, lo


def _dot_hi(a, b):
    ah, al = _split2(a)
    bh, bl = _split2(b)
    d = functools.partial(jnp.dot, preferred_element_type=F32)
    return d(ah, bh) + (d(ah, bl) + d(al, bh))


def _dot_mask_lhs(mask_bf16, x):
    d = functools.partial(jnp.dot, preferred_element_type=F32)
    h, m, l = _split3(x)
    return d(mask_bf16, h) + (d(mask_bf16, m) + d(mask_bf16, l))


def _dot_mask_rhs(x, mask_bf16):
    d = functools.partial(jnp.dot, preferred_element_type=F32)
    h, m, l = _split3(x)
    return d(h, mask_bf16) + (d(m, mask_bf16) + d(l, mask_bf16))


def _softplus(x):
    return jnp.maximum(x, 0.0) + jnp.log1p(jnp.exp(-jnp.abs(x)))


def _sigmoid(x):
    return 1.0 / (1.0 + jnp.exp(-x))


def _silu(x):
    return x * _sigmoid(x)


def _rms(x, g):
    return x * lax.rsqrt(jnp.mean(x * x, axis=-1, keepdims=True) + RMS_EPS) * g


def _l2(x):
    return x * lax.rsqrt(jnp.sum(x * x, axis=-1, keepdims=True) + L2_EPS)


def _iota2(shape, dim):
    return lax.broadcasted_iota(jnp.int32, shape, dim)


def _shift_rows(cur, carry, d):
    rolled = pltpu.roll(cur, d, 0)
    crolled = pltpu.roll(carry, d, 0)
    first = jnp.where(_iota2(carry.shape, 0) < d, crolled, rolled[:SUBLANES])
    if cur.shape[0] == SUBLANES:
        return first
    return jnp.concatenate([first, rolled[SUBLANES:]], axis=0)


def _tri_inv_multi(a_list):
    n = a_list[0].shape[0]
    row = _iota2((n, n), 0)
    col = _iota2((n, n), 1)
    eye = (row == col).astype(F32)
    ts = [eye - jnp.where((row >> 1) == (col >> 1), a, 0.0) for a in a_list]
    s = 2
    while s < n:
        sh = s.bit_length() - 1
        sub = ((row >> (sh + 1)) == (col >> (sh + 1))) & ((row >> sh) != (col >> sh))
        tes = [_dot_hi(t, jnp.where(sub, a, 0.0)) for t, a in zip(ts, a_list)]
        ts = [t - _dot_hi(te, t) for t, te in zip(ts, tes)]
        s *= 2
    return ts


def _norm_matmul_kernel(x_ref, g_ref, w_ref, *o_refs, splits):
    h = _rms(x_ref[...], g_ref[...]).astype(BF16)
    y = jnp.dot(h, w_ref[...], preferred_element_type=F32)
    off = 0
    for o_ref, n in zip(o_refs, splits):
        o_ref[...] = y[:, off:off + n]
        off += n


def norm_matmul(x, g, w, splits, tm):
    m, k = x.shape
    n = w.shape[1]
    assert sum(splits) == n and m % tm == 0
    return pl.pallas_call(
        functools.partial(_norm_matmul_kernel, splits=tuple(splits)),
        grid=(m // tm,),
        in_specs=[pl.BlockSpec((tm, k), lambda i: (i, 0)),
                  pl.BlockSpec((1, k), lambda i: (0, 0)),
                  pl.BlockSpec((k, n), lambda i: (0, 0))],
        out_specs=[pl.BlockSpec((tm, s), lambda i: (i, 0)) for s in splits],
        out_shape=[jax.ShapeDtypeStruct((m, s), F32) for s in splits],
        compiler_params=_cparams("arbitrary"),
        name="norm_matmul",
    )(x, g.reshape(1, k), w)


def _matmul_res_kernel(*refs, n_in):
    a_refs = refs[:n_in]
    w_refs = refs[n_in:2 * n_in]
    res_ref = refs[2 * n_in]
    o_ref = refs[2 * n_in + 1]
    acc = res_ref[...]
    for a_ref, w_ref in zip(a_refs, w_refs):
        acc = acc + jnp.dot(a_ref[...].astype(BF16), w_ref[...], preferred_element_type=F32)
    o_ref[...] = acc


def matmul_res(a_list, w_list, res, tm):
    m, n = res.shape
    assert m % tm == 0
    n_in = len(a_list)
    in_specs = ([pl.BlockSpec((tm, a.shape[1]), lambda i: (i, 0)) for a in a_list]
                + [pl.BlockSpec(w.shape, lambda i: (0, 0)) for w in w_list]
                + [pl.BlockSpec((tm, n), lambda i: (i, 0))])
    return pl.pallas_call(
        functools.partial(_matmul_res_kernel, n_in=n_in),
        grid=(m // tm,),
        in_specs=in_specs,
        out_specs=pl.BlockSpec((tm, n), lambda i: (i, 0)),
        out_shape=jax.ShapeDtypeStruct((m, n), F32),
        compiler_params=_cparams("arbitrary"),
        name="matmul_res",
    )(*a_list, *w_list, res)


def _ffn_kernel(x_ref, buf_ref, g_ref, wg_ref, wu_ref, cw_ref, cb_ref, wd_ref,
                y_ref, tail_ref, carry_scr):
    ti = pl.program_id(1)
    x = x_ref[0]
    tm = x.shape[0]
    h = _rms(x, g_ref[...]).astype(BF16)
    gate = jnp.dot(h, wg_ref[...], preferred_element_type=F32)
    up = jnp.dot(h, wu_ref[...], preferred_element_type=F32)

    @pl.when(ti == 0)
    def _():
        carry_scr[...] = jnp.zeros_like(carry_scr)
        carry_scr[SUBLANES - 2:SUBLANES, :] = buf_ref[0]

    carry = carry_scr[...]
    conv = (cw_ref[2:3, :] * gate + cw_ref[1:2, :] * _shift_rows(gate, carry, 1)
            + cw_ref[0:1, :] * _shift_rows(gate, carry, 2))
    carry_scr[...] = gate[tm - SUBLANES:tm]
    tail_ref[0] = gate[tm - 2:tm]
    act = (_silu(conv + cb_ref[...]) * up).astype(BF16)
    y_ref[0] = x + jnp.dot(act, wd_ref[...], preferred_element_type=F32)


def conv_ffn(x, buf, g, wg, wu, cw, cb, wd, tm):
    b, t, d = x.shape
    f = wg.shape[1]
    assert t % tm == 0 and tm % SUBLANES == 0
    const = lambda bi, ti: (0, 0)
    return pl.pallas_call(
        _ffn_kernel,
        grid=(b, t // tm),
        in_specs=[pl.BlockSpec((1, tm, d), lambda bi, ti: (bi, ti, 0)),
                  pl.BlockSpec((1, 2, f), lambda bi, ti: (bi, 0, 0)),
                  pl.BlockSpec((1, d), const),
                  pl.BlockSpec((d, f), const),
                  pl.BlockSpec((d, f), const),
                  pl.BlockSpec((FFN_CONV_W, f), const),
                  pl.BlockSpec((1, f), const),
                  pl.BlockSpec((f, d), const)],
        out_specs=[pl.BlockSpec((1, tm, d), lambda bi, ti: (bi, ti, 0)),
                   pl.BlockSpec((1, 2, f), lambda bi, ti: (bi, 0, 0))],
        out_shape=[jax.ShapeDtypeStruct((b, t, d), F32),
                   jax.ShapeDtypeStruct((b, 2, f), F32)],
        scratch_shapes=[pltpu.VMEM((SUBLANES, f), F32)],
        compiler_params=_cparams("arbitrary", "arbitrary"),
        name="conv_ffn",
    )(x, buf, g.reshape(1, d), wg, wu, cw, cb.reshape(1, f), wd)


def _ffn_step_kernel(x_ref, buf0_ref, buf1_ref, g_ref, wg_ref, wu_ref, cw_ref, cb_ref, wd_ref,
                     y_ref, gate_ref):
    x = x_ref[...]
    h = _rms(x, g_ref[...]).astype(BF16)
    gate = jnp.dot(h, wg_ref[...], preferred_element_type=F32)
    up = jnp.dot(h, wu_ref[...], preferred_element_type=F32)
    conv = cw_ref[2:3, :] * gate + cw_ref[1:2, :] * buf1_ref[...] + cw_ref[0:1, :] * buf0_ref[...]
    act = (_silu(conv + cb_ref[...]) * up).astype(BF16)
    y_ref[...] = x + jnp.dot(act, wd_ref[...], preferred_element_type=F32)
    gate_ref[...] = gate


def conv_ffn_step(x, buf, g, wg, wu, cw, cb, wd):
    b, d = x.shape
    f = wg.shape[1]
    full = lambda shape: pl.BlockSpec(shape, lambda i: (0,) * len(shape))
    y, gate = pl.pallas_call(
        _ffn_step_kernel,
        grid=(1,),
        in_specs=[full((b, d)), full((b, f)), full((b, f)), full((1, d)), full((d, f)), full((d, f)),
                  full((FFN_CONV_W, f)), full((1, f)), full((f, d))],
        out_specs=[full((b, d)), full((b, f))],
        out_shape=[jax.ShapeDtypeStruct((b, d), F32), jax.ShapeDtypeStruct((b, f), F32)],
        compiler_params=_cparams("arbitrary"),
        name="conv_ffn_step",
    )(x, buf[:, 0], buf[:, 1], g.reshape(1, d), wg, wu, cw, cb.reshape(1, f), wd)
    return y, jnp.stack([buf[:, 1], gate], axis=1)


def _pair_rms(x, w_row):
    left = _iota2(x.shape, 1) < DH_A
    x2 = x * x
    ms_l = jnp.sum(jnp.where(left, x2, 0.0), axis=-1, keepdims=True) * (1.0 / DH_A)
    ms_r = jnp.sum(jnp.where(left, 0.0, x2), axis=-1, keepdims=True) * (1.0 / DH_A)
    inv = jnp.where(left, lax.rsqrt(ms_l + RMS_EPS), lax.rsqrt(ms_r + RMS_EPS))
    return x * inv * w_row


def _sb_prompt_kernel(bias_ref, q_ref, k_ref, v_ref, qw_ref, kw_ref, o_ref, kn_ref,
                      kn_scr, v_scr, q_scr, later_scr, o_scr, tail_scr, *, n_heads):
    qi = pl.program_id(1)
    blk = q_ref.shape[0]
    n_pairs = n_heads // 2
    pw = 2 * DH_A
    scale = DH_A ** -0.5
    row0 = pl.multiple_of(qi * blk, blk)

    k = k_ref[...]
    kn = jnp.concatenate([_pair_rms(k[:, p * pw:(p + 1) * pw], kw_ref[...]) for p in range(n_pairs)],
                         axis=-1)
    kn_ref[...] = kn
    kn_scr[pl.ds(row0, blk), :] = kn.astype(BF16)
    v_scr[pl.ds(row0, blk), :] = v_ref[...].astype(BF16)

    q = q_ref[...]
    left = _iota2((blk, pw), 1) < DH_A
    for p in range(n_pairs):
        qn = _pair_rms(q[:, p * pw:(p + 1) * pw], qw_ref[...]) * scale
        q_scr[p] = jnp.concatenate([jnp.where(left, qn, 0.0), jnp.where(left, 0.0, qn)],
                                   axis=0).astype(BF16)
    krow = _iota2((blk, 2 * blk), 0)
    kcol = _iota2((blk, 2 * blk), 1)
    later_scr[...] = jnp.where((kcol >= blk) | (krow > kcol), 1.0, 0.0).astype(BF16)
    o_scr[...] = jnp.zeros_like(o_scr)
    tail_scr[...] = jnp.zeros_like(tail_scr)

    qrow = _iota2((2 * blk, blk), 0)
    causal = _iota2((2 * blk, blk), 1) < jnp.where(qrow >= blk, qrow - blk, qrow)
    vleft = _iota2((blk, pw), 1) < DH_A

    def block(off, diag):
        pairs = range(n_pairs)
        zs = []
        for p in pairs:
            z2 = lax.dot_general(q_scr[p], kn_scr[pl.ds(off, blk), p * pw:(p + 1) * pw],
                                 (((1,), (1,)), ((), ())), preferred_element_type=F32)
            zs.append(jnp.concatenate([z2[:blk] + bias_ref[2 * p], z2[blk:] + bias_ref[2 * p + 1]],
                                      axis=0))
        sps = [jnp.maximum(z, 0.0) + jnp.log(1.0 + jnp.exp(-jnp.abs(z))) for z in zs]
        if diag:
            sps = [jnp.where(causal, sp, 0.0) for sp in sps]
        his = [sp.astype(BF16) for sp in sps]
        los = [(sp - hi.astype(F32)).astype(BF16) for sp, hi in zip(sps, his)]
        lo_mat = later_scr[...]
        cums = [jnp.dot(hi, lo_mat, preferred_element_type=F32)
                + jnp.dot(lo, lo_mat, preferred_element_type=F32) for hi, lo in zip(his, los)]
        for p in pairs:
            tail = tail_scr[p]
            w = jnp.exp((zs[p] - sps[p]) - (cums[p][:, :blk] + tail))
            if diag:
                w = jnp.where(causal, w, 0.0)
            tail_scr[p] = tail + cums[p][:, blk:]
            w = w.astype(BF16)
            vj = v_scr[pl.ds(off, blk), p * pw:(p + 1) * pw]
            zero = jnp.zeros_like(vj)
            v_bd = jnp.concatenate([jnp.where(vleft, vj, zero), jnp.where(vleft, zero, vj)], axis=0)
            o_scr[:, p * pw:(p + 1) * pw] += jnp.dot(
                jnp.concatenate([w[:blk], w[blk:]], axis=1), v_bd, preferred_element_type=F32)

    block(row0, True)

    def body(i, carry):
        block(pl.multiple_of((qi - i) * blk, blk), False)
        return carry

    lax.fori_loop(1, qi + 1, body, 0)
    o_ref[...] = o_scr[...]


def sb_prompt(qkv, qw, kw, bias, batch, seq):
    m, n3 = qkv.shape
    d_a = n3 // 3
    n_heads = d_a // DH_A
    blk = SB_BLOCK
    assert seq % blk == 0 and n_heads % 2 == 0 and 2 * DH_A == LANES
    nq = seq // blk
    rowblk = lambda col: (lambda b, qi: (b * nq + qi, col))
    pair_w = lambda w: jnp.tile(w, 2).reshape(1, 2 * DH_A)
    return pl.pallas_call(
        functools.partial(_sb_prompt_kernel, n_heads=n_heads),
        grid=(batch, nq),
        in_specs=[pl.BlockSpec(memory_space=pltpu.SMEM),
                  pl.BlockSpec((blk, d_a), rowblk(0)),
                  pl.BlockSpec((blk, d_a), rowblk(1)),
                  pl.BlockSpec((blk, d_a), rowblk(2)),
                  pl.BlockSpec((1, 2 * DH_A), lambda b, qi: (0, 0)),
                  pl.BlockSpec((1, 2 * DH_A), lambda b, qi: (0, 0))],
        out_specs=[pl.BlockSpec((blk, d_a), rowblk(0)),
                   pl.BlockSpec((blk, d_a), rowblk(0))],
        out_shape=[jax.ShapeDtypeStruct((m, d_a), F32), jax.ShapeDtypeStruct((m, d_a), F32)],
        scratch_shapes=[pltpu.VMEM((seq, d_a), BF16), pltpu.VMEM((seq, d_a), BF16),
                        pltpu.VMEM((n_heads // 2, 2 * blk, 2 * DH_A), BF16),
                        pltpu.VMEM((blk, 2 * blk), BF16),
                        pltpu.VMEM((blk, d_a), F32),
                        pltpu.VMEM((n_heads // 2, 2 * blk, blk), F32)],
        compiler_params=_cparams("arbitrary", "arbitrary"),
        name="sb_prompt",
    )(bias, qkv, qkv, qkv, pair_w(qw), pair_w(kw))


DECODE_BANK = 4
DECODE_SLOTS = 8


def _sb_decode_kernel(pt_ref, q_ref, k_ref, ck_hbm, cv_hbm, qw_ref, kw_ref, bias_ref,
                      o_ref, kn_ref, kbuf, vbuf, sem, *, n_heads, n_pages, layer):
    b = pl.program_id(0)
    page = kbuf.shape[3]
    scale = DH_A ** -0.5
    nt = lambda x, y: lax.dot_general(x, y, (((1,), (1,)), ((), ())), preferred_element_type=F32)

    def copies(i, slot):
        phys = pt_ref[b, n_pages - 1 - i]
        return (pltpu.make_async_copy(ck_hbm.at[layer, phys], kbuf.at[slot], sem.at[0, slot]),
                pltpu.make_async_copy(cv_hbm.at[layer, phys], vbuf.at[slot], sem.at[1, slot]))

    for j in range(DECODE_SLOTS):
        for cp in copies(j, j):
            cp.start()

    q = q_ref[0]
    k = k_ref[0]
    hrow = _iota2((n_heads, DH_A), 0)
    qe, kn = [], []
    for h in range(n_heads):
        lanes = slice(h * DH_A, (h + 1) * DH_A)
        qn = _rms(q[:, lanes], qw_ref[...]) * scale
        qe.append(jnp.where(hrow == h, jnp.broadcast_to(qn, (n_heads, DH_A)), 0.0).astype(BF16))
        kn.append(_rms(k[:, lanes], kw_ref[...]))
    kn_ref[0] = jnp.concatenate(kn, axis=-1)

    krow = _iota2((page, 2 * page), 0)
    kcol = _iota2((page, 2 * page), 1)
    later_ones = jnp.where((kcol >= page) | (krow > kcol), 1.0, 0.0).astype(BF16)
    bias = bias_ref[...]

    def bank_pages(first, bank, tail, acc):
        pages = range(DECODE_BANK)
        slots = [bank * DECODE_BANK + j for j in pages]
        for j in pages:
            for cp in copies(first + j, slots[j]):
                cp.wait()
        zs = []
        for j in pages:
            z = bias
            for h in range(n_heads):
                z = z + jnp.dot(qe[h], kbuf[slots[j], h].astype(BF16), preferred_element_type=F32)
            zs.append(z)
        sps = [jnp.maximum(z, 0.0) + jnp.log(1.0 + jnp.exp(-jnp.abs(z))) for z in zs]
        his = [sp.astype(BF16) for sp in sps]
        los = [(sp - hi.astype(F32)).astype(BF16) for sp, hi in zip(sps, his)]
        cums = [jnp.dot(hi, later_ones, preferred_element_type=F32)
                + jnp.dot(lo, later_ones, preferred_element_type=F32) for hi, lo in zip(his, los)]
        ws = []
        for j in pages:
            ws.append(jnp.exp((zs[j] - sps[j]) - (cums[j][:, :page] + tail)).astype(BF16))
            tail = tail + cums[j][:, page:]
        for j in pages:
            acc = [acc[h] + nt(ws[j], vbuf[slots[j], h].astype(BF16)) for h in range(n_heads)]

        @pl.when(first + DECODE_SLOTS < n_pages)
        def _():
            for j in pages:
                for cp in copies(first + DECODE_SLOTS + j, slots[j]):
                    cp.start()

        return tail, acc

    def group(gi, carry):
        tail, acc = carry
        for bank in range(DECODE_SLOTS // DECODE_BANK):
            tail, acc = bank_pages(gi * DECODE_SLOTS + bank * DECODE_BANK, bank, tail, acc)
        return tail, acc

    zero = jnp.zeros((n_heads, DH_A), F32)
    _, acc = lax.fori_loop(0, n_pages // DECODE_SLOTS, group,
                           (jnp.zeros((n_heads, page), F32), [zero] * n_heads))
    o_ref[0] = jnp.concatenate([acc[h][h:h + 1, :] for h in range(n_heads)], axis=-1)


def sb_decode(qkv, cache_k, cache_v, page_table, layer, qw, kw, bias):
    b, n3 = qkv.shape
    d_a = n3 // 3
    n_heads = d_a // DH_A
    n_layers, n_phys, page = cache_k.shape[:3]
    n_pages = page_table.shape[1]
    assert n_pages % DECODE_SLOTS == 0 and cache_k.shape[3] == n_heads
    ck_t = jnp.transpose(cache_k, (0, 1, 3, 4, 2))
    cv_t = jnp.transpose(cache_v, (0, 1, 3, 4, 2))
    qkv3 = qkv.reshape(b, 1, n3)
    grid_spec = pltpu.PrefetchScalarGridSpec(
        num_scalar_prefetch=1,
        grid=(b,),
        in_specs=[pl.BlockSpec((1, 1, d_a), lambda bi, pt: (bi, 0, 0)),
                  pl.BlockSpec((1, 1, d_a), lambda bi, pt: (bi, 0, 1)),
                  pl.BlockSpec(memory_space=pl.ANY),
                  pl.BlockSpec(memory_space=pl.ANY),
                  pl.BlockSpec((1, DH_A), lambda bi, pt: (0, 0)),
                  pl.BlockSpec((1, DH_A), lambda bi, pt: (0, 0)),
                  pl.BlockSpec((n_heads, 1), lambda bi, pt: (0, 0))],
        out_specs=[pl.BlockSpec((1, 1, d_a), lambda bi, pt: (bi, 0, 0)),
                   pl.BlockSpec((1, 1, d_a), lambda bi, pt: (bi, 0, 0))],
        scratch_shapes=[pltpu.VMEM((DECODE_SLOTS, n_heads, DH_A, page), F32),
                        pltpu.VMEM((DECODE_SLOTS, n_heads, DH_A, page), F32),
                        pltpu.SemaphoreType.DMA((2, DECODE_SLOTS))],
    )
    o, kn = pl.pallas_call(
        functools.partial(_sb_decode_kernel, n_heads=n_heads, n_pages=n_pages, layer=layer),
        grid_spec=grid_spec,
        out_shape=[jax.ShapeDtypeStruct((b, 1, d_a), F32), jax.ShapeDtypeStruct((b, 1, d_a), F32)],
        compiler_params=_cparams("arbitrary"),
        name="sb_decode",
    )(page_table, qkv3, qkv3, ck_t, cv_t, qw.reshape(1, DH_A), kw.reshape(1, DH_A),
      bias.reshape(n_heads, 1))
    return o.reshape(b, d_a), kn.reshape(b, d_a)


def _rwkv_tokens(pb, prev, mu, w0, w_up, a0, a_up, g_up, k_k, k_a, d_b):
    xs = pb + (prev - pb) * mu
    r = xs[:, 0:d_b]
    k = xs[:, d_b:2 * d_b]
    v = xs[:, 2 * d_b:3 * d_b]
    xw = xs[:, 3 * d_b:3 * d_b + R_W]
    xa = xs[:, 3 * d_b + R_W:3 * d_b + R_W + R_A]
    xg = xs[:, 3 * d_b + R_W + R_A:3 * d_b + R_W + R_A + R_G]
    w_log = -_softplus(-(w0 + _dot(jnp.tanh(xw), w_up))) - 0.5
    log_decay = -jnp.exp(w_log)
    a = _sigmoid(a0 + _dot(xa, a_up))
    g = _dot(_sigmoid(xg), g_up)
    kk_raw = k * k_k
    k_mod = k * (1.0 + (a - 1.0) * k_a)
    return r, k_mod, v, kk_raw, a, g, log_decay


def _rwkv_finish(o, r, k_mod, v, g, r_k, ln_w, ln_b):
    mean = jnp.mean(o, axis=-1, keepdims=True)
    var = jnp.mean(jnp.square(o - mean), axis=-1, keepdims=True)
    on = (o - mean) * lax.rsqrt(var + GN_EPS) * ln_w + ln_b
    bonus = jnp.sum(r * k_mod * r_k, axis=-1, keepdims=True) * v
    return (on + bonus) * g


def _rwkv_chunk_kernel(pb_ref, shift0_ref, s0_ref, mu_ref, w0_ref, wup_ref, a0_ref, aup_ref, gup_ref,
                       kk_ref, ka_ref, rk_ref, lnw_ref, lnb_ref, o_ref, sfin_ref,
                       carry_scr, s_scr, *, n_heads):
    c = pl.program_id(1)
    d_b = n_heads * N_B
    pb = pb_ref[...]
    t = pb.shape[0]

    @pl.when(c == 0)
    def _():
        carry_scr[...] = jnp.zeros_like(carry_scr)
        carry_scr[SUBLANES - 1:SUBLANES, :] = shift0_ref[0]
        s_scr[...] = s0_ref[0]

    prev = _shift_rows(pb, carry_scr[...], 1)
    carry_scr[...] = pb[t - SUBLANES:t]
    r, k_mod, v, kk_raw, a, g, log_decay = _rwkv_tokens(
        pb, prev, mu_ref[...], w0_ref[...], wup_ref[...], a0_ref[...], aup_ref[...], gup_ref[...],
        kk_ref[...], ka_ref[...], d_b)

    row = _iota2((t, t), 0)
    col = _iota2((t, t), 1)
    incl = row >= col
    strict = row > col
    cum = _dot_mask_lhs(incl.astype(BF16), log_decay)
    cum_prev = cum - log_decay
    cum_last = cum[t - 1:t, :]
    e_cum = jnp.exp(cum)
    e_prev = jnp.exp(cum_prev)
    e_neg = jnp.exp(-cum)
    e_rest = jnp.exp(cum_last - cum)
    e_last = jnp.exp(cum_last)

    heads = range(n_heads)
    sl = [slice(h * N_B, (h + 1) * N_B) for h in heads]
    kk = [_l2(kk_raw[:, l]) for l in sl]
    ka = [kk[h] * a[:, sl[h]] for h in heads]
    x = [jnp.concatenate([kk[h] * e_prev[:, sl[h]], r[:, sl[h]] * e_cum[:, sl[h]]], axis=0)
         for h in heads]
    khat = [k_mod[:, sl[h]] * e_neg[:, sl[h]] for h in heads]
    bhat = [ka[h] * e_neg[:, sl[h]] for h in heads]
    vh = [v[:, l] for l in sl]
    s = [s_scr[h] for h in heads]
    xb = [_dot_nt(x[h], bhat[h]) for h in heads]
    xk = [_dot_nt(x[h], khat[h]) for h in heads]
    xs = [_dot_nt(x[h], s[h]) for h in heads]
    tinv = _tri_inv_multi([jnp.where(strict, xb[h][:t], 0.0) for h in heads])
    rhs = [xs[h][:t] + _dot(jnp.where(strict, xk[h][:t], 0.0), vh[h]) for h in heads]
    u = [_dot_hi(tinv[h], rhs[h]) for h in heads]
    o = [xs[h][t:] + _dot(jnp.where(incl, xk[h][t:], 0.0), vh[h])
         - _dot(jnp.where(incl, xb[h][t:], 0.0), u[h]) for h in heads]
    for h in heads:
        s_scr[h] = s[h] * e_last[:, sl[h]] + _dot_tn(
            jnp.concatenate([vh[h], -u[h]], axis=0),
            jnp.concatenate([k_mod[:, sl[h]] * e_rest[:, sl[h]], ka[h] * e_rest[:, sl[h]]], axis=0))
    for h in heads:
        o_ref[:, sl[h]] = _rwkv_finish(o[h], r[:, sl[h]], k_mod[:, sl[h]], vh[h], g[:, sl[h]],
                                       rk_ref[:, sl[h]], lnw_ref[:, sl[h]], lnb_ref[:, sl[h]])

    @pl.when(c == pl.num_programs(1) - 1)
    def _():
        sfin_ref[0] = s_scr[...]


def _rwkv_params(p, d_b):
    row = lambda x: x.reshape(1, -1)
    return (row(p['mu']), row(p['w0']), p['w_up'], row(p['a0']), p['a_up'], p['g_up'],
            row(p['k_k']), row(p['k_a']), row(p['r_k']), row(p['ln_w']), row(p['ln_b']))


def rwkv_chunked(pb, shift0, s0, p, batch, seq):
    m, nb = pb.shape
    n_heads = s0.shape[1]
    d_b = n_heads * N_B
    t = CHUNK
    assert seq % t == 0
    nc = seq // t
    params = _rwkv_params(p, d_b)
    const = lambda b, c: (0, 0)
    return pl.pallas_call(
        functools.partial(_rwkv_chunk_kernel, n_heads=n_heads),
        grid=(batch, nc),
        in_specs=[pl.BlockSpec((t, nb), lambda b, c: (b * nc + c, 0)),
                  pl.BlockSpec((1, 1, nb), lambda b, c: (b, 0, 0)),
                  pl.BlockSpec((1, n_heads, N_B, N_B), lambda b, c: (b, 0, 0, 0))]
                 + [pl.BlockSpec(x.shape, const) for x in params],
        out_specs=[pl.BlockSpec((t, d_b), lambda b, c: (b * nc + c, 0)),
                   pl.BlockSpec((1, n_heads, N_B, N_B), lambda b, c: (b, 0, 0, 0))],
        out_shape=[jax.ShapeDtypeStruct((m, d_b), F32),
                   jax.ShapeDtypeStruct((batch, n_heads, N_B, N_B), F32)],
        scratch_shapes=[pltpu.VMEM((SUBLANES, nb), F32), pltpu.VMEM((n_heads, N_B, N_B), F32)],
        compiler_params=_cparams("arbitrary", "arbitrary"),
        name="rwkv_chunked",
    )(pb, shift0.reshape(batch, 1, nb), s0, *params)


def _col_from_row(x_row, eye):
    return jnp.sum(eye * x_row, axis=-1, keepdims=True)


def _row_from_col(x_col, eye):
    return jnp.sum(eye * x_col, axis=0, keepdims=True)


def _rwkv_step_kernel(pb_ref, shift0_ref, s0_ref, mu_ref, w0_ref, wup_ref, a0_ref, aup_ref, gup_ref,
                      kk_ref, ka_ref, rk_ref, lnw_ref, lnb_ref, o_ref, s_ref, *, n_heads):
    d_b = n_heads * N_B
    r, k_mod, v, kk_raw, a, g, log_decay = _rwkv_tokens(
        pb_ref[0], shift0_ref[0], mu_ref[...], w0_ref[...], wup_ref[...], a0_ref[...], aup_ref[...],
        gup_ref[...], kk_ref[...], ka_ref[...], d_b)
    decay = jnp.exp(log_decay)
    eye = (_iota2((N_B, N_B), 0) == _iota2((N_B, N_B), 1)).astype(F32)
    for h in range(n_heads):
        lanes = slice(h * N_B, (h + 1) * N_B)
        kk = _l2(kk_raw[:, lanes])
        s = s0_ref[0, h]
        sa = jnp.sum(s * kk, axis=-1, keepdims=True)
        v_col = _col_from_row(v[:, lanes], eye)
        s = s * decay[:, lanes] - sa * (kk * a[:, lanes]) + v_col * k_mod[:, lanes]
        s_ref[0, h] = s
        o = _row_from_col(jnp.sum(s * r[:, lanes], axis=-1, keepdims=True), eye)
        o_ref[0, :, lanes] = _rwkv_finish(o, r[:, lanes], k_mod[:, lanes], v[:, lanes], g[:, lanes],
                                          rk_ref[:, lanes], lnw_ref[:, lanes], lnb_ref[:, lanes])


def rwkv_step(pb, shift0, s0, p):
    b, nb = pb.shape
    n_heads = s0.shape[1]
    d_b = n_heads * N_B
    params = _rwkv_params(p, d_b)
    o, s = pl.pallas_call(
        functools.partial(_rwkv_step_kernel, n_heads=n_heads),
        grid=(b,),
        in_specs=[pl.BlockSpec((1, 1, nb), lambda i: (i, 0, 0)),
                  pl.BlockSpec((1, 1, nb), lambda i: (i, 0, 0)),
                  pl.BlockSpec((1, n_heads, N_B, N_B), lambda i: (i, 0, 0, 0))]
                 + [pl.BlockSpec(x.shape, lambda i: (0, 0)) for x in params],
        out_specs=[pl.BlockSpec((1, 1, d_b), lambda i: (i, 0, 0)),
                   pl.BlockSpec((1, n_heads, N_B, N_B), lambda i: (i, 0, 0, 0))],
        out_shape=[jax.ShapeDtypeStruct((b, 1, d_b), F32),
                   jax.ShapeDtypeStruct((b, n_heads, N_B, N_B), F32)],
        compiler_params=_cparams("arbitrary"),
        name="rwkv_step",
    )(pb.reshape(b, 1, nb), shift0.reshape(b, 1, nb), s0, *params)
    return o.reshape(b, d_b), s


def _gdn_gates(gates, alog_row, dtb_row):
    beta = _sigmoid(gates)
    g = -jnp.exp(alog_row) * _softplus(gates + dtb_row)
    return beta, g


def _gdn_chunk_kernel(qkv_ref, z_ref, gates_ref, conv0_ref, s0_ref, cw_ref, alog_ref, dtb_ref, onorm_ref,
                      o_ref, sfin_ref, carry_scr, s_scr, *, n_heads):
    c = pl.program_id(1)
    x = qkv_ref[...]
    t = x.shape[0]
    dqk = n_heads * DK_C

    @pl.when(c == 0)
    def _():
        carry_scr[...] = jnp.zeros_like(carry_scr)
        carry_scr[SUBLANES - (DN_CONV_W - 1):SUBLANES, :] = conv0_ref[0]
        s_scr[...] = s0_ref[0]

    carry = carry_scr[...]
    y = cw_ref[DN_CONV_W - 1:DN_CONV_W, :] * x
    for d in range(1, DN_CONV_W):
        y = y + cw_ref[DN_CONV_W - 1 - d:DN_CONV_W - d, :] * _shift_rows(x, carry, d)
    carry_scr[...] = x[t - SUBLANES:t]
    y = _silu(y)

    row = _iota2((t, t), 0)
    col = _iota2((t, t), 1)
    incl = row >= col
    strict = row > col
    beta_all, g_all = _gdn_gates(gates_ref[...], alog_ref[...], dtb_ref[...])
    gcum_all = _dot_mask_lhs(incl.astype(BF16), g_all)
    g3 = _split3(gcum_all)
    lane = _iota2((t, LANES), 1)
    z = z_ref[...]

    heads = range(n_heads)
    q = [_l2(y[:, h * DK_C:(h + 1) * DK_C]) * (DK_C ** -0.5) for h in heads]
    k = [_l2(y[:, dqk + h * DK_C:dqk + (h + 1) * DK_C]) for h in heads]
    v = [y[:, 2 * dqk + h * DV_C:2 * dqk + (h + 1) * DV_C] for h in heads]
    beta = [beta_all[:, h:h + 1] for h in heads]
    gc = [gcum_all[:, n_heads + h:n_heads + h + 1] for h in heads]
    egc = [jnp.exp(x) for x in gc]
    picks = [(lane == n_heads + h).astype(BF16) for h in heads]
    nt = lambda a, b: lax.dot_general(a, b, (((1,), (1,)), ((), ())), preferred_element_type=F32)
    gc_row = [nt(pk, g3[0]) + (nt(pk, g3[1]) + nt(pk, g3[2])) for pk in picks]
    lmask = [jnp.where(incl, jnp.exp(jnp.where(incl, gc[h] - gc_row[h], 0.0)), 0.0) for h in heads]
    kb = [k[h] * beta[h] for h in heads]
    kq = [_dot_nt(jnp.concatenate([kb[h], q[h]], axis=0), k[h]) for h in heads]
    tinv = _tri_inv_multi([jnp.where(strict, kq[h][:t] * lmask[h], 0.0) for h in heads])
    uw = [_dot_hi(tinv[h], jnp.concatenate([v[h] * beta[h], kb[h] * egc[h]], axis=1)) for h in heads]
    s = [s_scr[h] for h in heads]
    ws = [_dot(jnp.concatenate([uw[h][:, DV_C:], q[h] * egc[h]], axis=0), s[h]) for h in heads]
    v_new = [uw[h][:, :DV_C] - ws[h][:t] for h in heads]
    o = [ws[h][t:] + _dot(jnp.where(incl, kq[h][t:] * lmask[h], 0.0), v_new[h]) for h in heads]
    for h in heads:
        g_last = gc[h][t - 1:t, :]
        s_scr[h] = s[h] * jnp.exp(g_last) + _dot_tn(k[h] * jnp.exp(g_last - gc[h]), v_new[h])
    for h in heads:
        o_ref[:, h * DV_C:(h + 1) * DV_C] = (_rms(o[h], onorm_ref[...])
                                             * _silu(z[:, h * DV_C:(h + 1) * DV_C]))

    @pl.when(c == pl.num_programs(1) - 1)
    def _():
        sfin_ref[0] = s_scr[...]


def _gdn_gate_params(a_log, dt_bias, n_heads):
    pad = lambda x: jnp.zeros((1, LANES), F32).at[0, n_heads:2 * n_heads].set(x)
    return pad(a_log), pad(dt_bias)


def gdn_chunked(qkvz, gates, conv0, s0, cw, a_log, dt_bias, onorm, batch, seq):
    m = qkvz.shape[0]
    n_heads = s0.shape[1]
    dconv = cw.shape[1]
    dvv = n_heads * DV_C
    assert qkvz.shape[1] == dconv + dvv and dconv % dvv == 0
    t = CHUNK
    assert seq % t == 0
    nc = seq // t
    alog_row, dtb_row = _gdn_gate_params(a_log, dt_bias, n_heads)
    const = lambda b, c: (0, 0)
    return pl.pallas_call(
        functools.partial(_gdn_chunk_kernel, n_heads=n_heads),
        grid=(batch, nc),
        in_specs=[pl.BlockSpec((t, dconv), lambda b, c: (b * nc + c, 0)),
                  pl.BlockSpec((t, dvv), lambda b, c: (b * nc + c, dconv // dvv)),
                  pl.BlockSpec((t, LANES), lambda b, c: (b * nc + c, 0)),
                  pl.BlockSpec((1, DN_CONV_W - 1, dconv), lambda b, c: (b, 0, 0)),
                  pl.BlockSpec((1, n_heads, DK_C, DV_C), lambda b, c: (b, 0, 0, 0)),
                  pl.BlockSpec((DN_CONV_W, dconv), const),
                  pl.BlockSpec((1, LANES), const),
                  pl.BlockSpec((1, LANES), const),
                  pl.BlockSpec((1, DV_C), const)],
        out_specs=[pl.BlockSpec((t, dvv), lambda b, c: (b * nc + c, 0)),
                   pl.BlockSpec((1, n_heads, DK_C, DV_C), lambda b, c: (b, 0, 0, 0))],
        out_shape=[jax.ShapeDtypeStruct((m, dvv), F32),
                   jax.ShapeDtypeStruct((batch, n_heads, DK_C, DV_C), F32)],
        scratch_shapes=[pltpu.VMEM((SUBLANES, dconv), F32), pltpu.VMEM((n_heads, DK_C, DV_C), F32)],
        compiler_params=_cparams("arbitrary", "arbitrary"),
        name="gdn_chunked",
    )(qkvz, qkvz, gates, conv0, s0, cw, alog_row, dtb_row, onorm.reshape(1, DV_C))


def _gdn_step_kernel(qkv_ref, z_ref, gates_ref, conv0_ref, s0_ref, cw_ref, alog_ref, dtb_ref, onorm_ref,
                     o_ref, s_ref, *, n_heads):
    dqk = n_heads * DK_C
    x = qkv_ref[0]
    y = cw_ref[DN_CONV_W - 1:DN_CONV_W, :] * x
    for i in range(DN_CONV_W - 1):
        y = y + cw_ref[i:i + 1, :] * conv0_ref[0, i:i + 1, :]
    y = _silu(y)
    beta_all, g_all = _gdn_gates(gates_ref[0], alog_ref[...], dtb_ref[...])
    z = z_ref[0]
    eye = (_iota2((DK_C, DK_C), 0) == _iota2((DK_C, DK_C), 1)).astype(F32)
    for h in range(n_heads):
        q = _l2(y[:, h * DK_C:(h + 1) * DK_C]) * (DK_C ** -0.5)
        k = _l2(y[:, dqk + h * DK_C:dqk + (h + 1) * DK_C])
        v = y[:, 2 * dqk + h * DV_C:2 * dqk + (h + 1) * DV_C]
        beta = beta_all[:, h:h + 1]
        decay = jnp.exp(g_all[:, n_heads + h:n_heads + h + 1])
        s = s0_ref[0, h]
        k_col = _col_from_row(k, eye)
        q_col = _col_from_row(q, eye)
        v_new = beta * v - jnp.sum((k_col * (beta * decay)) * s, axis=0, keepdims=True)
        o = (jnp.sum((q_col * decay) * s, axis=0, keepdims=True)
             + jnp.sum(q * k, axis=-1, keepdims=True) * v_new)
        s_ref[0, h] = s * decay + k_col * v_new
        o_ref[0, :, h * DV_C:(h + 1) * DV_C] = (_rms(o, onorm_ref[...])
                                                * _silu(z[:, h * DV_C:(h + 1) * DV_C]))


def gdn_step(qkvz, gates, conv0, s0, cw, a_log, dt_bias, onorm):
    b = qkvz.shape[0]
    n_heads = s0.shape[1]
    dconv = cw.shape[1]
    dvv = n_heads * DV_C
    alog_row, dtb_row = _gdn_gate_params(a_log, dt_bias, n_heads)
    qkvz3 = qkvz.reshape(b, 1, dconv + dvv)
    const = lambda i: (0, 0)
    o, s = pl.pallas_call(
        functools.partial(_gdn_step_kernel, n_heads=n_heads),
        grid=(b,),
        in_specs=[pl.BlockSpec((1, 1, dconv), lambda i: (i, 0, 0)),
                  pl.BlockSpec((1, 1, dvv), lambda i: (i, 0, dconv // dvv)),
                  pl.BlockSpec((1, 1, LANES), lambda i: (i, 0, 0)),
                  pl.BlockSpec((1, DN_CONV_W - 1, dconv), lambda i: (i, 0, 0)),
                  pl.BlockSpec((1, n_heads, DK_C, DV_C), lambda i: (i, 0, 0, 0)),
                  pl.BlockSpec((DN_CONV_W, dconv), const),
                  pl.BlockSpec((1, LANES), const),
                  pl.BlockSpec((1, LANES), const),
                  pl.BlockSpec((1, DV_C), const)],
        out_specs=[pl.BlockSpec((1, 1, dvv), lambda i: (i, 0, 0)),
                   pl.BlockSpec((1, n_heads, DK_C, DV_C), lambda i: (i, 0, 0, 0))],
        out_shape=[jax.ShapeDtypeStruct((b, 1, dvv), F32),
                   jax.ShapeDtypeStruct((b, n_heads, DK_C, DV_C), F32)],
        compiler_params=_cparams("arbitrary"),
        name="gdn_step",
    )(qkvz3, qkvz3, gates.reshape(b, 1, LANES), conv0, s0, cw, alog_row, dtb_row,
      onorm.reshape(1, DV_C))
    return o.reshape(b, dvv), s


def _row_tile(m, want):
    return want if m % want == 0 else m


def _trunk(x, kv, wkv0, shift0, dn0, dnconv0, ffnconv0, p):
    b, t, d = x.shape
    m = b * t
    depth = p['norm_ffn'].shape[0]
    d_a = p['sb_bias'].shape[1] * DH_A
    nb = p['mu_b'].shape[1]
    dconv = p['conv_dn'].shape[2]
    n_heads_c = p['a_log_dn'].shape[1]
    dvv = n_heads_c * DV_C
    tm = _row_tile(m, 256)
    new_k, new_v, new_wkv, new_shift, new_dn, new_dnconv, new_ffn = [], [], [], [], [], [], []
    x2 = x.reshape(m, d)
    for l in range(depth):
        i = l // 2
        if l % 2 == 0:
            qkv, pb = norm_matmul(x2, p['norm_mix_even'][i], p['w_in_even'][i], (3 * d_a, nb), tm)
            rp = dict(mu=p['mu_b'][i], w0=p['w0_b'][i], w_up=p['w_up_b'][i], a0=p['a0_b'][i],
                      a_up=p['a_up_b'][i], g_up=p['g_up_b'][i], k_k=p['kk_b'][i], k_a=p['ka_b'][i],
                      r_k=p['rk_b'][i], ln_w=p['lnx_w_b'][i], ln_b=p['lnx_b_b'][i])
            if kv is None:
                o_a, kn = sb_prompt(qkv, p['qnorm_a'][i], p['knorm_a'][i], p['sb_bias'][i], b, t)
                o_b, wkv = rwkv_chunked(pb, shift0[i], wkv0[i], rp, b, t)
                shift = pb.reshape(b, t, nb)[:, t - 1]
            else:
                o_a, kn = sb_decode(qkv, kv[0], kv[1], kv[2], i, p['qnorm_a'][i], p['knorm_a'][i],
                                    p['sb_bias'][i])
                o_b, wkv = rwkv_step(pb, shift0[i], wkv0[i], rp)
                shift = pb
            w_out = p['w_out_even'][i]
            x2 = matmul_res([o_a, o_b], [w_out[:d_a], w_out[d_a:]], x2, tm)
            new_k.append(kn.reshape(b, t, d_a // DH_A, DH_A))
            new_v.append(qkv[:, 2 * d_a:].reshape(b, t, d_a // DH_A, DH_A))
            new_wkv.append(wkv)
            new_shift.append(shift)
        else:
            qkvz, gates = norm_matmul(x2, p['norm_mix_odd'][i], p['w_in_odd'][i], (dconv + dvv, LANES), tm)
            if kv is None:
                o_c, s_fin = gdn_chunked(qkvz, gates, dnconv0[i], dn0[i], p['conv_dn'][i], p['a_log_dn'][i],
                                         p['dt_bias_dn'][i], p['onorm_dn'][i], b, t)
                cbuf = qkvz.reshape(b, t, dconv + dvv)[:, t - (DN_CONV_W - 1):, :dconv]
            else:
                o_c, s_fin = gdn_step(qkvz, gates, dnconv0[i], dn0[i], p['conv_dn'][i], p['a_log_dn'][i],
                                      p['dt_bias_dn'][i], p['onorm_dn'][i])
                cbuf = jnp.concatenate([dnconv0[i][:, 1:], qkvz[:, None, :dconv]], axis=1)
            x2 = matmul_res([o_c], [p['w_out_odd'][i]], x2, tm)
            new_dn.append(s_fin)
            new_dnconv.append(cbuf)
        ffn_args = (p['norm_ffn'][l], p['w_gate'][l], p['w_up'][l], p['conv_ffn'][l], p['conv_ffn_b'][l],
                    p['w_down'][l])
        if kv is None:
            x3, fbuf = conv_ffn(x2.reshape(b, t, d), ffnconv0[l], *ffn_args, tm=_row_tile(t, 256))
            x2 = x3.reshape(m, d)
        else:
            x2, fbuf = conv_ffn_step(x2, ffnconv0[l], *ffn_args)
        new_ffn.append(fbuf)
    states = tuple(jnp.stack(s) for s in (new_k, new_v, new_wkv, new_shift, new_dn, new_dnconv, new_ffn))
    return x2.reshape(b, t, d), states


def kernel(x_prompt, x_sample, cache_k, cache_v, page_table, state_wkv, state_shift, state_dn,
           state_dn_conv, state_ffn_conv, norm_mix_even, w_in_even, qnorm_a, knorm_a, sb_bias,
           mu_b, w0_b, w_up_b, a0_b, a_up_b, g_up_b, kk_b, ka_b, rk_b, lnx_w_b, lnx_b_b,
           w_out_even, norm_mix_odd, w_in_odd, conv_dn, a_log_dn, dt_bias_dn, onorm_dn, w_out_odd,
           norm_ffn, w_gate, w_up, conv_ffn, conv_ffn_b, w_down):
    n_heads_c = a_log_dn.shape[1]
    n_gate_cols = 2 * n_heads_c
    assert x_sample.shape[1] == 1 and n_gate_cols <= LANES
    w_in_odd_p = jnp.pad(w_in_odd, ((0, 0), (0, 0), (0, LANES - n_gate_cols)))
    bf = lambda w: w.astype(BF16)
    p = dict(norm_mix_even=norm_mix_even, w_in_even=bf(w_in_even), qnorm_a=qnorm_a, knorm_a=knorm_a,
             sb_bias=sb_bias, mu_b=mu_b, w0_b=w0_b, w_up_b=bf(w_up_b), a0_b=a0_b, a_up_b=bf(a_up_b),
             g_up_b=bf(g_up_b), kk_b=kk_b, ka_b=ka_b, rk_b=rk_b, lnx_w_b=lnx_w_b, lnx_b_b=lnx_b_b,
             w_out_even=bf(w_out_even), norm_mix_odd=norm_mix_odd, w_in_odd=bf(w_in_odd_p),
             conv_dn=conv_dn, a_log_dn=a_log_dn, dt_bias_dn=dt_bias_dn, onorm_dn=onorm_dn,
             w_out_odd=bf(w_out_odd), norm_ffn=norm_ffn, w_gate=bf(w_gate), w_up=bf(w_up),
             conv_ffn=conv_ffn, conv_ffn_b=conv_ffn_b, w_down=bf(w_down))
    b = x_prompt.shape[0]
    n_ab = state_wkv.shape[0]
    n_c = state_dn.shape[0]
    depth = norm_ffn.shape[0]
    zeros = lambda like, lead: jnp.zeros((lead, b) + like.shape[2:], F32)
    y_prompt, ps = _trunk(x_prompt, None, zeros(state_wkv, n_ab), zeros(state_shift, n_ab),
                          zeros(state_dn, n_c), zeros(state_dn_conv, n_c), zeros(state_ffn_conv, depth), p)
    y_sample, ss = _trunk(x_sample, (cache_k, cache_v, page_table), state_wkv, state_shift, state_dn,
                          state_dn_conv, state_ffn_conv, p)
    return (y_prompt, y_sample, ps[0], ps[1], ss[0], ss[1], ps[2], ss[2], ps[3], ss[3],
            ps[4], ss[4], ps[5], ss[5], ps[6], ss[6])
```

```python
import functools

import jax
import jax.numpy as jnp
from jax import lax
from jax.experimental import pallas as pl
from jax.experimental.pallas import tpu as pltpu

F32 = jnp.float32
BF16 = jnp.bfloat16

RMS_EPS = 1e-6
L2_EPS = 1e-6
GN_EPS = 64e-5

DH_A = 64
N_B = 64
R_W, R_A, R_G = 64, 64, 128
DK_C = 128
DV_C = 128
DN_CONV_W = 4
FFN_CONV_W = 3

SB_BLOCK = 128
CHUNK = 64
SEQS_PER_STEP = 2
SUBLANES = 8
LANES = 128
VMEM_LIMIT_BYTES = 56 * 1024 * 1024
PROJ_ROWS = 512
OUT_ROWS = 1024
FFN_ROWS = 512


def _cparams(*sem):
    return pltpu.CompilerParams(dimension_semantics=sem, vmem_limit_bytes=VMEM_LIMIT_BYTES)


def _dot(a, b):
    return jnp.dot(a.astype(BF16), b.astype(BF16), preferred_element_type=F32)


def _dot_nt(a, b):
    return lax.dot_general(a.astype(BF16), b.astype(BF16), (((1,), (1,)), ((), ())),
                           preferred_element_type=F32)


def _dot_tn(a, b):
    return lax.dot_general(a.astype(BF16), b.astype(BF16), (((0,), (0,)), ((), ())),
                           preferred_element_type=F32)


def _split2(a):
    hi = a.astype(BF16)
    lo = (a - hi.astype(F32)).astype(BF16)
    return hi, lo


def _split3(a):
    hi = a.astype(BF16)
    r1 = a - hi.astype(F32)
    mid = r1.astype(BF16)
    lo = (r1 - mid.astype(F32)).astype(BF16)
    return hi, mid, lo


def _dot_mask_lhs(mask_bf16, x):
    d = functools.partial(jnp.dot, preferred_element_type=F32)
    h, m, l = _split3(x)
    return d(mask_bf16, h) + (d(mask_bf16, m) + d(mask_bf16, l))


def _softplus(x):
    return jnp.maximum(x, 0.0) + jnp.log(1.0 + jnp.exp(-jnp.abs(x)))


def _sigmoid(x):
    return 1.0 / (1.0 + jnp.exp(-x))


def _silu(x):
    return x * _sigmoid(x)


def _rms(x, g):
    return x * lax.rsqrt(jnp.mean(x * x, axis=-1, keepdims=True) + RMS_EPS) * g


def _l2(x):
    return x * lax.rsqrt(jnp.sum(x * x, axis=-1, keepdims=True) + L2_EPS)


def _iota2(shape, dim):
    return lax.broadcasted_iota(jnp.int32, shape, dim)


def _shift_rows(cur, carry, d):
    rolled = pltpu.roll(cur, d, 0)
    crolled = pltpu.roll(carry, d, 0)
    first = jnp.where(_iota2(carry.shape, 0) < d, crolled, rolled[:SUBLANES])
    if cur.shape[0] == SUBLANES:
        return first
    return jnp.concatenate([first, rolled[SUBLANES:]], axis=0)


def _tri_inv_multi(a_list):
    n = a_list[0].shape[0]
    row = _iota2((n, n), 0)
    col = _iota2((n, n), 1)
    eye = (row == col).astype(F32)
    d = functools.partial(jnp.dot, preferred_element_type=F32)
    ts = [eye - jnp.where((row >> 1) == (col >> 1), a, 0.0) for a in a_list]
    s = 2
    while s < n:
        sh = s.bit_length() - 1
        sub = ((row >> (sh + 1)) == (col >> (sh + 1))) & ((row >> sh) != (col >> sh))
        tb = [t.astype(BF16) for t in ts]
        tes = [d(t, jnp.where(sub, a, 0.0).astype(BF16)).astype(BF16) for t, a in zip(tb, a_list)]
        ts = [t - d(te, t16) for t, te, t16 in zip(ts, tes, tb)]
        s *= 2
    return ts


def _norm_matmul_kernel(x_ref, g_ref, w_ref, *o_refs, splits):
    h = _rms(x_ref[...], g_ref[...]).astype(BF16)
    y = jnp.dot(h, w_ref[...], preferred_element_type=F32)
    off = 0
    for o_ref, n in zip(o_refs, splits):
        o_ref[...] = y[:, off:off + n]
        off += n


def norm_matmul(x, g, w, splits, tm):
    m, k = x.shape
    n = w.shape[1]
    assert sum(splits) == n and m % tm == 0
    return pl.pallas_call(
        functools.partial(_norm_matmul_kernel, splits=tuple(splits)),
        grid=(m // tm,),
        in_specs=[pl.BlockSpec((tm, k), lambda i: (i, 0)),
                  pl.BlockSpec((1, k), lambda i: (0, 0)),
                  pl.BlockSpec((k, n), lambda i: (0, 0))],
        out_specs=[pl.BlockSpec((tm, s), lambda i: (i, 0)) for s in splits],
        out_shape=[jax.ShapeDtypeStruct((m, s), F32) for s in splits],
        compiler_params=_cparams("arbitrary"),
        name="norm_matmul",
    )(x, g.reshape(1, k), w)


def _matmul_res_kernel(*refs, n_in):
    a_refs = refs[:n_in]
    w_refs = refs[n_in:2 * n_in]
    res_ref = refs[2 * n_in]
    o_ref = refs[2 * n_in + 1]
    acc = res_ref[...]
    for a_ref, w_ref in zip(a_refs, w_refs):
        acc = acc + jnp.dot(a_ref[...].astype(BF16), w_ref[...], preferred_element_type=F32)
    o_ref[...] = acc


def matmul_res(a_list, w_list, res, tm):
    m, n = res.shape
    assert m % tm == 0
    n_in = len(a_list)
    in_specs = ([pl.BlockSpec((tm, a.shape[1]), lambda i: (i, 0)) for a in a_list]
                + [pl.BlockSpec(w.shape, lambda i: (0, 0)) for w in w_list]
                + [pl.BlockSpec((tm, n), lambda i: (i, 0))])
    return pl.pallas_call(
        functools.partial(_matmul_res_kernel, n_in=n_in),
        grid=(m // tm,),
        in_specs=in_specs,
        out_specs=pl.BlockSpec((tm, n), lambda i: (i, 0)),
        out_shape=jax.ShapeDtypeStruct((m, n), F32),
        compiler_params=_cparams("arbitrary"),
        name="matmul_res",
    )(*a_list, *w_list, res)


def _ffn_kernel(x_ref, buf_ref, g_ref, wg_ref, wu_ref, cw_ref, cb_ref, wd_ref,
                y_ref, tail_ref, carry_scr):
    ti = pl.program_id(1)
    x = x_ref[0]
    tm = x.shape[0]
    h = _rms(x, g_ref[...]).astype(BF16)
    gate = jnp.dot(h, wg_ref[...], preferred_element_type=F32)
    up = jnp.dot(h, wu_ref[...], preferred_element_type=F32)

    @pl.when(ti == 0)
    def _():
        carry_scr[...] = jnp.zeros_like(carry_scr)
        carry_scr[SUBLANES - 2:SUBLANES, :] = buf_ref[0]

    carry = carry_scr[...]
    conv = (cw_ref[2:3, :] * gate + cw_ref[1:2, :] * _shift_rows(gate, carry, 1)
            + cw_ref[0:1, :] * _shift_rows(gate, carry, 2))
    carry_scr[...] = gate[tm - SUBLANES:tm]
    tail_ref[0] = gate[tm - 2:tm]
    act = (_silu(conv + cb_ref[...]) * up).astype(BF16)
    y_ref[0] = x + jnp.dot(act, wd_ref[...], preferred_element_type=F32)


def conv_ffn(x, buf, g, wg, wu, cw, cb, wd, tm):
    b, t, d = x.shape
    f = wg.shape[1]
    assert t % tm == 0 and tm % SUBLANES == 0
    const = lambda bi, ti: (0, 0)
    return pl.pallas_call(
        _ffn_kernel,
        grid=(b, t // tm),
        in_specs=[pl.BlockSpec((1, tm, d), lambda bi, ti: (bi, ti, 0)),
                  pl.BlockSpec((1, 2, f), lambda bi, ti: (bi, 0, 0)),
                  pl.BlockSpec((1, d), const),
                  pl.BlockSpec((d, f), const),
                  pl.BlockSpec((d, f), const),
                  pl.BlockSpec((FFN_CONV_W, f), const),
                  pl.BlockSpec((1, f), const),
                  pl.BlockSpec((f, d), const)],
        out_specs=[pl.BlockSpec((1, tm, d), lambda bi, ti: (bi, ti, 0)),
                   pl.BlockSpec((1, 2, f), lambda bi, ti: (bi, 0, 0))],
        out_shape=[jax.ShapeDtypeStruct((b, t, d), F32),
                   jax.ShapeDtypeStruct((b, 2, f), F32)],
        scratch_shapes=[pltpu.VMEM((SUBLANES, f), F32)],
        compiler_params=_cparams("arbitrary", "arbitrary"),
        name="conv_ffn",
    )(x, buf, g.reshape(1, d), wg, wu, cw, cb.reshape(1, f), wd)


def _ffn_step_kernel(x_ref, buf0_ref, buf1_ref, g_ref, wg_ref, wu_ref, cw_ref, cb_ref, wd_ref,
                     y_ref, gate_ref):
    x = x_ref[...]
    h = _rms(x, g_ref[...]).astype(BF16)
    gate = jnp.dot(h, wg_ref[...], preferred_element_type=F32)
    up = jnp.dot(h, wu_ref[...], preferred_element_type=F32)
    conv = cw_ref[2:3, :] * gate + cw_ref[1:2, :] * buf1_ref[...] + cw_ref[0:1, :] * buf0_ref[...]
    act = (_silu(conv + cb_ref[...]) * up).astype(BF16)
    y_ref[...] = x + jnp.dot(act, wd_ref[...], preferred_element_type=F32)
    gate_ref[...] = gate


def conv_ffn_step(x, buf, g, wg, wu, cw, cb, wd):
    b, d = x.shape
    f = wg.shape[1]
    full = lambda shape: pl.BlockSpec(shape, lambda i: (0,) * len(shape))
    y, gate = pl.pallas_call(
        _ffn_step_kernel,
        grid=(1,),
        in_specs=[full((b, d)), full((b, f)), full((b, f)), full((1, d)), full((d, f)), full((d, f)),
                  full((FFN_CONV_W, f)), full((1, f)), full((f, d))],
        out_specs=[full((b, d)), full((b, f))],
        out_shape=[jax.ShapeDtypeStruct((b, d), F32), jax.ShapeDtypeStruct((b, f), F32)],
        compiler_params=_cparams("arbitrary"),
        name="conv_ffn_step",
    )(x, buf[:, 0], buf[:, 1], g.reshape(1, d), wg, wu, cw, cb.reshape(1, f), wd)
    return y, jnp.stack([buf[:, 1], gate], axis=1)


def _pair_rms(x, w_row):
    left = _iota2(x.shape, 1) < DH_A
    x2 = x * x
    ms_l = jnp.sum(jnp.where(left, x2, 0.0), axis=-1, keepdims=True) * (1.0 / DH_A)
    ms_r = jnp.sum(jnp.where(left, 0.0, x2), axis=-1, keepdims=True) * (1.0 / DH_A)
    inv = jnp.where(left, lax.rsqrt(ms_l + RMS_EPS), lax.rsqrt(ms_r + RMS_EPS))
    return x * inv * w_row


def _sb_prompt_kernel(bias_ref, q_ref, k_ref, v_ref, qw_ref, kw_ref, o_ref, kn_ref,
                      kn_scr, v_scr, q_scr, later_scr, o_scr, tail_scr, *, n_heads):
    qi = pl.program_id(1)
    blk = q_ref.shape[0]
    n_pairs = n_heads // 2
    pw = 2 * DH_A
    scale = DH_A ** -0.5
    row0 = pl.multiple_of(qi * blk, blk)

    k = k_ref[...]
    kn = jnp.concatenate([_pair_rms(k[:, p * pw:(p + 1) * pw], kw_ref[...]) for p in range(n_pairs)],
                         axis=-1)
    kn_ref[...] = kn
    kn_scr[pl.ds(row0, blk), :] = kn.astype(BF16)
    v_scr[pl.ds(row0, blk), :] = v_ref[...].astype(BF16)

    q = q_ref[...]
    left = _iota2((blk, pw), 1) < DH_A
    for p in range(n_pairs):
        qn = _pair_rms(q[:, p * pw:(p + 1) * pw], qw_ref[...]) * scale
        q_scr[p] = jnp.concatenate([jnp.where(left, qn, 0.0), jnp.where(left, 0.0, qn)],
                                   axis=0).astype(BF16)
    krow = _iota2((2 * blk, 2 * blk), 0)
    krow = jnp.where(krow >= blk, krow - blk, krow)
    kcol = _iota2((2 * blk, 2 * blk), 1)
    later_scr[...] = jnp.where((kcol >= blk) | (krow > kcol), 1.0, 0.0).astype(BF16)
    o_scr[...] = jnp.zeros_like(o_scr)
    tail_scr[...] = jnp.zeros_like(tail_scr)

    qrow = _iota2((2 * blk, blk), 0)
    causal = _iota2((2 * blk, blk), 1) < jnp.where(qrow >= blk, qrow - blk, qrow)
    vleft = _iota2((blk, pw), 1) < DH_A

    def block(off, diag):
        pairs = range(n_pairs)
        zs = []
        for p in pairs:
            z2 = lax.dot_general(q_scr[p], kn_scr[pl.ds(off, blk), p * pw:(p + 1) * pw],
                                 (((1,), (1,)), ((), ())), preferred_element_type=F32)
            zs.append(jnp.concatenate([z2[:blk] + bias_ref[2 * p], z2[blk:] + bias_ref[2 * p + 1]],
                                      axis=0))
        sps = [_softplus(z) for z in zs]
        if diag:
            sps = [jnp.where(causal, sp, 0.0) for sp in sps]
        his = [sp.astype(BF16) for sp in sps]
        los = [(sp - hi.astype(F32)).astype(BF16) for sp, hi in zip(sps, his)]
        lo_mat = later_scr[...]
        cums = [jnp.dot(jnp.concatenate([hi, lo], axis=1), lo_mat, preferred_element_type=F32)
                for hi, lo in zip(his, los)]
        for p in pairs:
            tail = tail_scr[p]
            w = jnp.exp((zs[p] - sps[p]) - (cums[p][:, :blk] + tail))
            if diag:
                w = jnp.where(causal, w, 0.0)
            tail_scr[p] = tail + cums[p][:, blk:]
            w = w.astype(BF16)
            vj = v_scr[pl.ds(off, blk), p * pw:(p + 1) * pw]
            zero = jnp.zeros_like(vj)
            v_bd = jnp.concatenate([jnp.where(vleft, vj, zero), jnp.where(vleft, zero, vj)], axis=0)
            o_scr[:, p * pw:(p + 1) * pw] += jnp.dot(
                jnp.concatenate([w[:blk], w[blk:]], axis=1), v_bd, preferred_element_type=F32)

    block(row0, True)

    def body(i, carry):
        block(pl.multiple_of((qi - i) * blk, blk), False)
        return carry

    lax.fori_loop(1, qi + 1, body, 0)
    o_ref[...] = o_scr[...]


def sb_prompt(qkv, qw, kw, bias, batch, seq):
    m, n3 = qkv.shape
    d_a = n3 // 3
    n_heads = d_a // DH_A
    blk = SB_BLOCK
    assert seq % blk == 0 and n_heads % 2 == 0 and 2 * DH_A == LANES
    nq = seq // blk
    rowblk = lambda col: (lambda b, qi: (b * nq + qi, col))
    pair_w = lambda w: jnp.tile(w, 2).reshape(1, 2 * DH_A)
    return pl.pallas_call(
        functools.partial(_sb_prompt_kernel, n_heads=n_heads),
        grid=(batch, nq),
        in_specs=[pl.BlockSpec(memory_space=pltpu.SMEM),
                  pl.BlockSpec((blk, d_a), rowblk(0)),
                  pl.BlockSpec((blk, d_a), rowblk(1)),
                  pl.BlockSpec((blk, d_a), rowblk(2)),
                  pl.BlockSpec((1, 2 * DH_A), lambda b, qi: (0, 0)),
                  pl.BlockSpec((1, 2 * DH_A), lambda b, qi: (0, 0))],
        out_specs=[pl.BlockSpec((blk, d_a), rowblk(0)),
                   pl.BlockSpec((blk, d_a), rowblk(0))],
        out_shape=[jax.ShapeDtypeStruct((m, d_a), F32), jax.ShapeDtypeStruct((m, d_a), F32)],
        scratch_shapes=[pltpu.VMEM((seq, d_a), BF16), pltpu.VMEM((seq, d_a), BF16),
                        pltpu.VMEM((n_heads // 2, 2 * blk, 2 * DH_A), BF16),
                        pltpu.VMEM((2 * blk, 2 * blk), BF16),
                        pltpu.VMEM((blk, d_a), F32),
                        pltpu.VMEM((n_heads // 2, 2 * blk, blk), F32)],
        compiler_params=_cparams("arbitrary", "arbitrary"),
        name="sb_prompt",
    )(bias, qkv, qkv, qkv, pair_w(qw), pair_w(kw))


DECODE_BANK = 4
DECODE_SLOTS = 16


def _sb_decode_kernel(pt_ref, q_ref, k_ref, ck_hbm, cv_hbm, qw_ref, kw_ref, bias_ref,
                      o_ref, kn_ref, kbuf, vbuf, sem, *, n_heads, n_pages, layer):
    b = pl.program_id(0)
    page = kbuf.shape[3]
    scale = DH_A ** -0.5
    nt = lambda x, y: lax.dot_general(x, y, (((1,), (1,)), ((), ())), preferred_element_type=F32)

    def copies(i, slot):
        phys = pt_ref[b, n_pages - 1 - i]
        return (pltpu.make_async_copy(ck_hbm.at[layer, phys], kbuf.at[slot], sem.at[0, slot]),
                pltpu.make_async_copy(cv_hbm.at[layer, phys], vbuf.at[slot], sem.at[1, slot]))

    for j in range(DECODE_SLOTS):
        for cp in copies(j, j):
            cp.start()

    q = q_ref[0]
    k = k_ref[0]
    hrow = _iota2((n_heads, DH_A), 0)
    qe, kn = [], []
    for h in range(n_heads):
        lanes = slice(h * DH_A, (h + 1) * DH_A)
        qn = _rms(q[:, lanes], qw_ref[...]) * scale
        qe.append(jnp.where(hrow == h, jnp.broadcast_to(qn, (n_heads, DH_A)), 0.0).astype(BF16))
        kn.append(_rms(k[:, lanes], kw_ref[...]))
    kn_ref[0] = jnp.concatenate(kn, axis=-1)

    krow = _iota2((page, 2 * page), 0)
    kcol = _iota2((page, 2 * page), 1)
    later_ones = jnp.where((kcol >= page) | (krow > kcol), 1.0, 0.0).astype(BF16)
    bias = bias_ref[...]

    def bank_pages(first, bank, tail, acc):
        pages = range(DECODE_BANK)
        slots = [bank * DECODE_BANK + j for j in pages]
        for j in pages:
            for cp in copies(first + j, slots[j]):
                cp.wait()
        zs = []
        for j in pages:
            z = bias
            for h in range(n_heads):
                z = z + jnp.dot(qe[h], kbuf[slots[j], h].astype(BF16), preferred_element_type=F32)
            zs.append(z)
        sps = [_softplus(z) for z in zs]
        his = [sp.astype(BF16) for sp in sps]
        los = [(sp - hi.astype(F32)).astype(BF16) for sp, hi in zip(sps, his)]
        cums = [jnp.dot(hi, later_ones, preferred_element_type=F32)
                + jnp.dot(lo, later_ones, preferred_element_type=F32) for hi, lo in zip(his, los)]
        ws = []
        for j in pages:
            ws.append(jnp.exp((zs[j] - sps[j]) - (cums[j][:, :page] + tail)).astype(BF16))
            tail = tail + cums[j][:, page:]
        for j in pages:
            acc = [acc[h] + nt(ws[j], vbuf[slots[j], h].astype(BF16)) for h in range(n_heads)]

        @pl.when(first + DECODE_SLOTS < n_pages)
        def _():
            for j in pages:
                for cp in copies(first + DECODE_SLOTS + j, slots[j]):
                    cp.start()

        return tail, acc

    def group(gi, carry):
        tail, acc = carry
        for bank in range(DECODE_SLOTS // DECODE_BANK):
            tail, acc = bank_pages(gi * DECODE_SLOTS + bank * DECODE_BANK, bank, tail, acc)
        return tail, acc

    zero = jnp.zeros((n_heads, DH_A), F32)
    _, acc = lax.fori_loop(0, n_pages // DECODE_SLOTS, group,
                           (jnp.zeros((n_heads, page), F32), [zero] * n_heads))
    o_ref[0] = jnp.concatenate([acc[h][h:h + 1, :] for h in range(n_heads)], axis=-1)


def sb_decode(qkv, cache_k, cache_v, page_table, layer, qw, kw, bias):
    b, n3 = qkv.shape
    d_a = n3 // 3
    n_heads = d_a // DH_A
    n_layers, n_phys, page = cache_k.shape[:3]
    n_pages = page_table.shape[1]
    assert n_pages % DECODE_SLOTS == 0 and cache_k.shape[3] == n_heads
    ck_t = jnp.transpose(cache_k, (0, 1, 3, 4, 2))
    cv_t = jnp.transpose(cache_v, (0, 1, 3, 4, 2))
    qkv3 = qkv.reshape(b, 1, n3)
    grid_spec = pltpu.PrefetchScalarGridSpec(
        num_scalar_prefetch=1,
        grid=(b,),
        in_specs=[pl.BlockSpec((1, 1, d_a), lambda bi, pt: (bi, 0, 0)),
                  pl.BlockSpec((1, 1, d_a), lambda bi, pt: (bi, 0, 1)),
                  pl.BlockSpec(memory_space=pl.ANY),
                  pl.BlockSpec(memory_space=pl.ANY),
                  pl.BlockSpec((1, DH_A), lambda bi, pt: (0, 0)),
                  pl.BlockSpec((1, DH_A), lambda bi, pt: (0, 0)),
                  pl.BlockSpec((n_heads, 1), lambda bi, pt: (0, 0))],
        out_specs=[pl.BlockSpec((1, 1, d_a), lambda bi, pt: (bi, 0, 0)),
                   pl.BlockSpec((1, 1, d_a), lambda bi, pt: (bi, 0, 0))],
        scratch_shapes=[pltpu.VMEM((DECODE_SLOTS, n_heads, DH_A, page), F32),
                        pltpu.VMEM((DECODE_SLOTS, n_heads, DH_A, page), F32),
                        pltpu.SemaphoreType.DMA((2, DECODE_SLOTS))],
    )
    o, kn = pl.pallas_call(
        functools.partial(_sb_decode_kernel, n_heads=n_heads, n_pages=n_pages, layer=layer),
        grid_spec=grid_spec,
        out_shape=[jax.ShapeDtypeStruct((b, 1, d_a), F32), jax.ShapeDtypeStruct((b, 1, d_a), F32)],
        compiler_params=_cparams("arbitrary"),
        name="sb_decode",
    )(page_table, qkv3, qkv3, ck_t, cv_t, qw.reshape(1, DH_A), kw.reshape(1, DH_A),
      bias.reshape(n_heads, 1))
    return o.reshape(b, d_a), kn.reshape(b, d_a)


def _rwkv_tokens(pb, prev, mu, w0, w_up, a0, a_up, g_up, k_k, k_a, d_b):
    xs = pb + (prev - pb) * mu
    r = xs[:, 0:d_b]
    k = xs[:, d_b:2 * d_b]
    v = xs[:, 2 * d_b:3 * d_b]
    xw = xs[:, 3 * d_b:3 * d_b + R_W]
    xa = xs[:, 3 * d_b + R_W:3 * d_b + R_W + R_A]
    xg = xs[:, 3 * d_b + R_W + R_A:3 * d_b + R_W + R_A + R_G]
    w_log = -_softplus(-(w0 + _dot(jnp.tanh(xw), w_up))) - 0.5
    log_decay = -jnp.exp(w_log)
    a = _sigmoid(a0 + _dot(xa, a_up))
    g = _dot(_sigmoid(xg), g_up)
    kk_raw = k * k_k
    k_mod = k * (1.0 + (a - 1.0) * k_a)
    return r, k_mod, v, kk_raw, a, g, log_decay


def _rwkv_finish(o, r, k_mod, v, g, r_k, ln_w, ln_b):
    mean = jnp.mean(o, axis=-1, keepdims=True)
    var = jnp.mean(jnp.square(o - mean), axis=-1, keepdims=True)
    on = (o - mean) * lax.rsqrt(var + GN_EPS) * ln_w + ln_b
    bonus = jnp.sum(r * k_mod * r_k, axis=-1, keepdims=True) * v
    return (on + bonus) * g


def _rwkv_chunk_kernel(pb_ref, shift0_ref, s0_ref, mu_ref, w0_ref, wup_ref, a0_ref, aup_ref, gup_ref,
                       kk_ref, ka_ref, rk_ref, lnw_ref, lnb_ref, hsum_ref, o_ref, sfin_ref,
                       carry_scr, s_scr, *, n_heads):
    c = pl.program_id(1)
    d_b = n_heads * N_B
    n_seq, t, nb = pb_ref.shape

    @pl.when(c == 0)
    def _():
        carry_scr[...] = jnp.zeros_like(carry_scr)
        carry_scr[:, SUBLANES - 1:SUBLANES, :] = shift0_ref[...]
        s_scr[...] = s0_ref[...]

    pbs = [pb_ref[i] for i in range(n_seq)]
    prev = jnp.concatenate([_shift_rows(pbs[i], carry_scr[i], 1) for i in range(n_seq)], axis=0)
    for i in range(n_seq):
        carry_scr[i] = pbs[i][t - SUBLANES:t]
    r, k_mod, v, kk_raw, a, g, log_decay = _rwkv_tokens(
        jnp.concatenate(pbs, axis=0), prev, mu_ref[...], w0_ref[...], wup_ref[...], a0_ref[...],
        aup_ref[...], gup_ref[...], kk_ref[...], ka_ref[...], d_b)

    row = _iota2((t, t), 0)
    col = _iota2((t, t), 1)
    incl = row >= col
    strict = row > col
    rows = n_seq * t
    srow = _iota2((rows, rows), 0)
    scol = _iota2((rows, rows), 1)
    lg_t = t.bit_length() - 1
    same_seq = (srow >> lg_t) == (scol >> lg_t)
    cum = _dot_mask_lhs((same_seq & (srow >= scol)).astype(BF16), log_decay)
    cum_prev = cum - log_decay
    cum_last = jnp.concatenate(
        [jnp.broadcast_to(cum[(i + 1) * t - 1:(i + 1) * t, :], (t, d_b)) for i in range(n_seq)], axis=0)
    e_cum = jnp.exp(cum)
    e_prev = jnp.exp(cum_prev)
    e_neg = jnp.exp(-cum)
    e_rest = jnp.exp(cum_last - cum)
    e_last = jnp.exp(cum_last)

    items = [(i, h) for i in range(n_seq) for h in range(n_heads)]
    n = range(len(items))
    sub = lambda z, i, h: z[i * t:(i + 1) * t, h * N_B:(h + 1) * N_B]
    pick = lambda z: [sub(z, i, h) for i, h in items]
    r_, kmod_, v_, a_, g_ = pick(r), pick(k_mod), pick(v), pick(a), pick(g)
    ecum_, eprev_, eneg_, erest_, elast_ = pick(e_cum), pick(e_prev), pick(e_neg), pick(e_rest), pick(e_last)
    sq_hi, sq_lo = _split2(kk_raw * kk_raw)
    ssq = (jnp.dot(sq_hi, hsum_ref[...], preferred_element_type=F32)
           + jnp.dot(sq_lo, hsum_ref[...], preferred_element_type=F32))
    kk = pick(kk_raw * lax.rsqrt(ssq + L2_EPS))
    ka = [kk[j] * a_[j] for j in n]
    x = [jnp.concatenate([kk[j] * eprev_[j], r_[j] * ecum_[j]], axis=0) for j in n]
    khat = [kmod_[j] * eneg_[j] for j in n]
    bhat = [ka[j] * eneg_[j] for j in n]
    s = [s_scr[i, h] for i, h in items]
    xb = [_dot_nt(x[j], bhat[j]) for j in n]
    xk = [_dot_nt(x[j], khat[j]) for j in n]
    xs = [_dot_nt(x[j], s[j]) for j in n]
    tinv = _tri_inv_multi([jnp.where(strict, xb[j][:t], 0.0) for j in n])
    rhs = [xs[j][:t] + _dot(jnp.where(strict, xk[j][:t], 0.0), v_[j]) for j in n]
    u = [_dot(tinv[j], rhs[j]) for j in n]
    o = [xs[j][t:] + _dot(jnp.where(incl, xk[j][t:], 0.0), v_[j])
         - _dot(jnp.where(incl, xb[j][t:], 0.0), u[j]) for j in n]
    for j, (i, h) in enumerate(items):
        s_scr[i, h] = s[j] * elast_[j][:1] + _dot_tn(
            jnp.concatenate([v_[j], -u[j]], axis=0),
            jnp.concatenate([kmod_[j] * erest_[j], ka[j] * erest_[j]], axis=0))
    for j, (i, h) in enumerate(items):
        lanes = slice(h * N_B, (h + 1) * N_B)
        o_ref[i, :, lanes] = _rwkv_finish(o[j], r_[j], kmod_[j], v_[j], g_[j],
                                          rk_ref[:, lanes], lnw_ref[:, lanes], lnb_ref[:, lanes])

    @pl.when(c == pl.num_programs(1) - 1)
    def _():
        sfin_ref[...] = s_scr[...]


def _rwkv_params(p, d_b):
    row = lambda x: x.reshape(1, -1)
    return (row(p['mu']), row(p['w0']), p['w_up'], row(p['a0']), p['a_up'], p['g_up'],
            row(p['k_k']), row(p['k_a']), row(p['r_k']), row(p['ln_w']), row(p['ln_b']))


def rwkv_chunked(pb, shift0, s0, p, batch, seq):
    m, nb = pb.shape
    n_heads = s0.shape[1]
    d_b = n_heads * N_B
    t = CHUNK
    ns = SEQS_PER_STEP if batch % SEQS_PER_STEP == 0 else 1
    assert seq % t == 0 and t & (t - 1) == 0
    nc = seq // t
    head_sum = jnp.kron(jnp.eye(n_heads, dtype=F32), jnp.ones((N_B, N_B), F32)).astype(BF16)
    params = _rwkv_params(p, d_b) + (head_sum,)
    const = lambda b, c: (0, 0)
    o, s_fin = pl.pallas_call(
        functools.partial(_rwkv_chunk_kernel, n_heads=n_heads),
        grid=(batch // ns, nc),
        in_specs=[pl.BlockSpec((ns, t, nb), lambda b, c: (b, c, 0)),
                  pl.BlockSpec((ns, 1, nb), lambda b, c: (b, 0, 0)),
                  pl.BlockSpec((ns, n_heads, N_B, N_B), lambda b, c: (b, 0, 0, 0))]
                 + [pl.BlockSpec(x.shape, const) for x in params],
        out_specs=[pl.BlockSpec((ns, t, d_b), lambda b, c: (b, c, 0)),
                   pl.BlockSpec((ns, n_heads, N_B, N_B), lambda b, c: (b, 0, 0, 0))],
        out_shape=[jax.ShapeDtypeStruct((batch, seq, d_b), F32),
                   jax.ShapeDtypeStruct((batch, n_heads, N_B, N_B), F32)],
        scratch_shapes=[pltpu.VMEM((ns, SUBLANES, nb), F32), pltpu.VMEM((ns, n_heads, N_B, N_B), F32)],
        compiler_params=_cparams("arbitrary", "arbitrary"),
        name="rwkv_chunked",
    )(pb.reshape(batch, seq, nb), shift0.reshape(batch, 1, nb), s0, *params)
    return o.reshape(m, d_b), s_fin


def _col_from_row(x_row, eye):
    return jnp.sum(eye * x_row, axis=-1, keepdims=True)


def _row_from_col(x_col, eye):
    return jnp.sum(eye * x_col, axis=0, keepdims=True)


def _rwkv_step_kernel(pb_ref, shift0_ref, s0_ref, mu_ref, w0_ref, wup_ref, a0_ref, aup_ref, gup_ref,
                      kk_ref, ka_ref, rk_ref, lnw_ref, lnb_ref, o_ref, s_ref, *, n_heads):
    d_b = n_heads * N_B
    r, k_mod, v, kk_raw, a, g, log_decay = _rwkv_tokens(
        pb_ref[0], shift0_ref[0], mu_ref[...], w0_ref[...], wup_ref[...], a0_ref[...], aup_ref[...],
        gup_ref[...], kk_ref[...], ka_ref[...], d_b)
    decay = jnp.exp(log_decay)
    eye = (_iota2((N_B, N_B), 0) == _iota2((N_B, N_B), 1)).astype(F32)
    for h in range(n_heads):
        lanes = slice(h * N_B, (h + 1) * N_B)
        kk = _l2(kk_raw[:, lanes])
        s = s0_ref[0, h]
        sa = jnp.sum(s * kk, axis=-1, keepdims=True)
        v_col = _col_from_row(v[:, lanes], eye)
        s = s * decay[:, lanes] - sa * (kk * a[:, lanes]) + v_col * k_mod[:, lanes]
        s_ref[0, h] = s
        o = _row_from_col(jnp.sum(s * r[:, lanes], axis=-1, keepdims=True), eye)
        o_ref[0, :, lanes] = _rwkv_finish(o, r[:, lanes], k_mod[:, lanes], v[:, lanes], g[:, lanes],
                                          rk_ref[:, lanes], lnw_ref[:, lanes], lnb_ref[:, lanes])


def rwkv_step(pb, shift0, s0, p):
    b, nb = pb.shape
    n_heads = s0.shape[1]
    d_b = n_heads * N_B
    params = _rwkv_params(p, d_b)
    o, s = pl.pallas_call(
        functools.partial(_rwkv_step_kernel, n_heads=n_heads),
        grid=(b,),
        in_specs=[pl.BlockSpec((1, 1, nb), lambda i: (i, 0, 0)),
                  pl.BlockSpec((1, 1, nb), lambda i: (i, 0, 0)),
                  pl.BlockSpec((1, n_heads, N_B, N_B), lambda i: (i, 0, 0, 0))]
                 + [pl.BlockSpec(x.shape, lambda i: (0, 0)) for x in params],
        out_specs=[pl.BlockSpec((1, 1, d_b), lambda i: (i, 0, 0)),
                   pl.BlockSpec((1, n_heads, N_B, N_B), lambda i: (i, 0, 0, 0))],
        out_shape=[jax.ShapeDtypeStruct((b, 1, d_b), F32),
                   jax.ShapeDtypeStruct((b, n_heads, N_B, N_B), F32)],
        compiler_params=_cparams("arbitrary"),
        name="rwkv_step",
    )(pb.reshape(b, 1, nb), shift0.reshape(b, 1, nb), s0, *params)
    return o.reshape(b, d_b), s


def _gdn_gates(gates, alog_row, dtb_row):
    beta = _sigmoid(gates)
    g = -jnp.exp(alog_row) * _softplus(gates + dtb_row)
    return beta, g


def _gdn_chunk_kernel(qkv_ref, z_ref, gates_ref, conv0_ref, s0_ref, cw_ref, alog_ref, dtb_ref, onorm_ref,
                      o_ref, sfin_ref, carry_scr, s_scr, *, n_heads):
    c = pl.program_id(1)
    n_seq, t, _ = qkv_ref.shape
    dqk = n_heads * DK_C

    @pl.when(c == 0)
    def _():
        carry_scr[...] = jnp.zeros_like(carry_scr)
        carry_scr[:, SUBLANES - (DN_CONV_W - 1):SUBLANES, :] = conv0_ref[...]
        s_scr[...] = s0_ref[...]

    ys = []
    for i in range(n_seq):
        x = qkv_ref[i]
        carry = carry_scr[i]
        yi = cw_ref[DN_CONV_W - 1:DN_CONV_W, :] * x
        for d in range(1, DN_CONV_W):
            yi = yi + cw_ref[DN_CONV_W - 1 - d:DN_CONV_W - d, :] * _shift_rows(x, carry, d)
        carry_scr[i] = x[t - SUBLANES:t]
        ys.append(yi)
    y = _silu(jnp.concatenate(ys, axis=0))

    row = _iota2((t, t), 0)
    col = _iota2((t, t), 1)
    incl = row >= col
    strict = row > col
    rows = n_seq * t
    srow = _iota2((rows, rows), 0)
    scol = _iota2((rows, rows), 1)
    lg_t = t.bit_length() - 1
    same_seq = (srow >> lg_t) == (scol >> lg_t)
    gates = jnp.concatenate([gates_ref[i] for i in range(n_seq)], axis=0)
    beta_all, g_all = _gdn_gates(gates, alog_ref[...], dtb_ref[...])
    gcum_all = _dot_mask_lhs((same_seq & (srow >= scol)).astype(BF16), g_all)
    g3 = _split3(gcum_all)
    lane = _iota2((t, LANES), 1)
    z = jnp.concatenate([z_ref[i] for i in range(n_seq)], axis=0)

    items = [(i, h) for i in range(n_seq) for h in range(n_heads)]
    n = range(len(items))
    rs = lambda i: slice(i * t, (i + 1) * t)
    q = [_l2(y[rs(i), h * DK_C:(h + 1) * DK_C]) * (DK_C ** -0.5) for i, h in items]
    k = [_l2(y[rs(i), dqk + h * DK_C:dqk + (h + 1) * DK_C]) for i, h in items]
    v = [y[rs(i), 2 * dqk + h * DV_C:2 * dqk + (h + 1) * DV_C] for i, h in items]
    beta = [beta_all[rs(i), h:h + 1] for i, h in items]
    gc = [gcum_all[rs(i), n_heads + h:n_heads + h + 1] for i, h in items]
    egc = [jnp.exp(x) for x in gc]
    nt = lambda a, b: lax.dot_general(a, b, (((1,), (1,)), ((), ())), preferred_element_type=F32)
    gc_row = []
    for i, h in items:
        pk = (lane == n_heads + h).astype(BF16)
        gc_row.append(nt(pk, g3[0][rs(i)]) + (nt(pk, g3[1][rs(i)]) + nt(pk, g3[2][rs(i)])))
    lmask = [jnp.where(incl, jnp.exp(jnp.where(incl, gc[j] - gc_row[j], 0.0)), 0.0) for j in n]
    kb = [k[j] * beta[j] for j in n]
    kq = [_dot_nt(jnp.concatenate([kb[j], q[j]], axis=0), k[j]) for j in n]
    tinv = _tri_inv_multi([jnp.where(strict, kq[j][:t] * lmask[j], 0.0) for j in n])
    uw = [_dot(tinv[j], jnp.concatenate([v[j] * beta[j], kb[j] * egc[j]], axis=1)) for j in n]
    s = [s_scr[i, h] for i, h in items]
    ws = [_dot(jnp.concatenate([uw[j][:, DV_C:], q[j] * egc[j]], axis=0), s[j]) for j in n]
    v_new = [uw[j][:, :DV_C] - ws[j][:t] for j in n]
    o = [ws[j][t:] + _dot(jnp.where(incl, kq[j][t:] * lmask[j], 0.0), v_new[j]) for j in n]
    for j, (i, h) in enumerate(items):
        g_last = gc[j][t - 1:t, :]
        s_scr[i, h] = s[j] * jnp.exp(g_last) + _dot_tn(k[j] * jnp.exp(g_last - gc[j]), v_new[j])
    for j, (i, h) in enumerate(items):
        o_ref[i, :, h * DV_C:(h + 1) * DV_C] = (_rms(o[j], onorm_ref[...])
                                                * _silu(z[rs(i), h * DV_C:(h + 1) * DV_C]))

    @pl.when(c == pl.num_programs(1) - 1)
    def _():
        sfin_ref[...] = s_scr[...]


def _gdn_gate_params(a_log, dt_bias, n_heads):
    pad = lambda x: jnp.zeros((1, LANES), F32).at[0, n_heads:2 * n_heads].set(x)
    return pad(a_log), pad(dt_bias)


def gdn_chunked(qkvz, gates, conv0, s0, cw, a_log, dt_bias, onorm, batch, seq):
    m = qkvz.shape[0]
    n_heads = s0.shape[1]
    dconv = cw.shape[1]
    dvv = n_heads * DV_C
    assert qkvz.shape[1] == dconv + dvv and dconv % dvv == 0
    t = CHUNK
    ns = SEQS_PER_STEP if batch % SEQS_PER_STEP == 0 else 1
    assert seq % t == 0 and t & (t - 1) == 0
    nc = seq // t
    alog_row, dtb_row = _gdn_gate_params(a_log, dt_bias, n_heads)
    const = lambda b, c: (0, 0)
    qkvz3 = qkvz.reshape(batch, seq, dconv + dvv)
    o, s_fin = pl.pallas_call(
        functools.partial(_gdn_chunk_kernel, n_heads=n_heads),
        grid=(batch // ns, nc),
        in_specs=[pl.BlockSpec((ns, t, dconv), lambda b, c: (b, c, 0)),
                  pl.BlockSpec((ns, t, dvv), lambda b, c: (b, c, dconv // dvv)),
                  pl.BlockSpec((ns, t, LANES), lambda b, c: (b, c, 0)),
                  pl.BlockSpec((ns, DN_CONV_W - 1, dconv), lambda b, c: (b, 0, 0)),
                  pl.BlockSpec((ns, n_heads, DK_C, DV_C), lambda b, c: (b, 0, 0, 0)),
                  pl.BlockSpec((DN_CONV_W, dconv), const),
                  pl.BlockSpec((1, LANES), const),
                  pl.BlockSpec((1, LANES), const),
                  pl.BlockSpec((1, DV_C), const)],
        out_specs=[pl.BlockSpec((ns, t, dvv), lambda b, c: (b, c, 0)),
                   pl.BlockSpec((ns, n_heads, DK_C, DV_C), lambda b, c: (b, 0, 0, 0))],
        out_shape=[jax.ShapeDtypeStruct((batch, seq, dvv), F32),
                   jax.ShapeDtypeStruct((batch, n_heads, DK_C, DV_C), F32)],
        scratch_shapes=[pltpu.VMEM((ns, SUBLANES, dconv), F32),
                        pltpu.VMEM((ns, n_heads, DK_C, DV_C), F32)],
        compiler_params=_cparams("arbitrary", "arbitrary"),
        name="gdn_chunked",
    )(qkvz3, qkvz3, gates.reshape(batch, seq, LANES), conv0, s0, cw, alog_row, dtb_row,
      onorm.reshape(1, DV_C))
    return o.reshape(m, dvv), s_fin


def _gdn_step_kernel(qkv_ref, z_ref, gates_ref, conv0_ref, s0_ref, cw_ref, alog_ref, dtb_ref, onorm_ref,
                     o_ref, s_ref, *, n_heads):
    dqk = n_heads * DK_C
    x = qkv_ref[0]
    y = cw_ref[DN_CONV_W - 1:DN_CONV_W, :] * x
    for i in range(DN_CONV_W - 1):
        y = y + cw_ref[i:i + 1, :] * conv0_ref[0, i:i + 1, :]
    y = _silu(y)
    beta_all, g_all = _gdn_gates(gates_ref[0], alog_ref[...], dtb_ref[...])
    z = z_ref[0]
    eye = (_iota2((DK_C, DK_C), 0) == _iota2((DK_C, DK_C), 1)).astype(F32)
    for h in range(n_heads):
        q = _l2(y[:, h * DK_C:(h + 1) * DK_C]) * (DK_C ** -0.5)
        k = _l2(y[:, dqk + h * DK_C:dqk + (h + 1) * DK_C])
        v = y[:, 2 * dqk + h * DV_C:2 * dqk + (h + 1) * DV_C]
        beta = beta_all[:, h:h + 1]
        decay = jnp.exp(g_all[:, n_heads + h:n_heads + h + 1])
        s = s0_ref[0, h]
        k_col = _col_from_row(k, eye)
        q_col = _col_from_row(q, eye)
        v_new = beta * v - jnp.sum((k_col * (beta * decay)) * s, axis=0, keepdims=True)
        o = (jnp.sum((q_col * decay) * s, axis=0, keepdims=True)
             + jnp.sum(q * k, axis=-1, keepdims=True) * v_new)
        s_ref[0, h] = s * decay + k_col * v_new
        o_ref[0, :, h * DV_C:(h + 1) * DV_C] = (_rms(o, onorm_ref[...])
                                                * _silu(z[:, h * DV_C:(h + 1) * DV_C]))


def gdn_step(qkvz, gates, conv0, s0, cw, a_log, dt_bias, onorm):
    b = qkvz.shape[0]
    n_heads = s0.shape[1]
    dconv = cw.shape[1]
    dvv = n_heads * DV_C
    alog_row, dtb_row = _gdn_gate_params(a_log, dt_bias, n_heads)
    qkvz3 = qkvz.reshape(b, 1, dconv + dvv)
    const = lambda i: (0, 0)
    o, s = pl.pallas_call(
        functools.partial(_gdn_step_kernel, n_heads=n_heads),
        grid=(b,),
        in_specs=[pl.BlockSpec((1, 1, dconv), lambda i: (i, 0, 0)),
                  pl.BlockSpec((1, 1, dvv), lambda i: (i, 0, dconv // dvv)),
                  pl.BlockSpec((1, 1, LANES), lambda i: (i, 0, 0)),
                  pl.BlockSpec((1, DN_CONV_W - 1, dconv), lambda i: (i, 0, 0)),
                  pl.BlockSpec((1, n_heads, DK_C, DV_C), lambda i: (i, 0, 0, 0)),
                  pl.BlockSpec((DN_CONV_W, dconv), const),
                  pl.BlockSpec((1, LANES), const),
                  pl.BlockSpec((1, LANES), const),
                  pl.BlockSpec((1, DV_C), const)],
        out_specs=[pl.BlockSpec((1, 1, dvv), lambda i: (i, 0, 0)),
                   pl.BlockSpec((1, n_heads, DK_C, DV_C), lambda i: (i, 0, 0, 0))],
        out_shape=[jax.ShapeDtypeStruct((b, 1, dvv), F32),
                   jax.ShapeDtypeStruct((b, n_heads, DK_C, DV_C), F32)],
        compiler_params=_cparams("arbitrary"),
        name="gdn_step",
    )(qkvz3, qkvz3, gates.reshape(b, 1, LANES), conv0, s0, cw, alog_row, dtb_row,
      onorm.reshape(1, DV_C))
    return o.reshape(b, dvv), s


def _row_tile(m, want):
    return want if m % want == 0 else m


def _trunk(x, kv, wkv0, shift0, dn0, dnconv0, ffnconv0, p):
    b, t, d = x.shape
    m = b * t
    depth = p['norm_ffn'].shape[0]
    d_a = p['sb_bias'].shape[1] * DH_A
    nb = p['mu_b'].shape[1]
    dconv = p['conv_dn'].shape[2]
    n_heads_c = p['a_log_dn'].shape[1]
    dvv = n_heads_c * DV_C
    tm = _row_tile(m, PROJ_ROWS)
    tm_out = _row_tile(m, OUT_ROWS)
    new_k, new_v, new_wkv, new_shift, new_dn, new_dnconv, new_ffn = [], [], [], [], [], [], []
    x2 = x.reshape(m, d)
    for l in range(depth):
        i = l // 2
        if l % 2 == 0:
            qkv, pb = norm_matmul(x2, p['norm_mix_even'][i], p['w_in_even'][i], (3 * d_a, nb), tm)
            rp = dict(mu=p['mu_b'][i], w0=p['w0_b'][i], w_up=p['w_up_b'][i], a0=p['a0_b'][i],
                      a_up=p['a_up_b'][i], g_up=p['g_up_b'][i], k_k=p['kk_b'][i], k_a=p['ka_b'][i],
                      r_k=p['rk_b'][i], ln_w=p['lnx_w_b'][i], ln_b=p['lnx_b_b'][i])
            if kv is None:
                o_a, kn = sb_prompt(qkv, p['qnorm_a'][i], p['knorm_a'][i], p['sb_bias'][i], b, t)
                o_b, wkv = rwkv_chunked(pb, shift0[i], wkv0[i], rp, b, t)
                shift = pb.reshape(b, t, nb)[:, t - 1]
            else:
                o_a, kn = sb_decode(qkv, kv[0], kv[1], kv[2], i, p['qnorm_a'][i], p['knorm_a'][i],
                                    p['sb_bias'][i])
                o_b, wkv = rwkv_step(pb, shift0[i], wkv0[i], rp)
                shift = pb
            w_out = p['w_out_even'][i]
            x2 = matmul_res([o_a, o_b], [w_out[:d_a], w_out[d_a:]], x2, tm_out)
            new_k.append(kn.reshape(b, t, d_a // DH_A, DH_A))
            new_v.append(qkv[:, 2 * d_a:].reshape(b, t, d_a // DH_A, DH_A))
            new_wkv.append(wkv)
            new_shift.append(shift)
        else:
            qkvz, gates = norm_matmul(x2, p['norm_mix_odd'][i], p['w_in_odd'][i], (dconv + dvv, LANES), tm)
            if kv is None:
                o_c, s_fin = gdn_chunked(qkvz, gates, dnconv0[i], dn0[i], p['conv_dn'][i], p['a_log_dn'][i],
                                         p['dt_bias_dn'][i], p['onorm_dn'][i], b, t)
                cbuf = qkvz.reshape(b, t, dconv + dvv)[:, t - (DN_CONV_W - 1):, :dconv]
            else:
                o_c, s_fin = gdn_step(qkvz, gates, dnconv0[i], dn0[i], p['conv_dn'][i], p['a_log_dn'][i],
                                      p['dt_bias_dn'][i], p['onorm_dn'][i])
                cbuf = jnp.concatenate([dnconv0[i][:, 1:], qkvz[:, None, :dconv]], axis=1)
            x2 = matmul_res([o_c], [p['w_out_odd'][i]], x2, tm_out)
            new_dn.append(s_fin)
            new_dnconv.append(cbuf)
        ffn_args = (p['norm_ffn'][l], p['w_gate'][l], p['w_up'][l], p['conv_ffn'][l], p['conv_ffn_b'][l],
                    p['w_down'][l])
        if kv is None:
            x3, fbuf = conv_ffn(x2.reshape(b, t, d), ffnconv0[l], *ffn_args, tm=_row_tile(t, FFN_ROWS))
            x2 = x3.reshape(m, d)
        else:
            x2, fbuf = conv_ffn_step(x2, ffnconv0[l], *ffn_args)
        new_ffn.append(fbuf)
    states = tuple(jnp.stack(s) for s in (new_k, new_v, new_wkv, new_shift, new_dn, new_dnconv, new_ffn))
    return x2.reshape(b, t, d), states


def kernel(x_prompt, x_sample, cache_k, cache_v, page_table, state_wkv, state_shift, state_dn,
           state_dn_conv, state_ffn_conv, norm_mix_even, w_in_even, qnorm_a, knorm_a, sb_bias,
           mu_b, w0_b, w_up_b, a0_b, a_up_b, g_up_b, kk_b, ka_b, rk_b, lnx_w_b, lnx_b_b,
           w_out_even, norm_mix_odd, w_in_odd, conv_dn, a_log_dn, dt_bias_dn, onorm_dn, w_out_odd,
           norm_ffn, w_gate, w_up, conv_ffn, conv_ffn_b, w_down):
    n_heads_c = a_log_dn.shape[1]
    n_gate_cols = 2 * n_heads_c
    assert x_sample.shape[1] == 1 and n_gate_cols <= LANES
    w_in_odd_p = jnp.pad(w_in_odd, ((0, 0), (0, 0), (0, LANES - n_gate_cols)))
    bf = lambda w: w.astype(BF16)
    p = dict(norm_mix_even=norm_mix_even, w_in_even=bf(w_in_even), qnorm_a=qnorm_a, knorm_a=knorm_a,
             sb_bias=sb_bias, mu_b=mu_b, w0_b=w0_b, w_up_b=bf(w_up_b), a0_b=a0_b, a_up_b=bf(a_up_b),
             g_up_b=bf(g_up_b), kk_b=kk_b, ka_b=ka_b, rk_b=rk_b, lnx_w_b=lnx_w_b, lnx_b_b=lnx_b_b,
             w_out_even=bf(w_out_even), norm_mix_odd=norm_mix_odd, w_in_odd=bf(w_in_odd_p),
             conv_dn=conv_dn, a_log_dn=a_log_dn, dt_bias_dn=dt_bias_dn, onorm_dn=onorm_dn,
             w_out_odd=bf(w_out_odd), norm_ffn=norm_ffn, w_gate=bf(w_gate), w_up=bf(w_up),
             conv_ffn=conv_ffn, conv_ffn_b=conv_ffn_b, w_down=bf(w_down))
    b = x_prompt.shape[0]
    n_ab = state_wkv.shape[0]
    n_c = state_dn.shape[0]
    depth = norm_ffn.shape[0]
    zeros = lambda like, lead: jnp.zeros((lead, b) + like.shape[2:], F32)
    y_prompt, ps = _trunk(x_prompt, None, zeros(state_wkv, n_ab), zeros(state_shift, n_ab),
                          zeros(state_dn, n_c), zeros(state_dn_conv, n_c), zeros(state_ffn_conv, depth), p)
    y_sample, ss = _trunk(x_sample, (cache_k, cache_v, page_table), state_wkv, state_shift, state_dn,
                          state_dn_conv, state_ffn_conv, p)
    return (y_prompt, y_sample, ps[0], ps[1], ss[0], ss[1], ps[2], ss[2], ps[3], ss[3],
            ps[4], ss[4], ps[5], ss[5], ps[6], ss[6])
```

```python
import functools

import jax
import jax.numpy as jnp
from jax import lax
from jax.experimental import pallas as pl
from jax.experimental.pallas import tpu as pltpu

F32 = jnp.float32
BF16 = jnp.bfloat16

RMS_EPS = 1e-6
L2_EPS = 1e-6
GN_EPS = 64e-5

DH_A = 64
N_B = 64
R_W, R_A, R_G = 64, 64, 128
DK_C = 128
DV_C = 128
DN_CONV_W = 4
FFN_CONV_W = 3

SB_BLOCK = 128
CHUNK = 64
SEQS_PER_STEP = 2
SUBLANES = 8
LANES = 128
VMEM_LIMIT_BYTES = 56 * 1024 * 1024
PROJ_ROWS = 512
OUT_ROWS = 1024
FFN_ROWS = 512
FFN_COLS = 256


def _cparams(*sem):
    return pltpu.CompilerParams(dimension_semantics=sem, vmem_limit_bytes=VMEM_LIMIT_BYTES)


def _dot(a, b):
    return jnp.dot(a.astype(BF16), b.astype(BF16), preferred_element_type=F32)


def _dot_nt(a, b):
    return lax.dot_general(a.astype(BF16), b.astype(BF16), (((1,), (1,)), ((), ())),
                           preferred_element_type=F32)


def _dot_tn(a, b):
    return lax.dot_general(a.astype(BF16), b.astype(BF16), (((0,), (0,)), ((), ())),
                           preferred_element_type=F32)


def _split2(a):
    hi = a.astype(BF16)
    lo = (a - hi.astype(F32)).astype(BF16)
    return hi, lo


def _split3(a):
    hi = a.astype(BF16)
    r1 = a - hi.astype(F32)
    mid = r1.astype(BF16)
    lo = (r1 - mid.astype(F32)).astype(BF16)
    return hi, mid, lo


def _dot_mask_lhs(mask_bf16, x):
    d = functools.partial(jnp.dot, preferred_element_type=F32)
    h, m, l = _split3(x)
    return d(mask_bf16, h) + (d(mask_bf16, m) + d(mask_bf16, l))


def _softplus(x):
    return jnp.maximum(x, 0.0) + jnp.log(1.0 + jnp.exp(-jnp.abs(x)))


def _sigmoid(x):
    return 1.0 / (1.0 + jnp.exp(-x))


def _silu(x):
    return x * _sigmoid(x)


def _rms(x, g):
    return x * lax.rsqrt(jnp.mean(x * x, axis=-1, keepdims=True) + RMS_EPS) * g


def _l2(x):
    return x * lax.rsqrt(jnp.sum(x * x, axis=-1, keepdims=True) + L2_EPS)


def _iota2(shape, dim):
    return lax.broadcasted_iota(jnp.int32, shape, dim)


def _shift_rows(cur, carry, d):
    rolled = pltpu.roll(cur, d, 0)
    crolled = pltpu.roll(carry, d, 0)
    first = jnp.where(_iota2(carry.shape, 0) < d, crolled, rolled[:SUBLANES])
    if cur.shape[0] == SUBLANES:
        return first
    return jnp.concatenate([first, rolled[SUBLANES:]], axis=0)


def _tri_inv_multi(a_list):
    n = a_list[0].shape[0]
    row = _iota2((n, n), 0)
    col = _iota2((n, n), 1)
    eye = (row == col).astype(F32)
    d = functools.partial(jnp.dot, preferred_element_type=F32)
    ts = [eye - jnp.where((row >> 1) == (col >> 1), a, 0.0) for a in a_list]
    s = 2
    while s < n:
        sh = s.bit_length() - 1
        sub = ((row >> (sh + 1)) == (col >> (sh + 1))) & ((row >> sh) != (col >> sh))
        tb = [t.astype(BF16) for t in ts]
        tes = [d(t, jnp.where(sub, a, 0.0).astype(BF16)).astype(BF16) for t, a in zip(tb, a_list)]
        ts = [t - d(te, t16) for t, te, t16 in zip(ts, tes, tb)]
        s *= 2
    return ts


def _norm_matmul_kernel(x_ref, g_ref, w_ref, *o_refs, splits):
    h = _rms(x_ref[...], g_ref[...]).astype(BF16)
    y = jnp.dot(h, w_ref[...], preferred_element_type=F32)
    off = 0
    for o_ref, n in zip(o_refs, splits):
        o_ref[...] = y[:, off:off + n]
        off += n


def norm_matmul(x, g, w, splits, tm):
    m, k = x.shape
    n = w.shape[1]
    assert sum(splits) == n and m % tm == 0
    return pl.pallas_call(
        functools.partial(_norm_matmul_kernel, splits=tuple(splits)),
        grid=(m // tm,),
        in_specs=[pl.BlockSpec((tm, k), lambda i: (i, 0)),
                  pl.BlockSpec((1, k), lambda i: (0, 0)),
                  pl.BlockSpec((k, n), lambda i: (0, 0))],
        out_specs=[pl.BlockSpec((tm, s), lambda i: (i, 0)) for s in splits],
        out_shape=[jax.ShapeDtypeStruct((m, s), F32) for s in splits],
        compiler_params=_cparams("arbitrary"),
        name="norm_matmul",
    )(x, g.reshape(1, k), w)


def _matmul_res_kernel(*refs, n_in):
    a_refs = refs[:n_in]
    w_refs = refs[n_in:2 * n_in]
    res_ref = refs[2 * n_in]
    o_ref = refs[2 * n_in + 1]
    acc = res_ref[...]
    for a_ref, w_ref in zip(a_refs, w_refs):
        acc = acc + jnp.dot(a_ref[...].astype(BF16), w_ref[...], preferred_element_type=F32)
    o_ref[...] = acc


def matmul_res(a_list, w_list, res, tm):
    m, n = res.shape
    assert m % tm == 0
    n_in = len(a_list)
    in_specs = ([pl.BlockSpec((tm, a.shape[1]), lambda i: (i, 0)) for a in a_list]
                + [pl.BlockSpec(w.shape, lambda i: (0, 0)) for w in w_list]
                + [pl.BlockSpec((tm, n), lambda i: (i, 0))])
    return pl.pallas_call(
        functools.partial(_matmul_res_kernel, n_in=n_in),
        grid=(m // tm,),
        in_specs=in_specs,
        out_specs=pl.BlockSpec((tm, n), lambda i: (i, 0)),
        out_shape=jax.ShapeDtypeStruct((m, n), F32),
        compiler_params=_cparams("arbitrary"),
        name="matmul_res",
    )(*a_list, *w_list, res)


def _ffn_kernel(x_ref, buf_ref, g_ref, wg_ref, wu_ref, cw_ref, cb_ref, wd_ref,
                y_ref, tail_ref, carry_scr, act_scr):
    ti = pl.program_id(1)
    x = x_ref[0]
    tm = x.shape[0]
    f = wg_ref.shape[1]
    h = _rms(x, g_ref[...]).astype(BF16)

    @pl.when(ti == 0)
    def _():
        carry_scr[...] = jnp.zeros_like(carry_scr)
        carry_scr[SUBLANES - 2:SUBLANES, :] = buf_ref[0]

    for c0 in range(0, f, FFN_COLS):
        cols = slice(c0, min(c0 + FFN_COLS, f))
        gate = jnp.dot(h, wg_ref[:, cols], preferred_element_type=F32)
        up = jnp.dot(h, wu_ref[:, cols], preferred_element_type=F32)
        carry = carry_scr[:, cols]
        conv = (cw_ref[2:3, cols] * gate + cw_ref[1:2, cols] * _shift_rows(gate, carry, 1)
                + cw_ref[0:1, cols] * _shift_rows(gate, carry, 2))
        carry_scr[:, cols] = gate[tm - SUBLANES:tm]
        tail_ref[0, :, cols] = gate[tm - 2:tm]
        act_scr[:, cols] = (_silu(conv + cb_ref[:, cols]) * up).astype(BF16)
    y_ref[0] = x + jnp.dot(act_scr[...], wd_ref[...], preferred_element_type=F32)


def conv_ffn(x, buf, g, wg, wu, cw, cb, wd, tm):
    b, t, d = x.shape
    f = wg.shape[1]
    assert t % tm == 0 and tm % SUBLANES == 0
    const = lambda bi, ti: (0, 0)
    return pl.pallas_call(
        _ffn_kernel,
        grid=(b, t // tm),
        in_specs=[pl.BlockSpec((1, tm, d), lambda bi, ti: (bi, ti, 0)),
                  pl.BlockSpec((1, 2, f), lambda bi, ti: (bi, 0, 0)),
                  pl.BlockSpec((1, d), const),
                  pl.BlockSpec((d, f), const),
                  pl.BlockSpec((d, f), const),
                  pl.BlockSpec((FFN_CONV_W, f), const),
                  pl.BlockSpec((1, f), const),
                  pl.BlockSpec((f, d), const)],
        out_specs=[pl.BlockSpec((1, tm, d), lambda bi, ti: (bi, ti, 0)),
                   pl.BlockSpec((1, 2, f), lambda bi, ti: (bi, 0, 0))],
        out_shape=[jax.ShapeDtypeStruct((b, t, d), F32),
                   jax.ShapeDtypeStruct((b, 2, f), F32)],
        scratch_shapes=[pltpu.VMEM((SUBLANES, f), F32), pltpu.VMEM((tm, f), BF16)],
        compiler_params=_cparams("arbitrary", "arbitrary"),
        name="conv_ffn",
    )(x, buf, g.reshape(1, d), wg, wu, cw, cb.reshape(1, f), wd)


def _ffn_step_kernel(x_ref, buf0_ref, buf1_ref, g_ref, wg_ref, wu_ref, cw_ref, cb_ref, wd_ref,
                     y_ref, gate_ref):
    x = x_ref[...]
    h = _rms(x, g_ref[...]).astype(BF16)
    gate = jnp.dot(h, wg_ref[...], preferred_element_type=F32)
    up = jnp.dot(h, wu_ref[...], preferred_element_type=F32)
    conv = cw_ref[2:3, :] * gate + cw_ref[1:2, :] * buf1_ref[...] + cw_ref[0:1, :] * buf0_ref[...]
    act = (_silu(conv + cb_ref[...]) * up).astype(BF16)
    y_ref[...] = x + jnp.dot(act, wd_ref[...], preferred_element_type=F32)
    gate_ref[...] = gate


def conv_ffn_step(x, buf, g, wg, wu, cw, cb, wd):
    b, d = x.shape
    f = wg.shape[1]
    full = lambda shape: pl.BlockSpec(shape, lambda i: (0,) * len(shape))
    y, gate = pl.pallas_call(
        _ffn_step_kernel,
        grid=(1,),
        in_specs=[full((b, d)), full((b, f)), full((b, f)), full((1, d)), full((d, f)), full((d, f)),
                  full((FFN_CONV_W, f)), full((1, f)), full((f, d))],
        out_specs=[full((b, d)), full((b, f))],
        out_shape=[jax.ShapeDtypeStruct((b, d), F32), jax.ShapeDtypeStruct((b, f), F32)],
        compiler_params=_cparams("arbitrary"),
        name="conv_ffn_step",
    )(x, buf[:, 0], buf[:, 1], g.reshape(1, d), wg, wu, cw, cb.reshape(1, f), wd)
    return y, jnp.stack([buf[:, 1], gate], axis=1)


def _pair_rms(x, w_row):
    left = _iota2(x.shape, 1) < DH_A
    x2 = x * x
    ms_l = jnp.sum(jnp.where(left, x2, 0.0), axis=-1, keepdims=True) * (1.0 / DH_A)
    ms_r = jnp.sum(jnp.where(left, 0.0, x2), axis=-1, keepdims=True) * (1.0 / DH_A)
    inv = jnp.where(left, lax.rsqrt(ms_l + RMS_EPS), lax.rsqrt(ms_r + RMS_EPS))
    return x * inv * w_row


def _sb_prompt_kernel(bias_ref, q_ref, k_ref, v_ref, qw_ref, kw_ref, o_ref, kn_ref,
                      kn_scr, v_scr, q_scr, later_scr, o_scr, tail_scr, *, n_heads):
    qi = pl.program_id(1)
    blk = q_ref.shape[0]
    n_pairs = n_heads // 2
    pw = 2 * DH_A
    scale = DH_A ** -0.5
    row0 = pl.multiple_of(qi * blk, blk)

    k = k_ref[...]
    kn = jnp.concatenate([_pair_rms(k[:, p * pw:(p + 1) * pw], kw_ref[...]) for p in range(n_pairs)],
                         axis=-1)
    kn_ref[...] = kn
    kn_scr[pl.ds(row0, blk), :] = kn.astype(BF16)
    v_scr[pl.ds(row0, blk), :] = v_ref[...].astype(BF16)

    q = q_ref[...]
    left = _iota2((blk, pw), 1) < DH_A
    for p in range(n_pairs):
        qn = _pair_rms(q[:, p * pw:(p + 1) * pw], qw_ref[...]) * scale
        q_scr[p] = jnp.concatenate([jnp.where(left, qn, 0.0), jnp.where(left, 0.0, qn)],
                                   axis=0).astype(BF16)
    krow = _iota2((2 * blk, 2 * blk), 0)
    krow = jnp.where(krow >= blk, krow - blk, krow)
    kcol = _iota2((2 * blk, 2 * blk), 1)
    later_scr[...] = jnp.where((kcol >= blk) | (krow > kcol), 1.0, 0.0).astype(BF16)
    o_scr[...] = jnp.zeros_like(o_scr)
    tail_scr[...] = jnp.zeros_like(tail_scr)

    qrow = _iota2((2 * blk, blk), 0)
    causal = _iota2((2 * blk, blk), 1) < jnp.where(qrow >= blk, qrow - blk, qrow)
    vleft = _iota2((blk, pw), 1) < DH_A

    def blocks(offs, diag):
        items = [(j, p) for j in range(len(offs)) for p in range(n_pairs)]
        zs = []
        for j, p in items:
            z2 = lax.dot_general(q_scr[p], kn_scr[pl.ds(offs[j], blk), p * pw:(p + 1) * pw],
                                 (((1,), (1,)), ((), ())), preferred_element_type=F32)
            zs.append(jnp.concatenate([z2[:blk] + bias_ref[2 * p], z2[blk:] + bias_ref[2 * p + 1]],
                                      axis=0))
        sps = [_softplus(z) for z in zs]
        if diag:
            sps = [jnp.where(causal, sp, 0.0) for sp in sps]
        his = [sp.astype(BF16) for sp in sps]
        los = [(sp - hi.astype(F32)).astype(BF16) for sp, hi in zip(sps, his)]
        lo_mat = later_scr[...]
        cums = [jnp.dot(jnp.concatenate([hi, lo], axis=1), lo_mat, preferred_element_type=F32)
                for hi, lo in zip(his, los)]
        ws = []
        for p in range(n_pairs):
            tail = tail_scr[p]
            for j in range(len(offs)):
                n = j * n_pairs + p
                w = jnp.exp((zs[n] - sps[n]) - (cums[n][:, :blk] + tail))
                if diag:
                    w = jnp.where(causal, w, 0.0)
                tail = tail + cums[n][:, blk:]
                ws.append(w.astype(BF16))
            tail_scr[p] = tail
        for p in range(n_pairs):
            acc = o_scr[:, p * pw:(p + 1) * pw]
            for j in range(len(offs)):
                w = ws[p * len(offs) + j]
                vj = v_scr[pl.ds(offs[j], blk), p * pw:(p + 1) * pw]
                zero = jnp.zeros_like(vj)
                v_bd = jnp.concatenate([jnp.where(vleft, vj, zero), jnp.where(vleft, zero, vj)], axis=0)
                acc = acc + jnp.dot(jnp.concatenate([w[:blk], w[blk:]], axis=1), v_bd,
                                    preferred_element_type=F32)
            o_scr[:, p * pw:(p + 1) * pw] = acc

    blocks([row0], True)

    def body(i, carry):
        hi_off = pl.multiple_of((qi - 1 - 2 * i) * blk, blk)
        blocks([hi_off, pl.multiple_of(hi_off - blk, blk)], False)
        return carry

    lax.fori_loop(0, qi // 2, body, 0)

    @pl.when(qi % 2 == 1)
    def _():
        blocks([0], False)

    o_ref[...] = o_scr[...]


def sb_prompt(qkv, qw, kw, bias, batch, seq):
    m, n3 = qkv.shape
    d_a = n3 // 3
    n_heads = d_a // DH_A
    blk = SB_BLOCK
    assert seq % blk == 0 and n_heads % 2 == 0 and 2 * DH_A == LANES
    nq = seq // blk
    rowblk = lambda col: (lambda b, qi: (b * nq + qi, col))
    pair_w = lambda w: jnp.tile(w, 2).reshape(1, 2 * DH_A)
    return pl.pallas_call(
        functools.partial(_sb_prompt_kernel, n_heads=n_heads),
        grid=(batch, nq),
        in_specs=[pl.BlockSpec(memory_space=pltpu.SMEM),
                  pl.BlockSpec((blk, d_a), rowblk(0)),
                  pl.BlockSpec((blk, d_a), rowblk(1)),
                  pl.BlockSpec((blk, d_a), rowblk(2)),
                  pl.BlockSpec((1, 2 * DH_A), lambda b, qi: (0, 0)),
                  pl.BlockSpec((1, 2 * DH_A), lambda b, qi: (0, 0))],
        out_specs=[pl.BlockSpec((blk, d_a), rowblk(0)),
                   pl.BlockSpec((blk, d_a), rowblk(0))],
        out_shape=[jax.ShapeDtypeStruct((m, d_a), F32), jax.ShapeDtypeStruct((m, d_a), F32)],
        scratch_shapes=[pltpu.VMEM((seq, d_a), BF16), pltpu.VMEM((seq, d_a), BF16),
                        pltpu.VMEM((n_heads // 2, 2 * blk, 2 * DH_A), BF16),
                        pltpu.VMEM((2 * blk, 2 * blk), BF16),
                        pltpu.VMEM((blk, d_a), F32),
                        pltpu.VMEM((n_heads // 2, 2 * blk, blk), F32)],
        compiler_params=_cparams("arbitrary", "arbitrary"),
        name="sb_prompt",
    )(bias, qkv, qkv, qkv, pair_w(qw), pair_w(kw))


DECODE_BANK = 4
DECODE_SLOTS = 16


def _sb_decode_kernel(pt_ref, q_ref, k_ref, ck_hbm, cv_hbm, qw_ref, kw_ref, bias_ref,
                      o_ref, kn_ref, kbuf, vbuf, sem, *, n_heads, n_pages, layer):
    b = pl.program_id(0)
    page = kbuf.shape[3]
    scale = DH_A ** -0.5
    nt = lambda x, y: lax.dot_general(x, y, (((1,), (1,)), ((), ())), preferred_element_type=F32)

    def copies(i, slot):
        phys = pt_ref[b, n_pages - 1 - i]
        return (pltpu.make_async_copy(ck_hbm.at[layer, phys], kbuf.at[slot], sem.at[0, slot]),
                pltpu.make_async_copy(cv_hbm.at[layer, phys], vbuf.at[slot], sem.at[1, slot]))

    for j in range(DECODE_SLOTS):
        for cp in copies(j, j):
            cp.start()

    q = q_ref[0]
    k = k_ref[0]
    hrow = _iota2((n_heads, DH_A), 0)
    qe, kn = [], []
    for h in range(n_heads):
        lanes = slice(h * DH_A, (h + 1) * DH_A)
        qn = _rms(q[:, lanes], qw_ref[...]) * scale
        qe.append(jnp.where(hrow == h, jnp.broadcast_to(qn, (n_heads, DH_A)), 0.0).astype(BF16))
        kn.append(_rms(k[:, lanes], kw_ref[...]))
    kn_ref[0] = jnp.concatenate(kn, axis=-1)

    krow = _iota2((page, 2 * page), 0)
    kcol = _iota2((page, 2 * page), 1)
    later_ones = jnp.where((kcol >= page) | (krow > kcol), 1.0, 0.0).astype(BF16)
    bias = bias_ref[...]

    def bank_pages(first, bank, tail, acc):
        pages = range(DECODE_BANK)
        slots = [bank * DECODE_BANK + j for j in pages]
        for j in pages:
            for cp in copies(first + j, slots[j]):
                cp.wait()
        zs = []
        for j in pages:
            z = bias
            for h in range(n_heads):
                z = z + jnp.dot(qe[h], kbuf[slots[j], h].astype(BF16), preferred_element_type=F32)
            zs.append(z)
        sps = [_softplus(z) for z in zs]
        his = [sp.astype(BF16) for sp in sps]
        los = [(sp - hi.astype(F32)).astype(BF16) for sp, hi in zip(sps, his)]
        cums = [jnp.dot(hi, later_ones, preferred_element_type=F32)
                + jnp.dot(lo, later_ones, preferred_element_type=F32) for hi, lo in zip(his, los)]
        ws = []
        for j in pages:
            ws.append(jnp.exp((zs[j] - sps[j]) - (cums[j][:, :page] + tail)).astype(BF16))
            tail = tail + cums[j][:, page:]
        for j in pages:
            acc = [acc[h] + nt(ws[j], vbuf[slots[j], h].astype(BF16)) for h in range(n_heads)]

        @pl.when(first + DECODE_SLOTS < n_pages)
        def _():
            for j in pages:
                for cp in copies(first + DECODE_SLOTS + j, slots[j]):
                    cp.start()

        return tail, acc

    def group(gi, carry):
        tail, acc = carry
        for bank in range(DECODE_SLOTS // DECODE_BANK):
            tail, acc = bank_pages(gi * DECODE_SLOTS + bank * DECODE_BANK, bank, tail, acc)
        return tail, acc

    zero = jnp.zeros((n_heads, DH_A), F32)
    _, acc = lax.fori_loop(0, n_pages // DECODE_SLOTS, group,
                           (jnp.zeros((n_heads, page), F32), [zero] * n_heads))
    o_ref[0] = jnp.concatenate([acc[h][h:h + 1, :] for h in range(n_heads)], axis=-1)


def sb_decode(qkv, cache_k, cache_v, page_table, layer, qw, kw, bias):
    b, n3 = qkv.shape
    d_a = n3 // 3
    n_heads = d_a // DH_A
    n_layers, n_phys, page = cache_k.shape[:3]
    n_pages = page_table.shape[1]
    assert n_pages % DECODE_SLOTS == 0 and cache_k.shape[3] == n_heads
    ck_t = jnp.transpose(cache_k, (0, 1, 3, 4, 2))
    cv_t = jnp.transpose(cache_v, (0, 1, 3, 4, 2))
    qkv3 = qkv.reshape(b, 1, n3)
    grid_spec = pltpu.PrefetchScalarGridSpec(
        num_scalar_prefetch=1,
        grid=(b,),
        in_specs=[pl.BlockSpec((1, 1, d_a), lambda bi, pt: (bi, 0, 0)),
                  pl.BlockSpec((1, 1, d_a), lambda bi, pt: (bi, 0, 1)),
                  pl.BlockSpec(memory_space=pl.ANY),
                  pl.BlockSpec(memory_space=pl.ANY),
                  pl.BlockSpec((1, DH_A), lambda bi, pt: (0, 0)),
                  pl.BlockSpec((1, DH_A), lambda bi, pt: (0, 0)),
                  pl.BlockSpec((n_heads, 1), lambda bi, pt: (0, 0))],
        out_specs=[pl.BlockSpec((1, 1, d_a), lambda bi, pt: (bi, 0, 0)),
                   pl.BlockSpec((1, 1, d_a), lambda bi, pt: (bi, 0, 0))],
        scratch_shapes=[pltpu.VMEM((DECODE_SLOTS, n_heads, DH_A, page), F32),
                        pltpu.VMEM((DECODE_SLOTS, n_heads, DH_A, page), F32),
                        pltpu.SemaphoreType.DMA((2, DECODE_SLOTS))],
    )
    o, kn = pl.pallas_call(
        functools.partial(_sb_decode_kernel, n_heads=n_heads, n_pages=n_pages, layer=layer),
        grid_spec=grid_spec,
        out_shape=[jax.ShapeDtypeStruct((b, 1, d_a), F32), jax.ShapeDtypeStruct((b, 1, d_a), F32)],
        compiler_params=_cparams("arbitrary"),
        name="sb_decode",
    )(page_table, qkv3, qkv3, ck_t, cv_t, qw.reshape(1, DH_A), kw.reshape(1, DH_A),
      bias.reshape(n_heads, 1))
    return o.reshape(b, d_a), kn.reshape(b, d_a)


def _rwkv_tokens(pb, prev, mu, w0, w_up, a0, a_up, g_up, k_k, k_a, d_b):
    xs = pb + (prev - pb) * mu
    r = xs[:, 0:d_b]
    k = xs[:, d_b:2 * d_b]
    v = xs[:, 2 * d_b:3 * d_b]
    xw = xs[:, 3 * d_b:3 * d_b + R_W]
    xa = xs[:, 3 * d_b + R_W:3 * d_b + R_W + R_A]
    xg = xs[:, 3 * d_b + R_W + R_A:3 * d_b + R_W + R_A + R_G]
    w_log = -_softplus(-(w0 + _dot(jnp.tanh(xw), w_up))) - 0.5
    log_decay = -jnp.exp(w_log)
    a = _sigmoid(a0 + _dot(xa, a_up))
    g = _dot(_sigmoid(xg), g_up)
    kk_raw = k * k_k
    k_mod = k * (1.0 + (a - 1.0) * k_a)
    return r, k_mod, v, kk_raw, a, g, log_decay


def _rwkv_finish(o, r, k_mod, v, g, r_k, ln_w, ln_b):
    mean = jnp.mean(o, axis=-1, keepdims=True)
    var = jnp.mean(jnp.square(o - mean), axis=-1, keepdims=True)
    on = (o - mean) * lax.rsqrt(var + GN_EPS) * ln_w + ln_b
    bonus = jnp.sum(r * k_mod * r_k, axis=-1, keepdims=True) * v
    return (on + bonus) * g


def _rwkv_chunk_kernel(pb_ref, shift0_ref, s0_ref, mu_ref, w0_ref, wup_ref, a0_ref, aup_ref, gup_ref,
                       kk_ref, ka_ref, rk_ref, lnw_ref, lnb_ref, hsum_ref, o_ref, sfin_ref,
                       carry_scr, s_scr, o_scr, *, n_heads):
    c = pl.program_id(1)
    d_b = n_heads * N_B
    n_seq, t, nb = pb_ref.shape

    @pl.when(c == 0)
    def _():
        carry_scr[...] = jnp.zeros_like(carry_scr)
        carry_scr[:, SUBLANES - 1:SUBLANES, :] = shift0_ref[...]
        s_scr[...] = s0_ref[...]

    pbs = [pb_ref[i] for i in range(n_seq)]
    prev = jnp.concatenate([_shift_rows(pbs[i], carry_scr[i], 1) for i in range(n_seq)], axis=0)
    for i in range(n_seq):
        carry_scr[i] = pbs[i][t - SUBLANES:t]
    r, k_mod, v, kk_raw, a, g, log_decay = _rwkv_tokens(
        jnp.concatenate(pbs, axis=0), prev, mu_ref[...], w0_ref[...], wup_ref[...], a0_ref[...],
        aup_ref[...], gup_ref[...], kk_ref[...], ka_ref[...], d_b)

    row = _iota2((t, t), 0)
    col = _iota2((t, t), 1)
    incl = row >= col
    strict = row > col
    rows = n_seq * t
    srow = _iota2((rows, rows), 0)
    scol = _iota2((rows, rows), 1)
    lg_t = t.bit_length() - 1
    same_seq = (srow >> lg_t) == (scol >> lg_t)
    cum = _dot_mask_lhs((same_seq & (srow >= scol)).astype(BF16), log_decay)
    cum_prev = cum - log_decay
    cum_last = jnp.concatenate(
        [jnp.broadcast_to(cum[(i + 1) * t - 1:(i + 1) * t, :], (t, d_b)) for i in range(n_seq)], axis=0)
    e_cum = jnp.exp(cum)
    e_prev = jnp.exp(cum_prev)
    e_neg = jnp.exp(-cum)
    e_rest = jnp.exp(cum_last - cum)
    e_last = jnp.exp(cum_last)

    items = [(i, h) for i in range(n_seq) for h in range(n_heads)]
    n = range(len(items))
    sub = lambda z, i, h: z[i * t:(i + 1) * t, h * N_B:(h + 1) * N_B]
    pick = lambda z: [sub(z, i, h) for i, h in items]
    def head_sum(z):
        z_hi, z_lo = _split2(z)
        return (jnp.dot(z_hi, hsum_ref[...], preferred_element_type=F32)
                + jnp.dot(z_lo, hsum_ref[...], preferred_element_type=F32))

    kk_all = kk_raw * lax.rsqrt(head_sum(kk_raw * kk_raw) + L2_EPS)
    ka_all = kk_all * a
    v_ = pick(v)
    x = [jnp.concatenate([p, q], axis=0) for p, q in zip(pick(kk_all * e_prev), pick(r * e_cum))]
    khat = pick(k_mod * e_neg)
    bhat = pick(ka_all * e_neg)
    k_rest = pick(k_mod * e_rest)
    b_rest = pick(ka_all * e_rest)
    elast_ = pick(e_last)
    s = [s_scr[i, h] for i, h in items]
    xb = [_dot_nt(x[j], bhat[j]) for j in n]
    xk = [_dot_nt(x[j], khat[j]) for j in n]
    xs = [_dot_nt(x[j], s[j]) for j in n]
    tinv = _tri_inv_multi([jnp.where(strict, xb[j][:t], 0.0) for j in n])
    rhs = [xs[j][:t] + _dot(jnp.where(strict, xk[j][:t], 0.0), v_[j]) for j in n]
    u = [_dot(tinv[j], rhs[j]) for j in n]
    o = [xs[j][t:] + _dot(jnp.where(incl, xk[j][t:], 0.0), v_[j])
         - _dot(jnp.where(incl, xb[j][t:], 0.0), u[j]) for j in n]
    for j, (i, h) in enumerate(items):
        s_scr[i, h] = s[j] * elast_[j][:1] + _dot_tn(
            jnp.concatenate([v_[j], -u[j]], axis=0),
            jnp.concatenate([k_rest[j], b_rest[j]], axis=0))
    for j, (i, h) in enumerate(items):
        o_scr[i * t:(i + 1) * t, h * N_B:(h + 1) * N_B] = o[j]
    o_all = o_scr[...]
    inv_n = 1.0 / N_B
    cen = o_all - head_sum(o_all) * inv_n
    var = head_sum(cen * cen) * inv_n
    on = cen * lax.rsqrt(var + GN_EPS) * lnw_ref[...] + lnb_ref[...]
    out = (on + head_sum(r * k_mod * rk_ref[...]) * v) * g
    for i in range(n_seq):
        o_ref[i] = out[i * t:(i + 1) * t]

    @pl.when(c == pl.num_programs(1) - 1)
    def _():
        sfin_ref[...] = s_scr[...]


def _rwkv_params(p, d_b):
    row = lambda x: x.reshape(1, -1)
    return (row(p['mu']), row(p['w0']), p['w_up'], row(p['a0']), p['a_up'], p['g_up'],
            row(p['k_k']), row(p['k_a']), row(p['r_k']), row(p['ln_w']), row(p['ln_b']))


def rwkv_chunked(pb, shift0, s0, p, batch, seq):
    m, nb = pb.shape
    n_heads = s0.shape[1]
    d_b = n_heads * N_B
    t = CHUNK
    ns = SEQS_PER_STEP if batch % SEQS_PER_STEP == 0 else 1
    assert seq % t == 0 and t & (t - 1) == 0
    nc = seq // t
    head_sum = jnp.kron(jnp.eye(n_heads, dtype=F32), jnp.ones((N_B, N_B), F32)).astype(BF16)
    params = _rwkv_params(p, d_b) + (head_sum,)
    const = lambda b, c: (0, 0)
    o, s_fin = pl.pallas_call(
        functools.partial(_rwkv_chunk_kernel, n_heads=n_heads),
        grid=(batch // ns, nc),
        in_specs=[pl.BlockSpec((ns, t, nb), lambda b, c: (b, c, 0)),
                  pl.BlockSpec((ns, 1, nb), lambda b, c: (b, 0, 0)),
                  pl.BlockSpec((ns, n_heads, N_B, N_B), lambda b, c: (b, 0, 0, 0))]
                 + [pl.BlockSpec(x.shape, const) for x in params],
        out_specs=[pl.BlockSpec((ns, t, d_b), lambda b, c: (b, c, 0)),
                   pl.BlockSpec((ns, n_heads, N_B, N_B), lambda b, c: (b, 0, 0, 0))],
        out_shape=[jax.ShapeDtypeStruct((batch, seq, d_b), F32),
                   jax.ShapeDtypeStruct((batch, n_heads, N_B, N_B), F32)],
        scratch_shapes=[pltpu.VMEM((ns, SUBLANES, nb), F32), pltpu.VMEM((ns, n_heads, N_B, N_B), F32),
                        pltpu.VMEM((ns * t, d_b), F32)],
        compiler_params=_cparams("arbitrary", "arbitrary"),
        name="rwkv_chunked",
    )(pb.reshape(batch, seq, nb), shift0.reshape(batch, 1, nb), s0, *params)
    return o.reshape(m, d_b), s_fin


def _col_from_row(x_row, eye):
    return jnp.sum(eye * x_row, axis=-1, keepdims=True)


def _row_from_col(x_col, eye):
    return jnp.sum(eye * x_col, axis=0, keepdims=True)


def _rwkv_step_kernel(pb_ref, shift0_ref, s0_ref, mu_ref, w0_ref, wup_ref, a0_ref, aup_ref, gup_ref,
                      kk_ref, ka_ref, rk_ref, lnw_ref, lnb_ref, o_ref, s_ref, *, n_heads):
    d_b = n_heads * N_B
    r, k_mod, v, kk_raw, a, g, log_decay = _rwkv_tokens(
        pb_ref[0], shift0_ref[0], mu_ref[...], w0_ref[...], wup_ref[...], a0_ref[...], aup_ref[...],
        gup_ref[...], kk_ref[...], ka_ref[...], d_b)
    decay = jnp.exp(log_decay)
    eye = (_iota2((N_B, N_B), 0) == _iota2((N_B, N_B), 1)).astype(F32)
    for h in range(n_heads):
        lanes = slice(h * N_B, (h + 1) * N_B)
        kk = _l2(kk_raw[:, lanes])
        s = s0_ref[0, h]
        sa = jnp.sum(s * kk, axis=-1, keepdims=True)
        v_col = _col_from_row(v[:, lanes], eye)
        s = s * decay[:, lanes] - sa * (kk * a[:, lanes]) + v_col * k_mod[:, lanes]
        s_ref[0, h] = s
        o = _row_from_col(jnp.sum(s * r[:, lanes], axis=-1, keepdims=True), eye)
        o_ref[0, :, lanes] = _rwkv_finish(o, r[:, lanes], k_mod[:, lanes], v[:, lanes], g[:, lanes],
                                          rk_ref[:, lanes], lnw_ref[:, lanes], lnb_ref[:, lanes])


def rwkv_step(pb, shift0, s0, p):
    b, nb = pb.shape
    n_heads = s0.shape[1]
    d_b = n_heads * N_B
    params = _rwkv_params(p, d_b)
    o, s = pl.pallas_call(
        functools.partial(_rwkv_step_kernel, n_heads=n_heads),
        grid=(b,),
        in_specs=[pl.BlockSpec((1, 1, nb), lambda i: (i, 0, 0)),
                  pl.BlockSpec((1, 1, nb), lambda i: (i, 0, 0)),
                  pl.BlockSpec((1, n_heads, N_B, N_B), lambda i: (i, 0, 0, 0))]
                 + [pl.BlockSpec(x.shape, lambda i: (0, 0)) for x in params],
        out_specs=[pl.BlockSpec((1, 1, d_b), lambda i: (i, 0, 0)),
                   pl.BlockSpec((1, n_heads, N_B, N_B), lambda i: (i, 0, 0, 0))],
        out_shape=[jax.ShapeDtypeStruct((b, 1, d_b), F32),
                   jax.ShapeDtypeStruct((b, n_heads, N_B, N_B), F32)],
        compiler_params=_cparams("arbitrary"),
        name="rwkv_step",
    )(pb.reshape(b, 1, nb), shift0.reshape(b, 1, nb), s0, *params)
    return o.reshape(b, d_b), s


def _gdn_gates(gates, alog_row, dtb_row):
    beta = _sigmoid(gates)
    g = -jnp.exp(alog_row) * _softplus(gates + dtb_row)
    return beta, g


def _gdn_chunk_kernel(qkv_ref, z_ref, gates_ref, conv0_ref, s0_ref, cw_ref, alog_ref, dtb_ref, onorm_ref,
                      o_ref, sfin_ref, carry_scr, s_scr, *, n_heads):
    c = pl.program_id(1)
    n_seq, t, _ = qkv_ref.shape
    dqk = n_heads * DK_C

    @pl.when(c == 0)
    def _():
        carry_scr[:, 0:SUBLANES, :] = jnp.zeros((n_seq, SUBLANES, carry_scr.shape[2]), F32)
        carry_scr[:, SUBLANES - (DN_CONV_W - 1):SUBLANES, :] = conv0_ref[...]
        s_scr[...] = s0_ref[...]

    ys = []
    for i in range(n_seq):
        x = qkv_ref[i]
        carry_scr[i, SUBLANES:SUBLANES + t, :] = x
        yi = cw_ref[DN_CONV_W - 1:DN_CONV_W, :] * x
        for d in range(1, DN_CONV_W):
            yi = yi + (cw_ref[DN_CONV_W - 1 - d:DN_CONV_W - d, :]
                       * carry_scr[i, SUBLANES - d:SUBLANES - d + t, :])
        carry_scr[i, 0:SUBLANES, :] = x[t - SUBLANES:t]
        ys.append(yi)
    y = _silu(jnp.concatenate(ys, axis=0))

    row = _iota2((t, t), 0)
    col = _iota2((t, t), 1)
    incl = row >= col
    strict = row > col
    rows = n_seq * t
    srow = _iota2((rows, rows), 0)
    scol = _iota2((rows, rows), 1)
    lg_t = t.bit_length() - 1
    same_seq = (srow >> lg_t) == (scol >> lg_t)
    gates = jnp.concatenate([gates_ref[i] for i in range(n_seq)], axis=0)
    beta_all, g_all = _gdn_gates(gates, alog_ref[...], dtb_ref[...])
    gcum_all = _dot_mask_lhs((same_seq & (srow >= scol)).astype(BF16), g_all)
    g3 = _split3(gcum_all)
    lane = _iota2((t, LANES), 1)
    z = jnp.concatenate([z_ref[i] for i in range(n_seq)], axis=0)

    items = [(i, h) for i in range(n_seq) for h in range(n_heads)]
    n = range(len(items))
    rs = lambda i: slice(i * t, (i + 1) * t)
    q = [_l2(y[rs(i), h * DK_C:(h + 1) * DK_C]) * (DK_C ** -0.5) for i, h in items]
    k = [_l2(y[rs(i), dqk + h * DK_C:dqk + (h + 1) * DK_C]) for i, h in items]
    v = [y[rs(i), 2 * dqk + h * DV_C:2 * dqk + (h + 1) * DV_C] for i, h in items]
    beta = [beta_all[rs(i), h:h + 1] for i, h in items]
    gc = [gcum_all[rs(i), n_heads + h:n_heads + h + 1] for i, h in items]
    egc = [jnp.exp(x) for x in gc]
    nt = lambda a, b: lax.dot_general(a, b, (((1,), (1,)), ((), ())), preferred_element_type=F32)
    gc_row = []
    for i, h in items:
        pk = (lane == n_heads + h).astype(BF16)
        gc_row.append(nt(pk, g3[0][rs(i)]) + (nt(pk, g3[1][rs(i)]) + nt(pk, g3[2][rs(i)])))
    lmask = [jnp.where(incl, jnp.exp(jnp.where(incl, gc[j] - gc_row[j], 0.0)), 0.0) for j in n]
    kb = [k[j] * beta[j] for j in n]
    kq = [_dot_nt(jnp.concatenate([kb[j], q[j]], axis=0), k[j]) for j in n]
    tinv = _tri_inv_multi([jnp.where(strict, kq[j][:t] * lmask[j], 0.0) for j in n])
    uw = [_dot(tinv[j], jnp.concatenate([v[j] * beta[j], kb[j] * egc[j]], axis=1)) for j in n]
    s = [s_scr[i, h] for i, h in items]
    ws = [_dot(jnp.concatenate([uw[j][:, DV_C:], q[j] * egc[j]], axis=0), s[j]) for j in n]
    v_new = [uw[j][:, :DV_C] - ws[j][:t] for j in n]
    o = [ws[j][t:] + _dot(jnp.where(incl, kq[j][t:] * lmask[j], 0.0), v_new[j]) for j in n]
    for j, (i, h) in enumerate(items):
        g_last = gc[j][t - 1:t, :]
        s_scr[i, h] = s[j] * jnp.exp(g_last) + _dot_tn(k[j] * jnp.exp(g_last - gc[j]), v_new[j])
    for j, (i, h) in enumerate(items):
        o_ref[i, :, h * DV_C:(h + 1) * DV_C] = (_rms(o[j], onorm_ref[...])
                                                * _silu(z[rs(i), h * DV_C:(h + 1) * DV_C]))

    @pl.when(c == pl.num_programs(1) - 1)
    def _():
        sfin_ref[...] = s_scr[...]


def _gdn_gate_params(a_log, dt_bias, n_heads):
    pad = lambda x: jnp.zeros((1, LANES), F32).at[0, n_heads:2 * n_heads].set(x)
    return pad(a_log), pad(dt_bias)


def gdn_chunked(qkvz, gates, conv0, s0, cw, a_log, dt_bias, onorm, batch, seq):
    m = qkvz.shape[0]
    n_heads = s0.shape[1]
    dconv = cw.shape[1]
    dvv = n_heads * DV_C
    assert qkvz.shape[1] == dconv + dvv and dconv % dvv == 0
    t = CHUNK
    ns = SEQS_PER_STEP if batch % SEQS_PER_STEP == 0 else 1
    assert seq % t == 0 and t & (t - 1) == 0
    nc = seq // t
    alog_row, dtb_row = _gdn_gate_params(a_log, dt_bias, n_heads)
    const = lambda b, c: (0, 0)
    qkvz3 = qkvz.reshape(batch, seq, dconv + dvv)
    o, s_fin = pl.pallas_call(
        functools.partial(_gdn_chunk_kernel, n_heads=n_heads),
        grid=(batch // ns, nc),
        in_specs=[pl.BlockSpec((ns, t, dconv), lambda b, c: (b, c, 0)),
                  pl.BlockSpec((ns, t, dvv), lambda b, c: (b, c, dconv // dvv)),
                  pl.BlockSpec((ns, t, LANES), lambda b, c: (b, c, 0)),
                  pl.BlockSpec((ns, DN_CONV_W - 1, dconv), lambda b, c: (b, 0, 0)),
                  pl.BlockSpec((ns, n_heads, DK_C, DV_C), lambda b, c: (b, 0, 0, 0)),
                  pl.BlockSpec((DN_CONV_W, dconv), const),
                  pl.BlockSpec((1, LANES), const),
                  pl.BlockSpec((1, LANES), const),
                  pl.BlockSpec((1, DV_C), const)],
        out_specs=[pl.BlockSpec((ns, t, dvv), lambda b, c: (b, c, 0)),
                   pl.BlockSpec((ns, n_heads, DK_C, DV_C), lambda b, c: (b, 0, 0, 0))],
        out_shape=[jax.ShapeDtypeStruct((batch, seq, dvv), F32),
                   jax.ShapeDtypeStruct((batch, n_heads, DK_C, DV_C), F32)],
        scratch_shapes=[pltpu.VMEM((ns, SUBLANES + t, dconv), F32),
                        pltpu.VMEM((ns, n_heads, DK_C, DV_C), F32)],
        compiler_params=_cparams("arbitrary", "arbitrary"),
        name="gdn_chunked",
    )(qkvz3, qkvz3, gates.reshape(batch, seq, LANES), conv0, s0, cw, alog_row, dtb_row,
      onorm.reshape(1, DV_C))
    return o.reshape(m, dvv), s_fin


def _gdn_step_kernel(qkv_ref, z_ref, gates_ref, conv0_ref, s0_ref, cw_ref, alog_ref, dtb_ref, onorm_ref,
                     o_ref, s_ref, *, n_heads):
    dqk = n_heads * DK_C
    x = qkv_ref[0]
    y = cw_ref[DN_CONV_W - 1:DN_CONV_W, :] * x
    for i in range(DN_CONV_W - 1):
        y = y + cw_ref[i:i + 1, :] * conv0_ref[0, i:i + 1, :]
    y = _silu(y)
    beta_all, g_all = _gdn_gates(gates_ref[0], alog_ref[...], dtb_ref[...])
    z = z_ref[0]
    eye = (_iota2((DK_C, DK_C), 0) == _iota2((DK_C, DK_C), 1)).astype(F32)
    for h in range(n_heads):
        q = _l2(y[:, h * DK_C:(h + 1) * DK_C]) * (DK_C ** -0.5)
        k = _l2(y[:, dqk + h * DK_C:dqk + (h + 1) * DK_C])
        v = y[:, 2 * dqk + h * DV_C:2 * dqk + (h + 1) * DV_C]
        beta = beta_all[:, h:h + 1]
        decay = jnp.exp(g_all[:, n_heads + h:n_heads + h + 1])
        s = s0_ref[0, h]
        k_col = _col_from_row(k, eye)
        q_col = _col_from_row(q, eye)
        v_new = beta * v - jnp.sum((k_col * (beta * decay)) * s, axis=0, keepdims=True)
        o = (jnp.sum((q_col * decay) * s, axis=0, keepdims=True)
             + jnp.sum(q * k, axis=-1, keepdims=True) * v_new)
        s_ref[0, h] = s * decay + k_col * v_new
        o_ref[0, :, h * DV_C:(h + 1) * DV_C] = (_rms(o, onorm_ref[...])
                                                * _silu(z[:, h * DV_C:(h + 1) * DV_C]))


def gdn_step(qkvz, gates, conv0, s0, cw, a_log, dt_bias, onorm):
    b = qkvz.shape[0]
    n_heads = s0.shape[1]
    dconv = cw.shape[1]
    dvv = n_heads * DV_C
    alog_row, dtb_row = _gdn_gate_params(a_log, dt_bias, n_heads)
    qkvz3 = qkvz.reshape(b, 1, dconv + dvv)
    const = lambda i: (0, 0)
    o, s = pl.pallas_call(
        functools.partial(_gdn_step_kernel, n_heads=n_heads),
        grid=(b,),
        in_specs=[pl.BlockSpec((1, 1, dconv), lambda i: (i, 0, 0)),
                  pl.BlockSpec((1, 1, dvv), lambda i: (i, 0, dconv // dvv)),
                  pl.BlockSpec((1, 1, LANES), lambda i: (i, 0, 0)),
                  pl.BlockSpec((1, DN_CONV_W - 1, dconv), lambda i: (i, 0, 0)),
                  pl.BlockSpec((1, n_heads, DK_C, DV_C), lambda i: (i, 0, 0, 0)),
                  pl.BlockSpec((DN_CONV_W, dconv), const),
                  pl.BlockSpec((1, LANES), const),
                  pl.BlockSpec((1, LANES), const),
                  pl.BlockSpec((1, DV_C), const)],
        out_specs=[pl.BlockSpec((1, 1, dvv), lambda i: (i, 0, 0)),
                   pl.BlockSpec((1, n_heads, DK_C, DV_C), lambda i: (i, 0, 0, 0))],
        out_shape=[jax.ShapeDtypeStruct((b, 1, dvv), F32),
                   jax.ShapeDtypeStruct((b, n_heads, DK_C, DV_C), F32)],
        compiler_params=_cparams("arbitrary"),
        name="gdn_step",
    )(qkvz3, qkvz3, gates.reshape(b, 1, LANES), conv0, s0, cw, alog_row, dtb_row,
      onorm.reshape(1, DV_C))
    return o.reshape(b, dvv), s


def _row_tile(m, want):
    return want if m % want == 0 else m


def _trunk(x, kv, wkv0, shift0, dn0, dnconv0, ffnconv0, p):
    b, t, d = x.shape
    m = b * t
    depth = p['norm_ffn'].shape[0]
    d_a = p['sb_bias'].shape[1] * DH_A
    nb = p['mu_b'].shape[1]
    dconv = p['conv_dn'].shape[2]
    n_heads_c = p['a_log_dn'].shape[1]
    dvv = n_heads_c * DV_C
    tm = _row_tile(m, PROJ_ROWS)
    tm_out = _row_tile(m, OUT_ROWS)
    new_k, new_v, new_wkv, new_shift, new_dn, new_dnconv, new_ffn = [], [], [], [], [], [], []
    x2 = x.reshape(m, d)
    for l in range(depth):
        i = l // 2
        if l % 2 == 0:
            qkv, pb = norm_matmul(x2, p['norm_mix_even'][i], p['w_in_even'][i], (3 * d_a, nb), tm)
            rp = dict(mu=p['mu_b'][i], w0=p['w0_b'][i], w_up=p['w_up_b'][i], a0=p['a0_b'][i],
                      a_up=p['a_up_b'][i], g_up=p['g_up_b'][i], k_k=p['kk_b'][i], k_a=p['ka_b'][i],
                      r_k=p['rk_b'][i], ln_w=p['lnx_w_b'][i], ln_b=p['lnx_b_b'][i])
            if kv is None:
                o_a, kn = sb_prompt(qkv, p['qnorm_a'][i], p['knorm_a'][i], p['sb_bias'][i], b, t)
                o_b, wkv = rwkv_chunked(pb, shift0[i], wkv0[i], rp, b, t)
                shift = pb.reshape(b, t, nb)[:, t - 1]
            else:
                o_a, kn = sb_decode(qkv, kv[0], kv[1], kv[2], i, p['qnorm_a'][i], p['knorm_a'][i],
                                    p['sb_bias'][i])
                o_b, wkv = rwkv_step(pb, shift0[i], wkv0[i], rp)
                shift = pb
            w_out = p['w_out_even'][i]
            x2 = matmul_res([o_a, o_b], [w_out[:d_a], w_out[d_a:]], x2, tm_out)
            new_k.append(kn.reshape(b, t, d_a // DH_A, DH_A))
            new_v.append(qkv[:, 2 * d_a:].reshape(b, t, d_a // DH_A, DH_A))
            new_wkv.append(wkv)
            new_shift.append(shift)
        else:
            qkvz, gates = norm_matmul(x2, p['norm_mix_odd'][i], p['w_in_odd'][i], (dconv + dvv, LANES), tm)
            if kv is None:
                o_c, s_fin = gdn_chunked(qkvz, gates, dnconv0[i], dn0[i], p['conv_dn'][i], p['a_log_dn'][i],
                                         p['dt_bias_dn'][i], p['onorm_dn'][i], b, t)
                cbuf = qkvz.reshape(b, t, dconv + dvv)[:, t - (DN_CONV_W - 1):, :dconv]
            else:
                o_c, s_fin = gdn_step(qkvz, gates, dnconv0[i], dn0[i], p['conv_dn'][i], p['a_log_dn'][i],
                                      p['dt_bias_dn'][i], p['onorm_dn'][i])
                cbuf = jnp.concatenate([dnconv0[i][:, 1:], qkvz[:, None, :dconv]], axis=1)
            x2 = matmul_res([o_c], [p['w_out_odd'][i]], x2, tm_out)
            new_dn.append(s_fin)
            new_dnconv.append(cbuf)
        ffn_args = (p['norm_ffn'][l], p['w_gate'][l], p['w_up'][l], p['conv_ffn'][l], p['conv_ffn_b'][l],
                    p['w_down'][l])
        if kv is None:
            x3, fbuf = conv_ffn(x2.reshape(b, t, d), ffnconv0[l], *ffn_args, tm=_row_tile(t, FFN_ROWS))
            x2 = x3.reshape(m, d)
        else:
            x2, fbuf = conv_ffn_step(x2, ffnconv0[l], *ffn_args)
        new_ffn.append(fbuf)
    states = tuple(jnp.stack(s) for s in (new_k, new_v, new_wkv, new_shift, new_dn, new_dnconv, new_ffn))
    return x2.reshape(b, t, d), states


def kernel(x_prompt, x_sample, cache_k, cache_v, page_table, state_wkv, state_shift, state_dn,
           state_dn_conv, state_ffn_conv, norm_mix_even, w_in_even, qnorm_a, knorm_a, sb_bias,
           mu_b, w0_b, w_up_b, a0_b, a_up_b, g_up_b, kk_b, ka_b, rk_b, lnx_w_b, lnx_b_b,
           w_out_even, norm_mix_odd, w_in_odd, conv_dn, a_log_dn, dt_bias_dn, onorm_dn, w_out_odd,
           norm_ffn, w_gate, w_up, conv_ffn, conv_ffn_b, w_down):
    n_heads_c = a_log_dn.shape[1]
    n_gate_cols = 2 * n_heads_c
    assert x_sample.shape[1] == 1 and n_gate_cols <= LANES
    w_in_odd_p = jnp.pad(w_in_odd, ((0, 0), (0, 0), (0, LANES - n_gate_cols)))
    bf = lambda w: w.astype(BF16)
    p = dict(norm_mix_even=norm_mix_even, w_in_even=bf(w_in_even), qnorm_a=qnorm_a, knorm_a=knorm_a,
             sb_bias=sb_bias, mu_b=mu_b, w0_b=w0_b, w_up_b=bf(w_up_b), a0_b=a0_b, a_up_b=bf(a_up_b),
             g_up_b=bf(g_up_b), kk_b=kk_b, ka_b=ka_b, rk_b=rk_b, lnx_w_b=lnx_w_b, lnx_b_b=lnx_b_b,
             w_out_even=bf(w_out_even), norm_mix_odd=norm_mix_odd, w_in_odd=bf(w_in_odd_p),
             conv_dn=conv_dn, a_log_dn=a_log_dn, dt_bias_dn=dt_bias_dn, onorm_dn=onorm_dn,
             w_out_odd=bf(w_out_odd), norm_ffn=norm_ffn, w_gate=bf(w_gate), w_up=bf(w_up),
             conv_ffn=conv_ffn, conv_ffn_b=conv_ffn_b, w_down=bf(w_down))
    b = x_prompt.shape[0]
    n_ab = state_wkv.shape[0]
    n_c = state_dn.shape[0]
    depth = norm_ffn.shape[0]
    zeros = lambda like, lead: jnp.zeros((lead, b) + like.shape[2:], F32)
    y_prompt, ps = _trunk(x_prompt, None, zeros(state_wkv, n_ab), zeros(state_shift, n_ab),
                          zeros(state_dn, n_c), zeros(state_dn_conv, n_c), zeros(state_ffn_conv, depth), p)
    y_sample, ss = _trunk(x_sample, (cache_k, cache_v, page_table), state_wkv, state_shift, state_dn,
                          state_dn_conv, state_ffn_conv, p)
    return (y_prompt, y_sample, ps[0], ps[1], ss[0], ss[1], ps[2], ss[2], ps[3], ss[3],
            ps[4], ss[4], ps[5], ss[5], ps[6], ss[6])
```

```python
import functools

import jax
import jax.numpy as jnp
from jax import lax
from jax.experimental import pallas as pl
from jax.experimental.pallas import tpu as pltpu

F32 = jnp.float32
BF16 = jnp.bfloat16

RMS_EPS = 1e-6
L2_EPS = 1e-6
GN_EPS = 64e-5

DH_A = 64
N_B = 64
R_W, R_A, R_G = 64, 64, 128
DK_C = 128
DV_C = 128
DN_CONV_W = 4
FFN_CONV_W = 3

SB_BLOCK = 128
CHUNK = 64
SEQS_PER_STEP = 2
SUBLANES = 8
LANES = 128
VMEM_LIMIT_BYTES = 56 * 1024 * 1024
PROJ_ROWS = 512
OUT_ROWS = 1024
FFN_ROWS = 512
FFN_COLS = 256


def _cparams(*sem):
    return pltpu.CompilerParams(dimension_semantics=sem, vmem_limit_bytes=VMEM_LIMIT_BYTES)


def _dot(a, b):
    return jnp.dot(a.astype(BF16), b.astype(BF16), preferred_element_type=F32)


def _dot_nt(a, b):
    return lax.dot_general(a.astype(BF16), b.astype(BF16), (((1,), (1,)), ((), ())),
                           preferred_element_type=F32)


def _dot_tn(a, b):
    return lax.dot_general(a.astype(BF16), b.astype(BF16), (((0,), (0,)), ((), ())),
                           preferred_element_type=F32)


def _split2(a):
    hi = a.astype(BF16)
    lo = (a - hi.astype(F32)).astype(BF16)
    return hi, lo


def _split3(a):
    hi = a.astype(BF16)
    r1 = a - hi.astype(F32)
    mid = r1.astype(BF16)
    lo = (r1 - mid.astype(F32)).astype(BF16)
    return hi, mid, lo


def _dot_mask_lhs(mask_bf16, x):
    d = functools.partial(jnp.dot, preferred_element_type=F32)
    h, m, l = _split3(x)
    return d(mask_bf16, h) + (d(mask_bf16, m) + d(mask_bf16, l))


def _softplus(x):
    return jnp.maximum(x, 0.0) + jnp.log(1.0 + jnp.exp(-jnp.abs(x)))


def _sigmoid(x):
    return 1.0 / (1.0 + jnp.exp(-x))


def _silu(x):
    return x * _sigmoid(x)


def _rms(x, g):
    return x * lax.rsqrt(jnp.mean(x * x, axis=-1, keepdims=True) + RMS_EPS) * g


def _l2(x):
    return x * lax.rsqrt(jnp.sum(x * x, axis=-1, keepdims=True) + L2_EPS)


def _iota2(shape, dim):
    return lax.broadcasted_iota(jnp.int32, shape, dim)


def _shift_rows(cur, carry, d):
    rolled = pltpu.roll(cur, d, 0)
    crolled = pltpu.roll(carry, d, 0)
    first = jnp.where(_iota2(carry.shape, 0) < d, crolled, rolled[:SUBLANES])
    if cur.shape[0] == SUBLANES:
        return first
    return jnp.concatenate([first, rolled[SUBLANES:]], axis=0)


def _tri_inv_multi(a_list):
    n = a_list[0].shape[0]
    row = _iota2((n, n), 0)
    col = _iota2((n, n), 1)
    eye = (row == col).astype(F32)
    d = functools.partial(jnp.dot, preferred_element_type=F32)
    ts = [eye - jnp.where((row >> 1) == (col >> 1), a, 0.0) for a in a_list]
    s = 2
    while s < n:
        sh = s.bit_length() - 1
        sub = ((row >> (sh + 1)) == (col >> (sh + 1))) & ((row >> sh) != (col >> sh))
        tb = [t.astype(BF16) for t in ts]
        tes = [d(t, jnp.where(sub, a, 0.0).astype(BF16)).astype(BF16) for t, a in zip(tb, a_list)]
        ts = [t - d(te, t16) for t, te, t16 in zip(ts, tes, tb)]
        s *= 2
    return ts


def _norm_matmul_kernel(x_ref, g_ref, w_ref, *o_refs, splits):
    h = _rms(x_ref[...], g_ref[...]).astype(BF16)
    y = jnp.dot(h, w_ref[...], preferred_element_type=F32)
    off = 0
    for o_ref, n in zip(o_refs, splits):
        o_ref[...] = y[:, off:off + n]
        off += n


def norm_matmul(x, g, w, splits, tm):
    m, k = x.shape
    n = w.shape[1]
    assert sum(splits) == n and m % tm == 0
    return pl.pallas_call(
        functools.partial(_norm_matmul_kernel, splits=tuple(splits)),
        grid=(m // tm,),
        in_specs=[pl.BlockSpec((tm, k), lambda i: (i, 0)),
                  pl.BlockSpec((1, k), lambda i: (0, 0)),
                  pl.BlockSpec((k, n), lambda i: (0, 0))],
        out_specs=[pl.BlockSpec((tm, s), lambda i: (i, 0)) for s in splits],
        out_shape=[jax.ShapeDtypeStruct((m, s), F32) for s in splits],
        compiler_params=_cparams("arbitrary"),
        name="norm_matmul",
    )(x, g.reshape(1, k), w)


def _matmul_res_kernel(*refs, n_in):
    a_refs = refs[:n_in]
    w_refs = refs[n_in:2 * n_in]
    res_ref = refs[2 * n_in]
    o_ref = refs[2 * n_in + 1]
    acc = res_ref[...]
    for a_ref, w_ref in zip(a_refs, w_refs):
        acc = acc + jnp.dot(a_ref[...].astype(BF16), w_ref[...], preferred_element_type=F32)
    o_ref[...] = acc


def matmul_res(a_list, w_list, res, tm):
    m, n = res.shape
    assert m % tm == 0
    n_in = len(a_list)
    in_specs = ([pl.BlockSpec((tm, a.shape[1]), lambda i: (i, 0)) for a in a_list]
                + [pl.BlockSpec(w.shape, lambda i: (0, 0)) for w in w_list]
                + [pl.BlockSpec((tm, n), lambda i: (i, 0))])
    return pl.pallas_call(
        functools.partial(_matmul_res_kernel, n_in=n_in),
        grid=(m // tm,),
        in_specs=in_specs,
        out_specs=pl.BlockSpec((tm, n), lambda i: (i, 0)),
        out_shape=jax.ShapeDtypeStruct((m, n), F32),
        compiler_params=_cparams("arbitrary"),
        name="matmul_res",
    )(*a_list, *w_list, res)


def _ffn_kernel(*refs, n_mix):
    mix_refs = refs[:n_mix]
    wmix_refs = refs[n_mix:2 * n_mix]
    (x_ref, buf_ref, g_ref, wg_ref, wu_ref, cw_ref, cb_ref, wd_ref,
     y_ref, tail_ref, carry_scr, act_scr) = refs[2 * n_mix:]
    ti = pl.program_id(1)
    x = x_ref[0]
    for o_ref, w_ref in zip(mix_refs, wmix_refs):
        x = x + jnp.dot(o_ref[0].astype(BF16), w_ref[...], preferred_element_type=F32)
    tm = x.shape[0]
    f = wg_ref.shape[1]
    h = _rms(x, g_ref[...]).astype(BF16)

    @pl.when(ti == 0)
    def _():
        carry_scr[...] = jnp.zeros_like(carry_scr)
        carry_scr[SUBLANES - 2:SUBLANES, :] = buf_ref[0]

    for c0 in range(0, f, FFN_COLS):
        cols = slice(c0, min(c0 + FFN_COLS, f))
        gate = jnp.dot(h, wg_ref[:, cols], preferred_element_type=F32)
        up = jnp.dot(h, wu_ref[:, cols], preferred_element_type=F32)
        carry = carry_scr[:, cols]
        conv = (cw_ref[2:3, cols] * gate + cw_ref[1:2, cols] * _shift_rows(gate, carry, 1)
                + cw_ref[0:1, cols] * _shift_rows(gate, carry, 2))
        carry_scr[:, cols] = gate[tm - SUBLANES:tm]
        tail_ref[0, :, cols] = gate[tm - 2:tm]
        act_scr[:, cols] = (_silu(conv + cb_ref[:, cols]) * up).astype(BF16)
    y_ref[0] = x + jnp.dot(act_scr[...], wd_ref[...], preferred_element_type=F32)


def conv_ffn(mix, w_mix, x, buf, g, wg, wu, cw, cb, wd, tm):
    b, t, d = x.shape
    f = wg.shape[1]
    assert t % tm == 0 and tm % SUBLANES == 0
    const = lambda bi, ti: (0, 0)
    return pl.pallas_call(
        functools.partial(_ffn_kernel, n_mix=len(mix)),
        grid=(b, t // tm),
        in_specs=[pl.BlockSpec((1, tm, o.shape[2]), lambda bi, ti: (bi, ti, 0)) for o in mix]
                 + [pl.BlockSpec(w.shape, const) for w in w_mix]
                 + [pl.BlockSpec((1, tm, d), lambda bi, ti: (bi, ti, 0)),
                  pl.BlockSpec((1, 2, f), lambda bi, ti: (bi, 0, 0)),
                  pl.BlockSpec((1, d), const),
                  pl.BlockSpec((d, f), const),
                  pl.BlockSpec((d, f), const),
                  pl.BlockSpec((FFN_CONV_W, f), const),
                  pl.BlockSpec((1, f), const),
                  pl.BlockSpec((f, d), const)],
        out_specs=[pl.BlockSpec((1, tm, d), lambda bi, ti: (bi, ti, 0)),
                   pl.BlockSpec((1, 2, f), lambda bi, ti: (bi, 0, 0))],
        out_shape=[jax.ShapeDtypeStruct((b, t, d), F32),
                   jax.ShapeDtypeStruct((b, 2, f), F32)],
        scratch_shapes=[pltpu.VMEM((SUBLANES, f), F32), pltpu.VMEM((tm, f), BF16)],
        compiler_params=_cparams("arbitrary", "arbitrary"),
        name="conv_ffn",
    )(*mix, *w_mix, x, buf, g.reshape(1, d), wg, wu, cw, cb.reshape(1, f), wd)


def _ffn_step_kernel(x_ref, buf0_ref, buf1_ref, g_ref, wg_ref, wu_ref, cw_ref, cb_ref, wd_ref,
                     y_ref, gate_ref):
    x = x_ref[...]
    h = _rms(x, g_ref[...]).astype(BF16)
    gate = jnp.dot(h, wg_ref[...], preferred_element_type=F32)
    up = jnp.dot(h, wu_ref[...], preferred_element_type=F32)
    conv = cw_ref[2:3, :] * gate + cw_ref[1:2, :] * buf1_ref[...] + cw_ref[0:1, :] * buf0_ref[...]
    act = (_silu(conv + cb_ref[...]) * up).astype(BF16)
    y_ref[...] = x + jnp.dot(act, wd_ref[...], preferred_element_type=F32)
    gate_ref[...] = gate


def conv_ffn_step(x, buf, g, wg, wu, cw, cb, wd):
    b, d = x.shape
    f = wg.shape[1]
    full = lambda shape: pl.BlockSpec(shape, lambda i: (0,) * len(shape))
    y, gate = pl.pallas_call(
        _ffn_step_kernel,
        grid=(1,),
        in_specs=[full((b, d)), full((b, f)), full((b, f)), full((1, d)), full((d, f)), full((d, f)),
                  full((FFN_CONV_W, f)), full((1, f)), full((f, d))],
        out_specs=[full((b, d)), full((b, f))],
        out_shape=[jax.ShapeDtypeStruct((b, d), F32), jax.ShapeDtypeStruct((b, f), F32)],
        compiler_params=_cparams("arbitrary"),
        name="conv_ffn_step",
    )(x, buf[:, 0], buf[:, 1], g.reshape(1, d), wg, wu, cw, cb.reshape(1, f), wd)
    return y, jnp.stack([buf[:, 1], gate], axis=1)


def _pair_rms(x, w_row):
    left = _iota2(x.shape, 1) < DH_A
    x2 = x * x
    ms_l = jnp.sum(jnp.where(left, x2, 0.0), axis=-1, keepdims=True) * (1.0 / DH_A)
    ms_r = jnp.sum(jnp.where(left, 0.0, x2), axis=-1, keepdims=True) * (1.0 / DH_A)
    inv = jnp.where(left, lax.rsqrt(ms_l + RMS_EPS), lax.rsqrt(ms_r + RMS_EPS))
    return x * inv * w_row


def _sb_prompt_kernel(bias_ref, q_ref, k_ref, v_ref, qw_ref, kw_ref, o_ref, knt_ref, vt_ref,
                      kn_scr, v_scr, q_scr, later_scr, o_scr, tail_scr, *, n_heads):
    qi = pl.program_id(1)
    blk = q_ref.shape[0]
    n_pairs = n_heads // 2
    pw = 2 * DH_A
    scale = DH_A ** -0.5
    row0 = pl.multiple_of(qi * blk, blk)

    k = k_ref[...]
    v = v_ref[...]
    kn = jnp.concatenate([_pair_rms(k[:, p * pw:(p + 1) * pw], kw_ref[...]) for p in range(n_pairs)],
                         axis=-1)
    for h in range(n_heads):
        knt_ref[0, h] = kn[:, h * DH_A:(h + 1) * DH_A].T
        vt_ref[0, h] = v[:, h * DH_A:(h + 1) * DH_A].T
    kn_scr[pl.ds(row0, blk), :] = kn.astype(BF16)
    v_scr[pl.ds(row0, blk), :] = v.astype(BF16)

    q = q_ref[...]
    left = _iota2((blk, pw), 1) < DH_A
    for p in range(n_pairs):
        qn = _pair_rms(q[:, p * pw:(p + 1) * pw], qw_ref[...]) * scale
        q_scr[p] = jnp.concatenate([jnp.where(left, qn, 0.0), jnp.where(left, 0.0, qn)],
                                   axis=0).astype(BF16)
    krow = _iota2((2 * blk, 2 * blk), 0)
    krow = jnp.where(krow >= blk, krow - blk, krow)
    kcol = _iota2((2 * blk, 2 * blk), 1)
    later_scr[...] = jnp.where((kcol >= blk) | (krow > kcol), 1.0, 0.0).astype(BF16)
    o_scr[...] = jnp.zeros_like(o_scr)
    tail_scr[...] = jnp.zeros_like(tail_scr)

    qrow = _iota2((2 * blk, blk), 0)
    causal = _iota2((2 * blk, blk), 1) < jnp.where(qrow >= blk, qrow - blk, qrow)
    vleft = _iota2((blk, pw), 1) < DH_A

    def blocks(offs, diag):
        items = [(j, p) for j in range(len(offs)) for p in range(n_pairs)]
        zs = []
        for j, p in items:
            z2 = lax.dot_general(q_scr[p], kn_scr[pl.ds(offs[j], blk), p * pw:(p + 1) * pw],
                                 (((1,), (1,)), ((), ())), preferred_element_type=F32)
            zs.append(jnp.concatenate([z2[:blk] + bias_ref[2 * p], z2[blk:] + bias_ref[2 * p + 1]],
                                      axis=0))
        sps = [_softplus(z) for z in zs]
        if diag:
            sps = [jnp.where(causal, sp, 0.0) for sp in sps]
        his = [sp.astype(BF16) for sp in sps]
        los = [(sp - hi.astype(F32)).astype(BF16) for sp, hi in zip(sps, his)]
        lo_mat = later_scr[...]
        cums = [jnp.dot(jnp.concatenate([hi, lo], axis=1), lo_mat, preferred_element_type=F32)
                for hi, lo in zip(his, los)]
        ws = []
        for p in range(n_pairs):
            tail = tail_scr[p]
            for j in range(len(offs)):
                n = j * n_pairs + p
                w = jnp.exp((zs[n] - sps[n]) - (cums[n][:, :blk] + tail))
                if diag:
                    w = jnp.where(causal, w, 0.0)
                tail = tail + cums[n][:, blk:]
                ws.append(w.astype(BF16))
            tail_scr[p] = tail
        for p in range(n_pairs):
            acc = o_scr[:, p * pw:(p + 1) * pw]
            for j in range(len(offs)):
                w = ws[p * len(offs) + j]
                vj = v_scr[pl.ds(offs[j], blk), p * pw:(p + 1) * pw]
                zero = jnp.zeros_like(vj)
                v_bd = jnp.concatenate([jnp.where(vleft, vj, zero), jnp.where(vleft, zero, vj)], axis=0)
                acc = acc + jnp.dot(jnp.concatenate([w[:blk], w[blk:]], axis=1), v_bd,
                                    preferred_element_type=F32)
            o_scr[:, p * pw:(p + 1) * pw] = acc

    blocks([row0], True)

    def body(i, carry):
        hi_off = pl.multiple_of((qi - 1 - 2 * i) * blk, blk)
        blocks([hi_off, pl.multiple_of(hi_off - blk, blk)], False)
        return carry

    lax.fori_loop(0, qi // 2, body, 0)

    @pl.when(qi % 2 == 1)
    def _():
        blocks([0], False)

    o_ref[...] = o_scr[...]


def sb_prompt(qkv, qw, kw, bias, batch, seq):
    m, n3 = qkv.shape
    d_a = n3 // 3
    n_heads = d_a // DH_A
    blk = SB_BLOCK
    assert seq % blk == 0 and n_heads % 2 == 0 and 2 * DH_A == LANES
    nq = seq // blk
    rowblk = lambda col: (lambda b, qi: (b * nq + qi, col))
    pair_w = lambda w: jnp.tile(w, 2).reshape(1, 2 * DH_A)
    head_t = pl.BlockSpec((1, n_heads, DH_A, blk), lambda b, qi: (b, 0, 0, qi))
    head_t_shape = jax.ShapeDtypeStruct((batch, n_heads, DH_A, seq), F32)
    return pl.pallas_call(
        functools.partial(_sb_prompt_kernel, n_heads=n_heads),
        grid=(batch, nq),
        in_specs=[pl.BlockSpec(memory_space=pltpu.SMEM),
                  pl.BlockSpec((blk, d_a), rowblk(0)),
                  pl.BlockSpec((blk, d_a), rowblk(1)),
                  pl.BlockSpec((blk, d_a), rowblk(2)),
                  pl.BlockSpec((1, 2 * DH_A), lambda b, qi: (0, 0)),
                  pl.BlockSpec((1, 2 * DH_A), lambda b, qi: (0, 0))],
        out_specs=[pl.BlockSpec((blk, d_a), rowblk(0)), head_t, head_t],
        out_shape=[jax.ShapeDtypeStruct((m, d_a), F32), head_t_shape, head_t_shape],
        scratch_shapes=[pltpu.VMEM((seq, d_a), BF16), pltpu.VMEM((seq, d_a), BF16),
                        pltpu.VMEM((n_heads // 2, 2 * blk, 2 * DH_A), BF16),
                        pltpu.VMEM((2 * blk, 2 * blk), BF16),
                        pltpu.VMEM((blk, d_a), F32),
                        pltpu.VMEM((n_heads // 2, 2 * blk, blk), F32)],
        compiler_params=_cparams("arbitrary", "arbitrary"),
        name="sb_prompt",
    )(bias, qkv, qkv, qkv, pair_w(qw), pair_w(kw))


DECODE_BANK = 4
DECODE_SLOTS = 16


def _sb_decode_kernel(pt_ref, q_ref, k_ref, ck_hbm, cv_hbm, qw_ref, kw_ref, bias_ref,
                      o_ref, kn_ref, kbuf, vbuf, sem, *, n_heads, n_pages, layer):
    b = pl.program_id(0)
    page = kbuf.shape[3]
    scale = DH_A ** -0.5
    nt = lambda x, y: lax.dot_general(x, y, (((1,), (1,)), ((), ())), preferred_element_type=F32)

    def copies(i, slot):
        phys = pt_ref[b, n_pages - 1 - i]
        return (pltpu.make_async_copy(ck_hbm.at[layer, phys], kbuf.at[slot], sem.at[0, slot]),
                pltpu.make_async_copy(cv_hbm.at[layer, phys], vbuf.at[slot], sem.at[1, slot]))

    for j in range(DECODE_SLOTS):
        for cp in copies(j, j):
            cp.start()

    q = q_ref[0]
    k = k_ref[0]
    hrow = _iota2((n_heads, DH_A), 0)
    qe, kn = [], []
    for h in range(n_heads):
        lanes = slice(h * DH_A, (h + 1) * DH_A)
        qn = _rms(q[:, lanes], qw_ref[...]) * scale
        qe.append(jnp.where(hrow == h, jnp.broadcast_to(qn, (n_heads, DH_A)), 0.0).astype(BF16))
        kn.append(_rms(k[:, lanes], kw_ref[...]))
    kn_ref[0] = jnp.concatenate(kn, axis=-1)

    krow = _iota2((page, 2 * page), 0)
    kcol = _iota2((page, 2 * page), 1)
    later_ones = jnp.where((kcol >= page) | (krow > kcol), 1.0, 0.0).astype(BF16)
    bias = bias_ref[...]

    def bank_pages(first, bank, tail, acc):
        pages = range(DECODE_BANK)
        slots = [bank * DECODE_BANK + j for j in pages]
        for j in pages:
            for cp in copies(first + j, slots[j]):
                cp.wait()
        zs = []
        for j in pages:
            z = bias
            for h in range(n_heads):
                z = z + jnp.dot(qe[h], kbuf[slots[j], h].astype(BF16), preferred_element_type=F32)
            zs.append(z)
        sps = [_softplus(z) for z in zs]
        his = [sp.astype(BF16) for sp in sps]
        los = [(sp - hi.astype(F32)).astype(BF16) for sp, hi in zip(sps, his)]
        cums = [jnp.dot(hi, later_ones, preferred_element_type=F32)
                + jnp.dot(lo, later_ones, preferred_element_type=F32) for hi, lo in zip(his, los)]
        ws = []
        for j in pages:
            ws.append(jnp.exp((zs[j] - sps[j]) - (cums[j][:, :page] + tail)).astype(BF16))
            tail = tail + cums[j][:, page:]
        for j in pages:
            acc = [acc[h] + nt(ws[j], vbuf[slots[j], h].astype(BF16)) for h in range(n_heads)]

        @pl.when(first + DECODE_SLOTS < n_pages)
        def _():
            for j in pages:
                for cp in copies(first + DECODE_SLOTS + j, slots[j]):
                    cp.start()

        return tail, acc

    def group(gi, carry):
        tail, acc = carry
        for bank in range(DECODE_SLOTS // DECODE_BANK):
            tail, acc = bank_pages(gi * DECODE_SLOTS + bank * DECODE_BANK, bank, tail, acc)
        return tail, acc

    zero = jnp.zeros((n_heads, DH_A), F32)
    _, acc = lax.fori_loop(0, n_pages // DECODE_SLOTS, group,
                           (jnp.zeros((n_heads, page), F32), [zero] * n_heads))
    o_ref[0] = jnp.concatenate([acc[h][h:h + 1, :] for h in range(n_heads)], axis=-1)


def sb_decode(qkv, cache_k, cache_v, page_table, layer, qw, kw, bias):
    b, n3 = qkv.shape
    d_a = n3 // 3
    n_heads = d_a // DH_A
    n_layers, n_phys, page = cache_k.shape[:3]
    n_pages = page_table.shape[1]
    assert n_pages % DECODE_SLOTS == 0 and cache_k.shape[3] == n_heads
    ck_t = jnp.transpose(cache_k, (0, 1, 3, 4, 2))
    cv_t = jnp.transpose(cache_v, (0, 1, 3, 4, 2))
    qkv3 = qkv.reshape(b, 1, n3)
    grid_spec = pltpu.PrefetchScalarGridSpec(
        num_scalar_prefetch=1,
        grid=(b,),
        in_specs=[pl.BlockSpec((1, 1, d_a), lambda bi, pt: (bi, 0, 0)),
                  pl.BlockSpec((1, 1, d_a), lambda bi, pt: (bi, 0, 1)),
                  pl.BlockSpec(memory_space=pl.ANY),
                  pl.BlockSpec(memory_space=pl.ANY),
                  pl.BlockSpec((1, DH_A), lambda bi, pt: (0, 0)),
                  pl.BlockSpec((1, DH_A), lambda bi, pt: (0, 0)),
                  pl.BlockSpec((n_heads, 1), lambda bi, pt: (0, 0))],
        out_specs=[pl.BlockSpec((1, 1, d_a), lambda bi, pt: (bi, 0, 0)),
                   pl.BlockSpec((1, 1, d_a), lambda bi, pt: (bi, 0, 0))],
        scratch_shapes=[pltpu.VMEM((DECODE_SLOTS, n_heads, DH_A, page), F32),
                        pltpu.VMEM((DECODE_SLOTS, n_heads, DH_A, page), F32),
                        pltpu.SemaphoreType.DMA((2, DECODE_SLOTS))],
    )
    o, kn = pl.pallas_call(
        functools.partial(_sb_decode_kernel, n_heads=n_heads, n_pages=n_pages, layer=layer),
        grid_spec=grid_spec,
        out_shape=[jax.ShapeDtypeStruct((b, 1, d_a), F32), jax.ShapeDtypeStruct((b, 1, d_a), F32)],
        compiler_params=_cparams("arbitrary"),
        name="sb_decode",
    )(page_table, qkv3, qkv3, ck_t, cv_t, qw.reshape(1, DH_A), kw.reshape(1, DH_A),
      bias.reshape(n_heads, 1))
    return o.reshape(b, d_a), kn.reshape(b, d_a)


def _rwkv_tokens(pb, prev, mu, w0, w_up, a0, a_up, g_up, k_k, k_a, d_b):
    xs = pb + (prev - pb) * mu
    r = xs[:, 0:d_b]
    k = xs[:, d_b:2 * d_b]
    v = xs[:, 2 * d_b:3 * d_b]
    xw = xs[:, 3 * d_b:3 * d_b + R_W]
    xa = xs[:, 3 * d_b + R_W:3 * d_b + R_W + R_A]
    xg = xs[:, 3 * d_b + R_W + R_A:3 * d_b + R_W + R_A + R_G]
    w_log = -_softplus(-(w0 + _dot(jnp.tanh(xw), w_up))) - 0.5
    log_decay = -jnp.exp(w_log)
    a = _sigmoid(a0 + _dot(xa, a_up))
    g = _dot(_sigmoid(xg), g_up)
    kk_raw = k * k_k
    k_mod = k * (1.0 + (a - 1.0) * k_a)
    return r, k_mod, v, kk_raw, a, g, log_decay


def _rwkv_finish(o, r, k_mod, v, g, r_k, ln_w, ln_b):
    mean = jnp.mean(o, axis=-1, keepdims=True)
    var = jnp.mean(jnp.square(o - mean), axis=-1, keepdims=True)
    on = (o - mean) * lax.rsqrt(var + GN_EPS) * ln_w + ln_b
    bonus = jnp.sum(r * k_mod * r_k, axis=-1, keepdims=True) * v
    return (on + bonus) * g


def _rwkv_chunk_kernel(pb_ref, shift0_ref, s0_ref, mu_ref, w0_ref, wup_ref, a0_ref, aup_ref, gup_ref,
                       kk_ref, ka_ref, rk_ref, lnw_ref, lnb_ref, hsum_ref, o_ref, sfin_ref,
                       carry_scr, s_scr, o_scr, *, n_heads):
    c = pl.program_id(1)
    d_b = n_heads * N_B
    n_seq, t, nb = pb_ref.shape

    @pl.when(c == 0)
    def _():
        carry_scr[...] = jnp.zeros_like(carry_scr)
        carry_scr[:, SUBLANES - 1:SUBLANES, :] = shift0_ref[...]
        s_scr[...] = s0_ref[...]

    pbs = [pb_ref[i] for i in range(n_seq)]
    prev = jnp.concatenate([_shift_rows(pbs[i], carry_scr[i], 1) for i in range(n_seq)], axis=0)
    for i in range(n_seq):
        carry_scr[i] = pbs[i][t - SUBLANES:t]
    r, k_mod, v, kk_raw, a, g, log_decay = _rwkv_tokens(
        jnp.concatenate(pbs, axis=0), prev, mu_ref[...], w0_ref[...], wup_ref[...], a0_ref[...],
        aup_ref[...], gup_ref[...], kk_ref[...], ka_ref[...], d_b)

    row = _iota2((t, t), 0)
    col = _iota2((t, t), 1)
    incl = row >= col
    strict = row > col
    rows = n_seq * t
    srow = _iota2((rows, rows), 0)
    scol = _iota2((rows, rows), 1)
    lg_t = t.bit_length() - 1
    same_seq = (srow >> lg_t) == (scol >> lg_t)
    cum = _dot_mask_lhs((same_seq & (srow >= scol)).astype(BF16), log_decay)
    cum_prev = cum - log_decay
    cum_last = jnp.concatenate(
        [jnp.broadcast_to(cum[(i + 1) * t - 1:(i + 1) * t, :], (t, d_b)) for i in range(n_seq)], axis=0)
    e_cum = jnp.exp(cum)
    e_prev = jnp.exp(cum_prev)
    e_neg = jnp.exp(-cum)
    e_rest = jnp.exp(cum_last - cum)
    e_last = jnp.exp(cum_last)

    items = [(i, h) for i in range(n_seq) for h in range(n_heads)]
    n = range(len(items))
    sub = lambda z, i, h: z[i * t:(i + 1) * t, h * N_B:(h + 1) * N_B]
    pick = lambda z: [sub(z, i, h) for i, h in items]
    def head_sum(z):
        z_hi, z_lo = _split2(z)
        return (jnp.dot(z_hi, hsum_ref[...], preferred_element_type=F32)
                + jnp.dot(z_lo, hsum_ref[...], preferred_element_type=F32))

    kk_all = kk_raw * lax.rsqrt(head_sum(kk_raw * kk_raw) + L2_EPS)
    ka_all = kk_all * a
    v_ = pick(v)
    x = [jnp.concatenate([p, q], axis=0) for p, q in zip(pick(kk_all * e_prev), pick(r * e_cum))]
    khat = pick(k_mod * e_neg)
    bhat = pick(ka_all * e_neg)
    k_rest = pick(k_mod * e_rest)
    b_rest = pick(ka_all * e_rest)
    elast_ = pick(e_last)
    s = [s_scr[i, h] for i, h in items]
    xb = [_dot_nt(x[j], bhat[j]) for j in n]
    xk = [_dot_nt(x[j], khat[j]) for j in n]
    xs = [_dot_nt(x[j], s[j]) for j in n]
    tinv = _tri_inv_multi([jnp.where(strict, xb[j][:t], 0.0) for j in n])
    rhs = [xs[j][:t] + _dot(jnp.where(strict, xk[j][:t], 0.0), v_[j]) for j in n]
    u = [_dot(tinv[j], rhs[j]) for j in n]
    o = [xs[j][t:] + _dot(jnp.where(incl, xk[j][t:], 0.0), v_[j])
         - _dot(jnp.where(incl, xb[j][t:], 0.0), u[j]) for j in n]
    for j, (i, h) in enumerate(items):
        s_scr[i, h] = s[j] * elast_[j][:1] + _dot_tn(
            jnp.concatenate([v_[j], -u[j]], axis=0),
            jnp.concatenate([k_rest[j], b_rest[j]], axis=0))
    for j, (i, h) in enumerate(items):
        o_scr[i * t:(i + 1) * t, h * N_B:(h + 1) * N_B] = o[j]
    o_all = o_scr[...]
    inv_n = 1.0 / N_B
    cen = o_all - head_sum(o_all) * inv_n
    var = head_sum(cen * cen) * inv_n
    on = cen * lax.rsqrt(var + GN_EPS) * lnw_ref[...] + lnb_ref[...]
    out = (on + head_sum(r * k_mod * rk_ref[...]) * v) * g
    for i in range(n_seq):
        o_ref[i] = out[i * t:(i + 1) * t]

    @pl.when(c == pl.num_programs(1) - 1)
    def _():
        sfin_ref[...] = s_scr[...]


def _rwkv_params(p, d_b):
    row = lambda x: x.reshape(1, -1)
    return (row(p['mu']), row(p['w0']), p['w_up'], row(p['a0']), p['a_up'], p['g_up'],
            row(p['k_k']), row(p['k_a']), row(p['r_k']), row(p['ln_w']), row(p['ln_b']))


def rwkv_chunked(pb, shift0, s0, p, batch, seq):
    m, nb = pb.shape
    n_heads = s0.shape[1]
    d_b = n_heads * N_B
    t = CHUNK
    ns = SEQS_PER_STEP if batch % SEQS_PER_STEP == 0 else 1
    assert seq % t == 0 and t & (t - 1) == 0
    nc = seq // t
    head_sum = jnp.kron(jnp.eye(n_heads, dtype=F32), jnp.ones((N_B, N_B), F32)).astype(BF16)
    params = _rwkv_params(p, d_b) + (head_sum,)
    const = lambda b, c: (0, 0)
    o, s_fin = pl.pallas_call(
        functools.partial(_rwkv_chunk_kernel, n_heads=n_heads),
        grid=(batch // ns, nc),
        in_specs=[pl.BlockSpec((ns, t, nb), lambda b, c: (b, c, 0)),
                  pl.BlockSpec((ns, 1, nb), lambda b, c: (b, 0, 0)),
                  pl.BlockSpec((ns, n_heads, N_B, N_B), lambda b, c: (b, 0, 0, 0))]
                 + [pl.BlockSpec(x.shape, const) for x in params],
        out_specs=[pl.BlockSpec((ns, t, d_b), lambda b, c: (b, c, 0)),
                   pl.BlockSpec((ns, n_heads, N_B, N_B), lambda b, c: (b, 0, 0, 0))],
        out_shape=[jax.ShapeDtypeStruct((batch, seq, d_b), F32),
                   jax.ShapeDtypeStruct((batch, n_heads, N_B, N_B), F32)],
        scratch_shapes=[pltpu.VMEM((ns, SUBLANES, nb), F32), pltpu.VMEM((ns, n_heads, N_B, N_B), F32),
                        pltpu.VMEM((ns * t, d_b), F32)],
        compiler_params=_cparams("arbitrary", "arbitrary"),
        name="rwkv_chunked",
    )(pb.reshape(batch, seq, nb), shift0.reshape(batch, 1, nb), s0, *params)
    return o.reshape(m, d_b), s_fin


def _col_from_row(x_row, eye):
    return jnp.sum(eye * x_row, axis=-1, keepdims=True)


def _row_from_col(x_col, eye):
    return jnp.sum(eye * x_col, axis=0, keepdims=True)


def _rwkv_step_kernel(pb_ref, shift0_ref, s0_ref, mu_ref, w0_ref, wup_ref, a0_ref, aup_ref, gup_ref,
                      kk_ref, ka_ref, rk_ref, lnw_ref, lnb_ref, o_ref, s_ref, *, n_heads):
    d_b = n_heads * N_B
    r, k_mod, v, kk_raw, a, g, log_decay = _rwkv_tokens(
        pb_ref[0], shift0_ref[0], mu_ref[...], w0_ref[...], wup_ref[...], a0_ref[...], aup_ref[...],
        gup_ref[...], kk_ref[...], ka_ref[...], d_b)
    decay = jnp.exp(log_decay)
    eye = (_iota2((N_B, N_B), 0) == _iota2((N_B, N_B), 1)).astype(F32)
    for h in range(n_heads):
        lanes = slice(h * N_B, (h + 1) * N_B)
        kk = _l2(kk_raw[:, lanes])
        s = s0_ref[0, h]
        sa = jnp.sum(s * kk, axis=-1, keepdims=True)
        v_col = _col_from_row(v[:, lanes], eye)
        s = s * decay[:, lanes] - sa * (kk * a[:, lanes]) + v_col * k_mod[:, lanes]
        s_ref[0, h] = s
        o = _row_from_col(jnp.sum(s * r[:, lanes], axis=-1, keepdims=True), eye)
        o_ref[0, :, lanes] = _rwkv_finish(o, r[:, lanes], k_mod[:, lanes], v[:, lanes], g[:, lanes],
                                          rk_ref[:, lanes], lnw_ref[:, lanes], lnb_ref[:, lanes])


def rwkv_step(pb, shift0, s0, p):
    b, nb = pb.shape
    n_heads = s0.shape[1]
    d_b = n_heads * N_B
    params = _rwkv_params(p, d_b)
    o, s = pl.pallas_call(
        functools.partial(_rwkv_step_kernel, n_heads=n_heads),
        grid=(b,),
        in_specs=[pl.BlockSpec((1, 1, nb), lambda i: (i, 0, 0)),
                  pl.BlockSpec((1, 1, nb), lambda i: (i, 0, 0)),
                  pl.BlockSpec((1, n_heads, N_B, N_B), lambda i: (i, 0, 0, 0))]
                 + [pl.BlockSpec(x.shape, lambda i: (0, 0)) for x in params],
        out_specs=[pl.BlockSpec((1, 1, d_b), lambda i: (i, 0, 0)),
                   pl.BlockSpec((1, n_heads, N_B, N_B), lambda i: (i, 0, 0, 0))],
        out_shape=[jax.ShapeDtypeStruct((b, 1, d_b), F32),
                   jax.ShapeDtypeStruct((b, n_heads, N_B, N_B), F32)],
        compiler_params=_cparams("arbitrary"),
        name="rwkv_step",
    )(pb.reshape(b, 1, nb), shift0.reshape(b, 1, nb), s0, *params)
    return o.reshape(b, d_b), s


def _gdn_gates(gates, alog_row, dtb_row):
    beta = _sigmoid(gates)
    g = -jnp.exp(alog_row) * _softplus(gates + dtb_row)
    return beta, g


def _gdn_chunk_kernel(qkv_ref, z_ref, gates_ref, conv0_ref, s0_ref, cw_ref, alog_ref, dtb_ref, onorm_ref,
                      o_ref, sfin_ref, carry_scr, s_scr, *, n_heads):
    c = pl.program_id(1)
    n_seq, t, _ = qkv_ref.shape
    dqk = n_heads * DK_C

    @pl.when(c == 0)
    def _():
        carry_scr[:, 0:SUBLANES, :] = jnp.zeros((n_seq, SUBLANES, carry_scr.shape[2]), F32)
        carry_scr[:, SUBLANES - (DN_CONV_W - 1):SUBLANES, :] = conv0_ref[...]
        s_scr[...] = s0_ref[...]

    ys = []
    for i in range(n_seq):
        x = qkv_ref[i]
        carry_scr[i, SUBLANES:SUBLANES + t, :] = x
        yi = cw_ref[DN_CONV_W - 1:DN_CONV_W, :] * x
        for d in range(1, DN_CONV_W):
            yi = yi + (cw_ref[DN_CONV_W - 1 - d:DN_CONV_W - d, :]
                       * carry_scr[i, SUBLANES - d:SUBLANES - d + t, :])
        carry_scr[i, 0:SUBLANES, :] = x[t - SUBLANES:t]
        ys.append(yi)
    y = _silu(jnp.concatenate(ys, axis=0))

    row = _iota2((t, t), 0)
    col = _iota2((t, t), 1)
    incl = row >= col
    strict = row > col
    rows = n_seq * t
    srow = _iota2((rows, rows), 0)
    scol = _iota2((rows, rows), 1)
    lg_t = t.bit_length() - 1
    same_seq = (srow >> lg_t) == (scol >> lg_t)
    gates = jnp.concatenate([gates_ref[i] for i in range(n_seq)], axis=0)
    beta_all, g_all = _gdn_gates(gates, alog_ref[...], dtb_ref[...])
    gcum_all = _dot_mask_lhs((same_seq & (srow >= scol)).astype(BF16), g_all)
    g3 = _split3(gcum_all)
    lane = _iota2((t, LANES), 1)
    z = jnp.concatenate([z_ref[i] for i in range(n_seq)], axis=0)

    items = [(i, h) for i in range(n_seq) for h in range(n_heads)]
    n = range(len(items))
    rs = lambda i: slice(i * t, (i + 1) * t)
    q = [_l2(y[rs(i), h * DK_C:(h + 1) * DK_C]) * (DK_C ** -0.5) for i, h in items]
    k = [_l2(y[rs(i), dqk + h * DK_C:dqk + (h + 1) * DK_C]) for i, h in items]
    v = [y[rs(i), 2 * dqk + h * DV_C:2 * dqk + (h + 1) * DV_C] for i, h in items]
    beta = [beta_all[rs(i), h:h + 1] for i, h in items]
    gc = [gcum_all[rs(i), n_heads + h:n_heads + h + 1] for i, h in items]
    egc = [jnp.exp(x) for x in gc]
    nt = lambda a, b: lax.dot_general(a, b, (((1,), (1,)), ((), ())), preferred_element_type=F32)
    gc_row = []
    for i, h in items:
        pk = (lane == n_heads + h).astype(BF16)
        gc_row.append(nt(pk, g3[0][rs(i)]) + (nt(pk, g3[1][rs(i)]) + nt(pk, g3[2][rs(i)])))
    lmask = [jnp.where(incl, jnp.exp(jnp.where(incl, gc[j] - gc_row[j], 0.0)), 0.0) for j in n]
    kb = [k[j] * beta[j] for j in n]
    kq = [_dot_nt(jnp.concatenate([kb[j], q[j]], axis=0), k[j]) for j in n]
    tinv = _tri_inv_multi([jnp.where(strict, kq[j][:t] * lmask[j], 0.0) for j in n])
    uw = [_dot(tinv[j], jnp.concatenate([v[j] * beta[j], kb[j] * egc[j]], axis=1)) for j in n]
    s = [s_scr[i, h] for i, h in items]
    ws = [_dot(jnp.concatenate([uw[j][:, DV_C:], q[j] * egc[j]], axis=0), s[j]) for j in n]
    v_new = [uw[j][:, :DV_C] - ws[j][:t] for j in n]
    o = [ws[j][t:] + _dot(jnp.where(incl, kq[j][t:] * lmask[j], 0.0), v_new[j]) for j in n]
    for j, (i, h) in enumerate(items):
        g_last = gc[j][t - 1:t, :]
        s_scr[i, h] = s[j] * jnp.exp(g_last) + _dot_tn(k[j] * jnp.exp(g_last - gc[j]), v_new[j])
    for j, (i, h) in enumerate(items):
        o_ref[i, :, h * DV_C:(h + 1) * DV_C] = (_rms(o[j], onorm_ref[...])
                                                * _silu(z[rs(i), h * DV_C:(h + 1) * DV_C]))

    @pl.when(c == pl.num_programs(1) - 1)
    def _():
        sfin_ref[...] = s_scr[...]


def _gdn_gate_params(a_log, dt_bias, n_heads):
    pad = lambda x: jnp.zeros((1, LANES), F32).at[0, n_heads:2 * n_heads].set(x)
    return pad(a_log), pad(dt_bias)


def gdn_chunked(qkvz, gates, conv0, s0, cw, a_log, dt_bias, onorm, batch, seq):
    m = qkvz.shape[0]
    n_heads = s0.shape[1]
    dconv = cw.shape[1]
    dvv = n_heads * DV_C
    assert qkvz.shape[1] == dconv + dvv and dconv % dvv == 0
    t = CHUNK
    ns = SEQS_PER_STEP if batch % SEQS_PER_STEP == 0 else 1
    assert seq % t == 0 and t & (t - 1) == 0
    nc = seq // t
    alog_row, dtb_row = _gdn_gate_params(a_log, dt_bias, n_heads)
    const = lambda b, c: (0, 0)
    qkvz3 = qkvz.reshape(batch, seq, dconv + dvv)
    o, s_fin = pl.pallas_call(
        functools.partial(_gdn_chunk_kernel, n_heads=n_heads),
        grid=(batch // ns, nc),
        in_specs=[pl.BlockSpec((ns, t, dconv), lambda b, c: (b, c, 0)),
                  pl.BlockSpec((ns, t, dvv), lambda b, c: (b, c, dconv // dvv)),
                  pl.BlockSpec((ns, t, LANES), lambda b, c: (b, c, 0)),
                  pl.BlockSpec((ns, DN_CONV_W - 1, dconv), lambda b, c: (b, 0, 0)),
                  pl.BlockSpec((ns, n_heads, DK_C, DV_C), lambda b, c: (b, 0, 0, 0)),
                  pl.BlockSpec((DN_CONV_W, dconv), const),
                  pl.BlockSpec((1, LANES), const),
                  pl.BlockSpec((1, LANES), const),
                  pl.BlockSpec((1, DV_C), const)],
        out_specs=[pl.BlockSpec((ns, t, dvv), lambda b, c: (b, c, 0)),
                   pl.BlockSpec((ns, n_heads, DK_C, DV_C), lambda b, c: (b, 0, 0, 0))],
        out_shape=[jax.ShapeDtypeStruct((batch, seq, dvv), F32),
                   jax.ShapeDtypeStruct((batch, n_heads, DK_C, DV_C), F32)],
        scratch_shapes=[pltpu.VMEM((ns, SUBLANES + t, dconv), F32),
                        pltpu.VMEM((ns, n_heads, DK_C, DV_C), F32)],
        compiler_params=_cparams("arbitrary", "arbitrary"),
        name="gdn_chunked",
    )(qkvz3, qkvz3, gates.reshape(batch, seq, LANES), conv0, s0, cw, alog_row, dtb_row,
      onorm.reshape(1, DV_C))
    return o.reshape(m, dvv), s_fin


def _gdn_step_kernel(qkv_ref, z_ref, gates_ref, conv0_ref, s0_ref, cw_ref, alog_ref, dtb_ref, onorm_ref,
                     o_ref, s_ref, *, n_heads):
    dqk = n_heads * DK_C
    x = qkv_ref[0]
    y = cw_ref[DN_CONV_W - 1:DN_CONV_W, :] * x
    for i in range(DN_CONV_W - 1):
        y = y + cw_ref[i:i + 1, :] * conv0_ref[0, i:i + 1, :]
    y = _silu(y)
    beta_all, g_all = _gdn_gates(gates_ref[0], alog_ref[...], dtb_ref[...])
    z = z_ref[0]
    eye = (_iota2((DK_C, DK_C), 0) == _iota2((DK_C, DK_C), 1)).astype(F32)
    for h in range(n_heads):
        q = _l2(y[:, h * DK_C:(h + 1) * DK_C]) * (DK_C ** -0.5)
        k = _l2(y[:, dqk + h * DK_C:dqk + (h + 1) * DK_C])
        v = y[:, 2 * dqk + h * DV_C:2 * dqk + (h + 1) * DV_C]
        beta = beta_all[:, h:h + 1]
        decay = jnp.exp(g_all[:, n_heads + h:n_heads + h + 1])
        s = s0_ref[0, h]
        k_col = _col_from_row(k, eye)
        q_col = _col_from_row(q, eye)
        v_new = beta * v - jnp.sum((k_col * (beta * decay)) * s, axis=0, keepdims=True)
        o = (jnp.sum((q_col * decay) * s, axis=0, keepdims=True)
             + jnp.sum(q * k, axis=-1, keepdims=True) * v_new)
        s_ref[0, h] = s * decay + k_col * v_new
        o_ref[0, :, h * DV_C:(h + 1) * DV_C] = (_rms(o, onorm_ref[...])
                                                * _silu(z[:, h * DV_C:(h + 1) * DV_C]))


def gdn_step(qkvz, gates, conv0, s0, cw, a_log, dt_bias, onorm):
    b = qkvz.shape[0]
    n_heads = s0.shape[1]
    dconv = cw.shape[1]
    dvv = n_heads * DV_C
    alog_row, dtb_row = _gdn_gate_params(a_log, dt_bias, n_heads)
    qkvz3 = qkvz.reshape(b, 1, dconv + dvv)
    const = lambda i: (0, 0)
    o, s = pl.pallas_call(
        functools.partial(_gdn_step_kernel, n_heads=n_heads),
        grid=(b,),
        in_specs=[pl.BlockSpec((1, 1, dconv), lambda i: (i, 0, 0)),
                  pl.BlockSpec((1, 1, dvv), lambda i: (i, 0, dconv // dvv)),
                  pl.BlockSpec((1, 1, LANES), lambda i: (i, 0, 0)),
                  pl.BlockSpec((1, DN_CONV_W - 1, dconv), lambda i: (i, 0, 0)),
                  pl.BlockSpec((1, n_heads, DK_C, DV_C), lambda i: (i, 0, 0, 0)),
                  pl.BlockSpec((DN_CONV_W, dconv), const),
                  pl.BlockSpec((1, LANES), const),
                  pl.BlockSpec((1, LANES), const),
                  pl.BlockSpec((1, DV_C), const)],
        out_specs=[pl.BlockSpec((1, 1, dvv), lambda i: (i, 0, 0)),
                   pl.BlockSpec((1, n_heads, DK_C, DV_C), lambda i: (i, 0, 0, 0))],
        out_shape=[jax.ShapeDtypeStruct((b, 1, dvv), F32),
                   jax.ShapeDtypeStruct((b, n_heads, DK_C, DV_C), F32)],
        compiler_params=_cparams("arbitrary"),
        name="gdn_step",
    )(qkvz3, qkvz3, gates.reshape(b, 1, LANES), conv0, s0, cw, alog_row, dtb_row,
      onorm.reshape(1, DV_C))
    return o.reshape(b, dvv), s


def _row_tile(m, want):
    return want if m % want == 0 else m


def _trunk(x, kv, wkv0, shift0, dn0, dnconv0, ffnconv0, p):
    b, t, d = x.shape
    m = b * t
    depth = p['norm_ffn'].shape[0]
    d_a = p['sb_bias'].shape[1] * DH_A
    nb = p['mu_b'].shape[1]
    dconv = p['conv_dn'].shape[2]
    n_heads_c = p['a_log_dn'].shape[1]
    dvv = n_heads_c * DV_C
    tm = _row_tile(m, PROJ_ROWS)
    tm_out = _row_tile(m, OUT_ROWS)
    new_k, new_v, new_wkv, new_shift, new_dn, new_dnconv, new_ffn = [], [], [], [], [], [], []
    x2 = x.reshape(m, d)
    for l in range(depth):
        i = l // 2
        if l % 2 == 0:
            qkv, pb = norm_matmul(x2, p['norm_mix_even'][i], p['w_in_even'][i], (3 * d_a, nb), tm)
            rp = dict(mu=p['mu_b'][i], w0=p['w0_b'][i], w_up=p['w_up_b'][i], a0=p['a0_b'][i],
                      a_up=p['a_up_b'][i], g_up=p['g_up_b'][i], k_k=p['kk_b'][i], k_a=p['ka_b'][i],
                      r_k=p['rk_b'][i], ln_w=p['lnx_w_b'][i], ln_b=p['lnx_b_b'][i])
            if kv is None:
                o_a, kn_t, v_t = sb_prompt(qkv, p['qnorm_a'][i], p['knorm_a'][i], p['sb_bias'][i], b, t)
                o_b, wkv = rwkv_chunked(pb, shift0[i], wkv0[i], rp, b, t)
                shift = pb.reshape(b, t, nb)[:, t - 1]
                new_k.append(jnp.transpose(kn_t, (0, 3, 1, 2)))
                new_v.append(jnp.transpose(v_t, (0, 3, 1, 2)))
            else:
                o_a, kn = sb_decode(qkv, kv[0], kv[1], kv[2], i, p['qnorm_a'][i], p['knorm_a'][i],
                                    p['sb_bias'][i])
                o_b, wkv = rwkv_step(pb, shift0[i], wkv0[i], rp)
                shift = pb
                new_k.append(kn.reshape(b, t, d_a // DH_A, DH_A))
                new_v.append(qkv[:, 2 * d_a:].reshape(b, t, d_a // DH_A, DH_A))
            w_out = p['w_out_even'][i]
            mix, w_mix = [o_a, o_b], [w_out[:d_a], w_out[d_a:]]
            new_wkv.append(wkv)
            new_shift.append(shift)
        else:
            qkvz, gates = norm_matmul(x2, p['norm_mix_odd'][i], p['w_in_odd'][i], (dconv + dvv, LANES), tm)
            if kv is None:
                o_c, s_fin = gdn_chunked(qkvz, gates, dnconv0[i], dn0[i], p['conv_dn'][i], p['a_log_dn'][i],
                                         p['dt_bias_dn'][i], p['onorm_dn'][i], b, t)
                cbuf = qkvz.reshape(b, t, dconv + dvv)[:, t - (DN_CONV_W - 1):, :dconv]
            else:
                o_c, s_fin = gdn_step(qkvz, gates, dnconv0[i], dn0[i], p['conv_dn'][i], p['a_log_dn'][i],
                                      p['dt_bias_dn'][i], p['onorm_dn'][i])
                cbuf = jnp.concatenate([dnconv0[i][:, 1:], qkvz[:, None, :dconv]], axis=1)
            mix, w_mix = [o_c], [p['w_out_odd'][i]]
            new_dn.append(s_fin)
            new_dnconv.append(cbuf)
        ffn_args = (p['norm_ffn'][l], p['w_gate'][l], p['w_up'][l], p['conv_ffn'][l], p['conv_ffn_b'][l],
                    p['w_down'][l])
        if kv is None:
            x3, fbuf = conv_ffn([o.reshape(b, t, -1) for o in mix], w_mix, x2.reshape(b, t, d),
                                ffnconv0[l], *ffn_args, tm=_row_tile(t, FFN_ROWS))
            x2 = x3.reshape(m, d)
        else:
            x2 = matmul_res(mix, w_mix, x2, tm_out)
            x2, fbuf = conv_ffn_step(x2, ffnconv0[l], *ffn_args)
        new_ffn.append(fbuf)
    states = tuple(jnp.stack(s) for s in (new_k, new_v, new_wkv, new_shift, new_dn, new_dnconv, new_ffn))
    return x2.reshape(b, t, d), states


def kernel(x_prompt, x_sample, cache_k, cache_v, page_table, state_wkv, state_shift, state_dn,
           state_dn_conv, state_ffn_conv, norm_mix_even, w_in_even, qnorm_a, knorm_a, sb_bias,
           mu_b, w0_b, w_up_b, a0_b, a_up_b, g_up_b, kk_b, ka_b, rk_b, lnx_w_b, lnx_b_b,
           w_out_even, norm_mix_odd, w_in_odd, conv_dn, a_log_dn, dt_bias_dn, onorm_dn, w_out_odd,
           norm_ffn, w_gate, w_up, conv_ffn, conv_ffn_b, w_down):
    n_heads_c = a_log_dn.shape[1]
    n_gate_cols = 2 * n_heads_c
    assert x_sample.shape[1] == 1 and n_gate_cols <= LANES
    w_in_odd_p = jnp.pad(w_in_odd, ((0, 0), (0, 0), (0, LANES - n_gate_cols)))
    bf = lambda w: w.astype(BF16)
    p = dict(norm_mix_even=norm_mix_even, w_in_even=bf(w_in_even), qnorm_a=qnorm_a, knorm_a=knorm_a,
             sb_bias=sb_bias, mu_b=mu_b, w0_b=w0_b, w_up_b=bf(w_up_b), a0_b=a0_b, a_up_b=bf(a_up_b),
             g_up_b=bf(g_up_b), kk_b=kk_b, ka_b=ka_b, rk_b=rk_b, lnx_w_b=lnx_w_b, lnx_b_b=lnx_b_b,
             w_out_even=bf(w_out_even), norm_mix_odd=norm_mix_odd, w_in_odd=bf(w_in_odd_p),
             conv_dn=conv_dn, a_log_dn=a_log_dn, dt_bias_dn=dt_bias_dn, onorm_dn=onorm_dn,
             w_out_odd=bf(w_out_odd), norm_ffn=norm_ffn, w_gate=bf(w_gate), w_up=bf(w_up),
             conv_ffn=conv_ffn, conv_ffn_b=conv_ffn_b, w_down=bf(w_down))
    b = x_prompt.shape[0]
    n_ab = state_wkv.shape[0]
    n_c = state_dn.shape[0]
    depth = norm_ffn.shape[0]
    zeros = lambda like, lead: jnp.zeros((lead, b) + like.shape[2:], F32)
    y_prompt, ps = _trunk(x_prompt, None, zeros(state_wkv, n_ab), zeros(state_shift, n_ab),
                          zeros(state_dn, n_c), zeros(state_dn_conv, n_c), zeros(state_ffn_conv, depth), p)
    y_sample, ss = _trunk(x_sample, (cache_k, cache_v, page_table), state_wkv, state_shift, state_dn,
                          state_dn_conv, state_ffn_conv, p)
    return (y_prompt, y_sample, ps[0], ps[1], ss[0], ss[1], ps[2], ss[2], ps[3], ss[3],
            ps[4], ss[4], ps[5], ss[5], ps[6], ss[6])
```

```python
import functools

import jax
import jax.numpy as jnp
from jax import lax
from jax.experimental import pallas as pl
from jax.experimental.pallas import tpu as pltpu

F32 = jnp.float32
BF16 = jnp.bfloat16

RMS_EPS = 1e-6
L2_EPS = 1e-6
GN_EPS = 64e-5

DH_A = 64
N_B = 64
R_W, R_A, R_G = 64, 64, 128
DK_C = 128
DV_C = 128
DN_CONV_W = 4
FFN_CONV_W = 3

SB_BLOCK = 128
CHUNK = 64
SEQS_PER_STEP = 4
SUBLANES = 8
LANES = 128
VMEM_LIMIT_BYTES = 56 * 1024 * 1024
PROJ_ROWS = 512
OUT_ROWS = 1024
FFN_ROWS = 512
FFN_COLS = 256


def _cparams(*sem):
    return pltpu.CompilerParams(dimension_semantics=sem, vmem_limit_bytes=VMEM_LIMIT_BYTES)


def _dot(a, b):
    return jnp.dot(a.astype(BF16), b.astype(BF16), preferred_element_type=F32)


def _dot_nt(a, b):
    return lax.dot_general(a.astype(BF16), b.astype(BF16), (((1,), (1,)), ((), ())),
                           preferred_element_type=F32)


def _dot_tn(a, b):
    return lax.dot_general(a.astype(BF16), b.astype(BF16), (((0,), (0,)), ((), ())),
                           preferred_element_type=F32)


def _split2(a):
    hi = a.astype(BF16)
    lo = (a - hi.astype(F32)).astype(BF16)
    return hi, lo


def _split3(a):
    hi = a.astype(BF16)
    r1 = a - hi.astype(F32)
    mid = r1.astype(BF16)
    lo = (r1 - mid.astype(F32)).astype(BF16)
    return hi, mid, lo


def _dot_mask_lhs(mask_bf16, x):
    d = functools.partial(jnp.dot, preferred_element_type=F32)
    h, m, l = _split3(x)
    return d(mask_bf16, h) + (d(mask_bf16, m) + d(mask_bf16, l))


def _softplus(x):
    return jnp.maximum(x, 0.0) + jnp.log(1.0 + jnp.exp(-jnp.abs(x)))


def _sigmoid(x):
    return 1.0 / (1.0 + jnp.exp(-x))


def _silu(x):
    return x * _sigmoid(x)


def _rms(x, g):
    return x * lax.rsqrt(jnp.mean(x * x, axis=-1, keepdims=True) + RMS_EPS) * g


def _l2(x):
    return x * lax.rsqrt(jnp.sum(x * x, axis=-1, keepdims=True) + L2_EPS)


def _iota2(shape, dim):
    return lax.broadcasted_iota(jnp.int32, shape, dim)


def _shift_rows(cur, carry, d):
    rolled = pltpu.roll(cur, d, 0)
    crolled = pltpu.roll(carry, d, 0)
    first = jnp.where(_iota2(carry.shape, 0) < d, crolled, rolled[:SUBLANES])
    if cur.shape[0] == SUBLANES:
        return first
    return jnp.concatenate([first, rolled[SUBLANES:]], axis=0)


def _tri_inv_multi(a_list):
    n = a_list[0].shape[0]
    row = _iota2((n, n), 0)
    col = _iota2((n, n), 1)
    eye = (row == col).astype(F32)
    d = functools.partial(jnp.dot, preferred_element_type=F32)
    ts = [eye - jnp.where((row >> 1) == (col >> 1), a, 0.0) for a in a_list]
    s = 2
    while s < n:
        sh = s.bit_length() - 1
        sub = ((row >> (sh + 1)) == (col >> (sh + 1))) & ((row >> sh) != (col >> sh))
        tb = [t.astype(BF16) for t in ts]
        tes = [d(t, jnp.where(sub, a, 0.0).astype(BF16)).astype(BF16) for t, a in zip(tb, a_list)]
        ts = [t - d(te, t16) for t, te, t16 in zip(ts, tes, tb)]
        s *= 2
    return ts


def _norm_matmul_kernel(x_ref, g_ref, w_ref, *o_refs, splits):
    h = _rms(x_ref[...], g_ref[...]).astype(BF16)
    y = jnp.dot(h, w_ref[...], preferred_element_type=F32)
    off = 0
    for o_ref, n in zip(o_refs, splits):
        o_ref[...] = y[:, off:off + n]
        off += n


def norm_matmul(x, g, w, splits, tm):
    m, k = x.shape
    n = w.shape[1]
    assert sum(splits) == n and m % tm == 0
    return pl.pallas_call(
        functools.partial(_norm_matmul_kernel, splits=tuple(splits)),
        grid=(m // tm,),
        in_specs=[pl.BlockSpec((tm, k), lambda i: (i, 0)),
                  pl.BlockSpec((1, k), lambda i: (0, 0)),
                  pl.BlockSpec((k, n), lambda i: (0, 0))],
        out_specs=[pl.BlockSpec((tm, s), lambda i: (i, 0)) for s in splits],
        out_shape=[jax.ShapeDtypeStruct((m, s), F32) for s in splits],
        compiler_params=_cparams("arbitrary"),
        name="norm_matmul",
    )(x, g.reshape(1, k), w)


def _matmul_res_kernel(*refs, n_in):
    a_refs = refs[:n_in]
    w_refs = refs[n_in:2 * n_in]
    res_ref = refs[2 * n_in]
    o_ref = refs[2 * n_in + 1]
    acc = res_ref[...]
    for a_ref, w_ref in zip(a_refs, w_refs):
        acc = acc + jnp.dot(a_ref[...].astype(BF16), w_ref[...], preferred_element_type=F32)
    o_ref[...] = acc


def matmul_res(a_list, w_list, res, tm):
    m, n = res.shape
    assert m % tm == 0
    n_in = len(a_list)
    in_specs = ([pl.BlockSpec((tm, a.shape[1]), lambda i: (i, 0)) for a in a_list]
                + [pl.BlockSpec(w.shape, lambda i: (0, 0)) for w in w_list]
                + [pl.BlockSpec((tm, n), lambda i: (i, 0))])
    return pl.pallas_call(
        functools.partial(_matmul_res_kernel, n_in=n_in),
        grid=(m // tm,),
        in_specs=in_specs,
        out_specs=pl.BlockSpec((tm, n), lambda i: (i, 0)),
        out_shape=jax.ShapeDtypeStruct((m, n), F32),
        compiler_params=_cparams("arbitrary"),
        name="matmul_res",
    )(*a_list, *w_list, res)


def _ffn_kernel(*refs, n_mix):
    mix_refs = refs[:n_mix]
    wmix_refs = refs[n_mix:2 * n_mix]
    (x_ref, buf_ref, g_ref, wg_ref, wu_ref, cw_ref, cb_ref, wd_ref,
     y_ref, tail_ref, carry_scr, act_scr) = refs[2 * n_mix:]
    ti = pl.program_id(1)
    x = x_ref[0]
    for o_ref, w_ref in zip(mix_refs, wmix_refs):
        x = x + jnp.dot(o_ref[0].astype(BF16), w_ref[...], preferred_element_type=F32)
    tm = x.shape[0]
    f = wg_ref.shape[1]
    h = _rms(x, g_ref[...]).astype(BF16)

    @pl.when(ti == 0)
    def _():
        carry_scr[...] = jnp.zeros_like(carry_scr)
        carry_scr[SUBLANES - 2:SUBLANES, :] = buf_ref[0]

    for c0 in range(0, f, FFN_COLS):
        cols = slice(c0, min(c0 + FFN_COLS, f))
        gate = jnp.dot(h, wg_ref[:, cols], preferred_element_type=F32)
        up = jnp.dot(h, wu_ref[:, cols], preferred_element_type=F32)
        carry = carry_scr[:, cols]
        conv = (cw_ref[2:3, cols] * gate + cw_ref[1:2, cols] * _shift_rows(gate, carry, 1)
                + cw_ref[0:1, cols] * _shift_rows(gate, carry, 2))
        carry_scr[:, cols] = gate[tm - SUBLANES:tm]
        tail_ref[0, :, cols] = gate[tm - 2:tm]
        act_scr[:, cols] = (_silu(conv + cb_ref[:, cols]) * up).astype(BF16)
    y_ref[0] = x + jnp.dot(act_scr[...], wd_ref[...], preferred_element_type=F32)


def conv_ffn(mix, w_mix, x, buf, g, wg, wu, cw, cb, wd, tm):
    b, t, d = x.shape
    f = wg.shape[1]
    assert t % tm == 0 and tm % SUBLANES == 0
    const = lambda bi, ti: (0, 0)
    return pl.pallas_call(
        functools.partial(_ffn_kernel, n_mix=len(mix)),
        grid=(b, t // tm),
        in_specs=[pl.BlockSpec((1, tm, o.shape[2]), lambda bi, ti: (bi, ti, 0)) for o in mix]
                 + [pl.BlockSpec(w.shape, const) for w in w_mix]
                 + [pl.BlockSpec((1, tm, d), lambda bi, ti: (bi, ti, 0)),
                  pl.BlockSpec((1, 2, f), lambda bi, ti: (bi, 0, 0)),
                  pl.BlockSpec((1, d), const),
                  pl.BlockSpec((d, f), const),
                  pl.BlockSpec((d, f), const),
                  pl.BlockSpec((FFN_CONV_W, f), const),
                  pl.BlockSpec((1, f), const),
                  pl.BlockSpec((f, d), const)],
        out_specs=[pl.BlockSpec((1, tm, d), lambda bi, ti: (bi, ti, 0)),
                   pl.BlockSpec((1, 2, f), lambda bi, ti: (bi, 0, 0))],
        out_shape=[jax.ShapeDtypeStruct((b, t, d), F32),
                   jax.ShapeDtypeStruct((b, 2, f), F32)],
        scratch_shapes=[pltpu.VMEM((SUBLANES, f), F32), pltpu.VMEM((tm, f), BF16)],
        compiler_params=_cparams("arbitrary", "arbitrary"),
        name="conv_ffn",
    )(*mix, *w_mix, x, buf, g.reshape(1, d), wg, wu, cw, cb.reshape(1, f), wd)


def _ffn_step_kernel(x_ref, buf0_ref, buf1_ref, g_ref, wg_ref, wu_ref, cw_ref, cb_ref, wd_ref,
                     y_ref, gate_ref):
    x = x_ref[...]
    h = _rms(x, g_ref[...]).astype(BF16)
    gate = jnp.dot(h, wg_ref[...], preferred_element_type=F32)
    up = jnp.dot(h, wu_ref[...], preferred_element_type=F32)
    conv = cw_ref[2:3, :] * gate + cw_ref[1:2, :] * buf1_ref[...] + cw_ref[0:1, :] * buf0_ref[...]
    act = (_silu(conv + cb_ref[...]) * up).astype(BF16)
    y_ref[...] = x + jnp.dot(act, wd_ref[...], preferred_element_type=F32)
    gate_ref[...] = gate


def conv_ffn_step(x, buf, g, wg, wu, cw, cb, wd):
    b, d = x.shape
    f = wg.shape[1]
    full = lambda shape: pl.BlockSpec(shape, lambda i: (0,) * len(shape))
    y, gate = pl.pallas_call(
        _ffn_step_kernel,
        grid=(1,),
        in_specs=[full((b, d)), full((b, f)), full((b, f)), full((1, d)), full((d, f)), full((d, f)),
                  full((FFN_CONV_W, f)), full((1, f)), full((f, d))],
        out_specs=[full((b, d)), full((b, f))],
        out_shape=[jax.ShapeDtypeStruct((b, d), F32), jax.ShapeDtypeStruct((b, f), F32)],
        compiler_params=_cparams("arbitrary"),
        name="conv_ffn_step",
    )(x, buf[:, 0], buf[:, 1], g.reshape(1, d), wg, wu, cw, cb.reshape(1, f), wd)
    return y, jnp.stack([buf[:, 1], gate], axis=1)


def _pair_rms(x, w_row):
    left = _iota2(x.shape, 1) < DH_A
    x2 = x * x
    ms_l = jnp.sum(jnp.where(left, x2, 0.0), axis=-1, keepdims=True) * (1.0 / DH_A)
    ms_r = jnp.sum(jnp.where(left, 0.0, x2), axis=-1, keepdims=True) * (1.0 / DH_A)
    inv = jnp.where(left, lax.rsqrt(ms_l + RMS_EPS), lax.rsqrt(ms_r + RMS_EPS))
    return x * inv * w_row


def _sb_prompt_kernel(bias_ref, q_ref, k_ref, v_ref, qw_ref, kw_ref, o_ref, knt_ref, vt_ref,
                      kn_scr, v_scr, q_scr, later_scr, o_scr, tail_scr, *, n_heads):
    qi = pl.program_id(1)
    blk = q_ref.shape[0]
    n_pairs = n_heads // 2
    pw = 2 * DH_A
    scale = DH_A ** -0.5
    row0 = pl.multiple_of(qi * blk, blk)

    k = k_ref[...]
    v = v_ref[...]
    kn = jnp.concatenate([_pair_rms(k[:, p * pw:(p + 1) * pw], kw_ref[...]) for p in range(n_pairs)],
                         axis=-1)
    for h in range(n_heads):
        knt_ref[0, h] = kn[:, h * DH_A:(h + 1) * DH_A].T
        vt_ref[0, h] = v[:, h * DH_A:(h + 1) * DH_A].T
    kn_scr[pl.ds(row0, blk), :] = kn.astype(BF16)
    v_scr[pl.ds(row0, blk), :] = v.astype(BF16)

    q = q_ref[...]
    left = _iota2((blk, pw), 1) < DH_A
    for p in range(n_pairs):
        qn = _pair_rms(q[:, p * pw:(p + 1) * pw], qw_ref[...]) * scale
        q_scr[p] = jnp.concatenate([jnp.where(left, qn, 0.0), jnp.where(left, 0.0, qn)],
                                   axis=0).astype(BF16)
    krow = _iota2((2 * blk, 2 * blk), 0)
    krow = jnp.where(krow >= blk, krow - blk, krow)
    kcol = _iota2((2 * blk, 2 * blk), 1)
    later_scr[...] = jnp.where((kcol >= blk) | (krow > kcol), 1.0, 0.0).astype(BF16)
    o_scr[...] = jnp.zeros_like(o_scr)
    tail_scr[...] = jnp.zeros_like(tail_scr)

    qrow = _iota2((2 * blk, blk), 0)
    causal = _iota2((2 * blk, blk), 1) < jnp.where(qrow >= blk, qrow - blk, qrow)
    vleft = _iota2((blk, pw), 1) < DH_A

    def blocks(offs, diag):
        items = [(j, p) for j in range(len(offs)) for p in range(n_pairs)]
        zs = []
        for j, p in items:
            z2 = lax.dot_general(q_scr[p], kn_scr[pl.ds(offs[j], blk), p * pw:(p + 1) * pw],
                                 (((1,), (1,)), ((), ())), preferred_element_type=F32)
            zs.append(jnp.concatenate([z2[:blk] + bias_ref[2 * p], z2[blk:] + bias_ref[2 * p + 1]],
                                      axis=0))
        sps = [_softplus(z) for z in zs]
        if diag:
            sps = [jnp.where(causal, sp, 0.0) for sp in sps]
        his = [sp.astype(BF16) for sp in sps]
        los = [(sp - hi.astype(F32)).astype(BF16) for sp, hi in zip(sps, his)]
        lo_mat = later_scr[...]
        cums = [jnp.dot(jnp.concatenate([hi, lo], axis=1), lo_mat, preferred_element_type=F32)
                for hi, lo in zip(his, los)]
        ws = []
        for p in range(n_pairs):
            tail = tail_scr[p]
            for j in range(len(offs)):
                n = j * n_pairs + p
                w = jnp.exp((zs[n] - sps[n]) - (cums[n][:, :blk] + tail))
                if diag:
                    w = jnp.where(causal, w, 0.0)
                tail = tail + cums[n][:, blk:]
                ws.append(w.astype(BF16))
            tail_scr[p] = tail
        for p in range(n_pairs):
            acc = o_scr[:, p * pw:(p + 1) * pw]
            for j in range(len(offs)):
                w = ws[p * len(offs) + j]
                vj = v_scr[pl.ds(offs[j], blk), p * pw:(p + 1) * pw]
                zero = jnp.zeros_like(vj)
                v_bd = jnp.concatenate([jnp.where(vleft, vj, zero), jnp.where(vleft, zero, vj)], axis=0)
                acc = acc + jnp.dot(jnp.concatenate([w[:blk], w[blk:]], axis=1), v_bd,
                                    preferred_element_type=F32)
            o_scr[:, p * pw:(p + 1) * pw] = acc

    blocks([row0], True)

    def body(i, carry):
        hi_off = pl.multiple_of((qi - 1 - 2 * i) * blk, blk)
        blocks([hi_off, pl.multiple_of(hi_off - blk, blk)], False)
        return carry

    lax.fori_loop(0, qi // 2, body, 0)

    @pl.when(qi % 2 == 1)
    def _():
        blocks([0], False)

    o_ref[...] = o_scr[...]


def sb_prompt(qkv, qw, kw, bias, batch, seq):
    m, n3 = qkv.shape
    d_a = n3 // 3
    n_heads = d_a // DH_A
    blk = SB_BLOCK
    assert seq % blk == 0 and n_heads % 2 == 0 and 2 * DH_A == LANES
    nq = seq // blk
    rowblk = lambda col: (lambda b, qi: (b * nq + qi, col))
    pair_w = lambda w: jnp.tile(w, 2).reshape(1, 2 * DH_A)
    head_t = pl.BlockSpec((1, n_heads, DH_A, blk), lambda b, qi: (b, 0, 0, qi))
    head_t_shape = jax.ShapeDtypeStruct((batch, n_heads, DH_A, seq), F32)
    return pl.pallas_call(
        functools.partial(_sb_prompt_kernel, n_heads=n_heads),
        grid=(batch, nq),
        in_specs=[pl.BlockSpec(memory_space=pltpu.SMEM),
                  pl.BlockSpec((blk, d_a), rowblk(0)),
                  pl.BlockSpec((blk, d_a), rowblk(1)),
                  pl.BlockSpec((blk, d_a), rowblk(2)),
                  pl.BlockSpec((1, 2 * DH_A), lambda b, qi: (0, 0)),
                  pl.BlockSpec((1, 2 * DH_A), lambda b, qi: (0, 0))],
        out_specs=[pl.BlockSpec((blk, d_a), rowblk(0)), head_t, head_t],
        out_shape=[jax.ShapeDtypeStruct((m, d_a), F32), head_t_shape, head_t_shape],
        scratch_shapes=[pltpu.VMEM((seq, d_a), BF16), pltpu.VMEM((seq, d_a), BF16),
                        pltpu.VMEM((n_heads // 2, 2 * blk, 2 * DH_A), BF16),
                        pltpu.VMEM((2 * blk, 2 * blk), BF16),
                        pltpu.VMEM((blk, d_a), F32),
                        pltpu.VMEM((n_heads // 2, 2 * blk, blk), F32)],
        compiler_params=_cparams("arbitrary", "arbitrary"),
        name="sb_prompt",
    )(bias, qkv, qkv, qkv, pair_w(qw), pair_w(kw))


DECODE_BANK = 4
DECODE_SLOTS = 16


def _sb_decode_kernel(pt_ref, q_ref, k_ref, ck_hbm, cv_hbm, qw_ref, kw_ref, bias_ref,
                      o_ref, kn_ref, kbuf, vbuf, sem, *, n_heads, n_pages, layer):
    b = pl.program_id(0)
    n_seq = pl.num_programs(0)
    page = kbuf.shape[3]
    scale = DH_A ** -0.5
    nt = lambda x, y: lax.dot_general(x, y, (((1,), (1,)), ((), ())), preferred_element_type=F32)

    def copies(seq, i, slot):
        phys = pt_ref[seq, n_pages - 1 - i]
        return (pltpu.make_async_copy(ck_hbm.at[layer, phys], kbuf.at[slot], sem.at[0, slot]),
                pltpu.make_async_copy(cv_hbm.at[layer, phys], vbuf.at[slot], sem.at[1, slot]))

    @pl.when(b == 0)
    def _():
        for j in range(DECODE_SLOTS):
            for cp in copies(0, j, j):
                cp.start()

    q = q_ref[0]
    k = k_ref[0]
    hrow = _iota2((n_heads, DH_A), 0)
    qe, kn = [], []
    for h in range(n_heads):
        lanes = slice(h * DH_A, (h + 1) * DH_A)
        qn = _rms(q[:, lanes], qw_ref[...]) * scale
        qe.append(jnp.where(hrow == h, jnp.broadcast_to(qn, (n_heads, DH_A)), 0.0).astype(BF16))
        kn.append(_rms(k[:, lanes], kw_ref[...]))
    kn_ref[0] = jnp.concatenate(kn, axis=-1)

    krow = _iota2((page, 2 * page), 0)
    kcol = _iota2((page, 2 * page), 1)
    later_ones = jnp.where((kcol >= page) | (krow > kcol), 1.0, 0.0).astype(BF16)
    bias = bias_ref[...]

    def bank_pages(first, bank, tail, acc):
        pages = range(DECODE_BANK)
        slots = [bank * DECODE_BANK + j for j in pages]
        for j in pages:
            for cp in copies(b, first + j, slots[j]):
                cp.wait()
        zs = []
        for j in pages:
            z = bias
            for h in range(n_heads):
                z = z + jnp.dot(qe[h], kbuf[slots[j], h].astype(BF16), preferred_element_type=F32)
            zs.append(z)
        sps = [_softplus(z) for z in zs]
        his = [sp.astype(BF16) for sp in sps]
        los = [(sp - hi.astype(F32)).astype(BF16) for sp, hi in zip(sps, his)]
        cums = [jnp.dot(hi, later_ones, preferred_element_type=F32)
                + jnp.dot(lo, later_ones, preferred_element_type=F32) for hi, lo in zip(his, los)]
        ws = []
        for j in pages:
            ws.append(jnp.exp((zs[j] - sps[j]) - (cums[j][:, :page] + tail)).astype(BF16))
            tail = tail + cums[j][:, page:]
        for j in pages:
            acc = [acc[h] + nt(ws[j], vbuf[slots[j], h].astype(BF16)) for h in range(n_heads)]

        more = first + DECODE_SLOTS < n_pages

        @pl.when(more)
        def _():
            for j in pages:
                for cp in copies(b, first + DECODE_SLOTS + j, slots[j]):
                    cp.start()

        @pl.when(jnp.logical_not(more) & (b + 1 < n_seq))
        def _():
            for j in pages:
                for cp in copies(b + 1, slots[j], slots[j]):
                    cp.start()

        return tail, acc

    def group(gi, carry):
        tail, acc = carry
        for bank in range(DECODE_SLOTS // DECODE_BANK):
            tail, acc = bank_pages(gi * DECODE_SLOTS + bank * DECODE_BANK, bank, tail, acc)
        return tail, acc

    zero = jnp.zeros((n_heads, DH_A), F32)
    _, acc = lax.fori_loop(0, n_pages // DECODE_SLOTS, group,
                           (jnp.zeros((n_heads, page), F32), [zero] * n_heads))
    o_ref[0] = jnp.concatenate([acc[h][h:h + 1, :] for h in range(n_heads)], axis=-1)


def sb_decode(qkv, cache_k, cache_v, page_table, layer, qw, kw, bias):
    b, n3 = qkv.shape
    d_a = n3 // 3
    n_heads = d_a // DH_A
    n_layers, n_phys, page = cache_k.shape[:3]
    n_pages = page_table.shape[1]
    assert n_pages % DECODE_SLOTS == 0 and cache_k.shape[3] == n_heads
    ck_t = jnp.transpose(cache_k, (0, 1, 3, 4, 2))
    cv_t = jnp.transpose(cache_v, (0, 1, 3, 4, 2))
    qkv3 = qkv.reshape(b, 1, n3)
    grid_spec = pltpu.PrefetchScalarGridSpec(
        num_scalar_prefetch=1,
        grid=(b,),
        in_specs=[pl.BlockSpec((1, 1, d_a), lambda bi, pt: (bi, 0, 0)),
                  pl.BlockSpec((1, 1, d_a), lambda bi, pt: (bi, 0, 1)),
                  pl.BlockSpec(memory_space=pl.ANY),
                  pl.BlockSpec(memory_space=pl.ANY),
                  pl.BlockSpec((1, DH_A), lambda bi, pt: (0, 0)),
                  pl.BlockSpec((1, DH_A), lambda bi, pt: (0, 0)),
                  pl.BlockSpec((n_heads, 1), lambda bi, pt: (0, 0))],
        out_specs=[pl.BlockSpec((1, 1, d_a), lambda bi, pt: (bi, 0, 0)),
                   pl.BlockSpec((1, 1, d_a), lambda bi, pt: (bi, 0, 0))],
        scratch_shapes=[pltpu.VMEM((DECODE_SLOTS, n_heads, DH_A, page), F32),
                        pltpu.VMEM((DECODE_SLOTS, n_heads, DH_A, page), F32),
                        pltpu.SemaphoreType.DMA((2, DECODE_SLOTS))],
    )
    o, kn = pl.pallas_call(
        functools.partial(_sb_decode_kernel, n_heads=n_heads, n_pages=n_pages, layer=layer),
        grid_spec=grid_spec,
        out_shape=[jax.ShapeDtypeStruct((b, 1, d_a), F32), jax.ShapeDtypeStruct((b, 1, d_a), F32)],
        compiler_params=_cparams("arbitrary"),
        name="sb_decode",
    )(page_table, qkv3, qkv3, ck_t, cv_t, qw.reshape(1, DH_A), kw.reshape(1, DH_A),
      bias.reshape(n_heads, 1))
    return o.reshape(b, d_a), kn.reshape(b, d_a)


def _rwkv_tokens(pb, prev, mu, w0, w_up, a0, a_up, g_up, k_k, k_a, d_b):
    xs = pb + (prev - pb) * mu
    r = xs[:, 0:d_b]
    k = xs[:, d_b:2 * d_b]
    v = xs[:, 2 * d_b:3 * d_b]
    xw = xs[:, 3 * d_b:3 * d_b + R_W]
    xa = xs[:, 3 * d_b + R_W:3 * d_b + R_W + R_A]
    xg = xs[:, 3 * d_b + R_W + R_A:3 * d_b + R_W + R_A + R_G]
    w_log = -_softplus(-(w0 + _dot(jnp.tanh(xw), w_up))) - 0.5
    log_decay = -jnp.exp(w_log)
    a = _sigmoid(a0 + _dot(xa, a_up))
    g = _dot(_sigmoid(xg), g_up)
    kk_raw = k * k_k
    k_mod = k * (1.0 + (a - 1.0) * k_a)
    return r, k_mod, v, kk_raw, a, g, log_decay


def _rwkv_finish(o, r, k_mod, v, g, r_k, ln_w, ln_b):
    mean = jnp.mean(o, axis=-1, keepdims=True)
    var = jnp.mean(jnp.square(o - mean), axis=-1, keepdims=True)
    on = (o - mean) * lax.rsqrt(var + GN_EPS) * ln_w + ln_b
    bonus = jnp.sum(r * k_mod * r_k, axis=-1, keepdims=True) * v
    return (on + bonus) * g


def _rwkv_chunk_kernel(pb_ref, shift0_ref, s0_ref, mu_ref, w0_ref, wup_ref, a0_ref, aup_ref, gup_ref,
                       kk_ref, ka_ref, rk_ref, lnw_ref, lnb_ref, hsum_ref, o_ref, sfin_ref,
                       carry_scr, s_scr, o_scr, *, n_heads):
    c = pl.program_id(1)
    d_b = n_heads * N_B
    n_seq, t, nb = pb_ref.shape

    @pl.when(c == 0)
    def _():
        carry_scr[...] = jnp.zeros_like(carry_scr)
        carry_scr[:, SUBLANES - 1:SUBLANES, :] = shift0_ref[...]
        s_scr[...] = s0_ref[...]

    pbs = [pb_ref[i] for i in range(n_seq)]
    prev = jnp.concatenate([_shift_rows(pbs[i], carry_scr[i], 1) for i in range(n_seq)], axis=0)
    for i in range(n_seq):
        carry_scr[i] = pbs[i][t - SUBLANES:t]
    r, k_mod, v, kk_raw, a, g, log_decay = _rwkv_tokens(
        jnp.concatenate(pbs, axis=0), prev, mu_ref[...], w0_ref[...], wup_ref[...], a0_ref[...],
        aup_ref[...], gup_ref[...], kk_ref[...], ka_ref[...], d_b)

    row = _iota2((t, t), 0)
    col = _iota2((t, t), 1)
    incl = row >= col
    strict = row > col
    rows = n_seq * t
    srow = _iota2((rows, rows), 0)
    scol = _iota2((rows, rows), 1)
    lg_t = t.bit_length() - 1
    same_seq = (srow >> lg_t) == (scol >> lg_t)
    cum = _dot_mask_lhs((same_seq & (srow >= scol)).astype(BF16), log_decay)
    cum_prev = cum - log_decay
    cum_last = jnp.concatenate(
        [jnp.broadcast_to(cum[(i + 1) * t - 1:(i + 1) * t, :], (t, d_b)) for i in range(n_seq)], axis=0)
    e_cum = jnp.exp(cum)
    e_prev = jnp.exp(cum_prev)
    e_neg = jnp.exp(-cum)
    e_rest = jnp.exp(cum_last - cum)
    e_last = jnp.exp(cum_last)

    items = [(i, h) for i in range(n_seq) for h in range(n_heads)]
    n = range(len(items))
    sub = lambda z, i, h: z[i * t:(i + 1) * t, h * N_B:(h + 1) * N_B]
    pick = lambda z: [sub(z, i, h) for i, h in items]
    def head_sum(z):
        z_hi, z_lo = _split2(z)
        return (jnp.dot(z_hi, hsum_ref[...], preferred_element_type=F32)
                + jnp.dot(z_lo, hsum_ref[...], preferred_element_type=F32))

    kk_all = kk_raw * lax.rsqrt(head_sum(kk_raw * kk_raw) + L2_EPS)
    ka_all = kk_all * a
    v_ = pick(v)
    x = [jnp.concatenate([p, q], axis=0) for p, q in zip(pick(kk_all * e_prev), pick(r * e_cum))]
    khat = pick(k_mod * e_neg)
    bhat = pick(ka_all * e_neg)
    k_rest = pick(k_mod * e_rest)
    b_rest = pick(ka_all * e_rest)
    elast_ = pick(e_last)
    s = [s_scr[i, h] for i, h in items]
    xb = [_dot_nt(x[j], bhat[j]) for j in n]
    xk = [_dot_nt(x[j], khat[j]) for j in n]
    xs = [_dot_nt(x[j], s[j]) for j in n]
    tinv = _tri_inv_multi([jnp.where(strict, xb[j][:t], 0.0) for j in n])
    rhs = [xs[j][:t] + _dot(jnp.where(strict, xk[j][:t], 0.0), v_[j]) for j in n]
    u = [_dot(tinv[j], rhs[j]) for j in n]
    o = [xs[j][t:] + _dot(jnp.where(incl, xk[j][t:], 0.0), v_[j])
         - _dot(jnp.where(incl, xb[j][t:], 0.0), u[j]) for j in n]
    for j, (i, h) in enumerate(items):
        s_scr[i, h] = s[j] * elast_[j][:1] + _dot_tn(
            jnp.concatenate([v_[j], -u[j]], axis=0),
            jnp.concatenate([k_rest[j], b_rest[j]], axis=0))
    for j, (i, h) in enumerate(items):
        o_scr[i * t:(i + 1) * t, h * N_B:(h + 1) * N_B] = o[j]
    o_all = o_scr[...]
    inv_n = 1.0 / N_B
    cen = o_all - head_sum(o_all) * inv_n
    var = head_sum(cen * cen) * inv_n
    on = cen * lax.rsqrt(var + GN_EPS) * lnw_ref[...] + lnb_ref[...]
    out = (on + head_sum(r * k_mod * rk_ref[...]) * v) * g
    for i in range(n_seq):
        o_ref[i] = out[i * t:(i + 1) * t]

    @pl.when(c == pl.num_programs(1) - 1)
    def _():
        sfin_ref[...] = s_scr[...]


def _rwkv_params(p, d_b):
    row = lambda x: x.reshape(1, -1)
    return (row(p['mu']), row(p['w0']), p['w_up'], row(p['a0']), p['a_up'], p['g_up'],
            row(p['k_k']), row(p['k_a']), row(p['r_k']), row(p['ln_w']), row(p['ln_b']))


def rwkv_chunked(pb, shift0, s0, p, batch, seq):
    m, nb = pb.shape
    n_heads = s0.shape[1]
    d_b = n_heads * N_B
    t = CHUNK
    ns = SEQS_PER_STEP if batch % SEQS_PER_STEP == 0 else 1
    assert seq % t == 0 and t & (t - 1) == 0
    nc = seq // t
    head_sum = jnp.kron(jnp.eye(n_heads, dtype=F32), jnp.ones((N_B, N_B), F32)).astype(BF16)
    params = _rwkv_params(p, d_b) + (head_sum,)
    const = lambda b, c: (0, 0)
    o, s_fin = pl.pallas_call(
        functools.partial(_rwkv_chunk_kernel, n_heads=n_heads),
        grid=(batch // ns, nc),
        in_specs=[pl.BlockSpec((ns, t, nb), lambda b, c: (b, c, 0)),
                  pl.BlockSpec((ns, 1, nb), lambda b, c: (b, 0, 0)),
                  pl.BlockSpec((ns, n_heads, N_B, N_B), lambda b, c: (b, 0, 0, 0))]
                 + [pl.BlockSpec(x.shape, const) for x in params],
        out_specs=[pl.BlockSpec((ns, t, d_b), lambda b, c: (b, c, 0)),
                   pl.BlockSpec((ns, n_heads, N_B, N_B), lambda b, c: (b, 0, 0, 0))],
        out_shape=[jax.ShapeDtypeStruct((batch, seq, d_b), F32),
                   jax.ShapeDtypeStruct((batch, n_heads, N_B, N_B), F32)],
        scratch_shapes=[pltpu.VMEM((ns, SUBLANES, nb), F32), pltpu.VMEM((ns, n_heads, N_B, N_B), F32),
                        pltpu.VMEM((ns * t, d_b), F32)],
        compiler_params=_cparams("arbitrary", "arbitrary"),
        name="rwkv_chunked",
    )(pb.reshape(batch, seq, nb), shift0.reshape(batch, 1, nb), s0, *params)
    return o.reshape(m, d_b), s_fin


def _col_from_row(x_row, eye):
    return jnp.sum(eye * x_row, axis=-1, keepdims=True)


def _row_from_col(x_col, eye):
    return jnp.sum(eye * x_col, axis=0, keepdims=True)


def _rwkv_step_kernel(pb_ref, shift0_ref, s0_ref, mu_ref, w0_ref, wup_ref, a0_ref, aup_ref, gup_ref,
                      kk_ref, ka_ref, rk_ref, lnw_ref, lnb_ref, o_ref, s_ref, *, n_heads):
    d_b = n_heads * N_B
    r, k_mod, v, kk_raw, a, g, log_decay = _rwkv_tokens(
        pb_ref[0], shift0_ref[0], mu_ref[...], w0_ref[...], wup_ref[...], a0_ref[...], aup_ref[...],
        gup_ref[...], kk_ref[...], ka_ref[...], d_b)
    decay = jnp.exp(log_decay)
    eye = (_iota2((N_B, N_B), 0) == _iota2((N_B, N_B), 1)).astype(F32)
    for h in range(n_heads):
        lanes = slice(h * N_B, (h + 1) * N_B)
        kk = _l2(kk_raw[:, lanes])
        s = s0_ref[0, h]
        sa = jnp.sum(s * kk, axis=-1, keepdims=True)
        v_col = _col_from_row(v[:, lanes], eye)
        s = s * decay[:, lanes] - sa * (kk * a[:, lanes]) + v_col * k_mod[:, lanes]
        s_ref[0, h] = s
        o = _row_from_col(jnp.sum(s * r[:, lanes], axis=-1, keepdims=True), eye)
        o_ref[0, :, lanes] = _rwkv_finish(o, r[:, lanes], k_mod[:, lanes], v[:, lanes], g[:, lanes],
                                          rk_ref[:, lanes], lnw_ref[:, lanes], lnb_ref[:, lanes])


def rwkv_step(pb, shift0, s0, p):
    b, nb = pb.shape
    n_heads = s0.shape[1]
    d_b = n_heads * N_B
    params = _rwkv_params(p, d_b)
    o, s = pl.pallas_call(
        functools.partial(_rwkv_step_kernel, n_heads=n_heads),
        grid=(b,),
        in_specs=[pl.BlockSpec((1, 1, nb), lambda i: (i, 0, 0)),
                  pl.BlockSpec((1, 1, nb), lambda i: (i, 0, 0)),
                  pl.BlockSpec((1, n_heads, N_B, N_B), lambda i: (i, 0, 0, 0))]
                 + [pl.BlockSpec(x.shape, lambda i: (0, 0)) for x in params],
        out_specs=[pl.BlockSpec((1, 1, d_b), lambda i: (i, 0, 0)),
                   pl.BlockSpec((1, n_heads, N_B, N_B), lambda i: (i, 0, 0, 0))],
        out_shape=[jax.ShapeDtypeStruct((b, 1, d_b), F32),
                   jax.ShapeDtypeStruct((b, n_heads, N_B, N_B), F32)],
        compiler_params=_cparams("arbitrary"),
        name="rwkv_step",
    )(pb.reshape(b, 1, nb), shift0.reshape(b, 1, nb), s0, *params)
    return o.reshape(b, d_b), s


def _gdn_gates(gates, alog_row, dtb_row):
    beta = _sigmoid(gates)
    g = -jnp.exp(alog_row) * _softplus(gates + dtb_row)
    return beta, g


def _gdn_chunk_kernel(qkv_ref, z_ref, gates_ref, conv0_ref, s0_ref, cw_ref, alog_ref, dtb_ref, onorm_ref,
                      o_ref, sfin_ref, carry_scr, s_scr, *, n_heads):
    c = pl.program_id(1)
    n_seq, t, _ = qkv_ref.shape
    dqk = n_heads * DK_C

    @pl.when(c == 0)
    def _():
        carry_scr[:, 0:SUBLANES, :] = jnp.zeros((n_seq, SUBLANES, carry_scr.shape[2]), F32)
        carry_scr[:, SUBLANES - (DN_CONV_W - 1):SUBLANES, :] = conv0_ref[...]
        s_scr[...] = s0_ref[...]

    ys = []
    for i in range(n_seq):
        x = qkv_ref[i]
        carry_scr[i, SUBLANES:SUBLANES + t, :] = x
        yi = cw_ref[DN_CONV_W - 1:DN_CONV_W, :] * x
        for d in range(1, DN_CONV_W):
            yi = yi + (cw_ref[DN_CONV_W - 1 - d:DN_CONV_W - d, :]
                       * carry_scr[i, SUBLANES - d:SUBLANES - d + t, :])
        carry_scr[i, 0:SUBLANES, :] = x[t - SUBLANES:t]
        ys.append(yi)
    y = _silu(jnp.concatenate(ys, axis=0))

    row = _iota2((t, t), 0)
    col = _iota2((t, t), 1)
    incl = row >= col
    strict = row > col
    rows = n_seq * t
    srow = _iota2((rows, rows), 0)
    scol = _iota2((rows, rows), 1)
    lg_t = t.bit_length() - 1
    same_seq = (srow >> lg_t) == (scol >> lg_t)
    gates = jnp.concatenate([gates_ref[i] for i in range(n_seq)], axis=0)
    beta_all, g_all = _gdn_gates(gates, alog_ref[...], dtb_ref[...])
    gcum_all = _dot_mask_lhs((same_seq & (srow >= scol)).astype(BF16), g_all)
    g3 = _split3(gcum_all)
    lane = _iota2((t, LANES), 1)
    z = jnp.concatenate([z_ref[i] for i in range(n_seq)], axis=0)

    items = [(i, h) for i in range(n_seq) for h in range(n_heads)]
    n = range(len(items))
    rs = lambda i: slice(i * t, (i + 1) * t)
    q = [_l2(y[rs(i), h * DK_C:(h + 1) * DK_C]) * (DK_C ** -0.5) for i, h in items]
    k = [_l2(y[rs(i), dqk + h * DK_C:dqk + (h + 1) * DK_C]) for i, h in items]
    v = [y[rs(i), 2 * dqk + h * DV_C:2 * dqk + (h + 1) * DV_C] for i, h in items]
    beta = [beta_all[rs(i), h:h + 1] for i, h in items]
    gc = [gcum_all[rs(i), n_heads + h:n_heads + h + 1] for i, h in items]
    egc = [jnp.exp(x) for x in gc]
    nt = lambda a, b: lax.dot_general(a, b, (((1,), (1,)), ((), ())), preferred_element_type=F32)
    gc_row = []
    for i, h in items:
        pk = (lane == n_heads + h).astype(BF16)
        gc_row.append(nt(pk, g3[0][rs(i)]) + (nt(pk, g3[1][rs(i)]) + nt(pk, g3[2][rs(i)])))
    lmask = [jnp.where(incl, jnp.exp(jnp.where(incl, gc[j] - gc_row[j], 0.0)), 0.0) for j in n]
    kb = [k[j] * beta[j] for j in n]
    kq = [_dot_nt(jnp.concatenate([kb[j], q[j]], axis=0), k[j]) for j in n]
    tinv = _tri_inv_multi([jnp.where(strict, kq[j][:t] * lmask[j], 0.0) for j in n])
    uw = [_dot(tinv[j], jnp.concatenate([v[j] * beta[j], kb[j] * egc[j]], axis=1)) for j in n]
    s = [s_scr[i, h] for i, h in items]
    ws = [_dot(jnp.concatenate([uw[j][:, DV_C:], q[j] * egc[j]], axis=0), s[j]) for j in n]
    v_new = [uw[j][:, :DV_C] - ws[j][:t] for j in n]
    o = [ws[j][t:] + _dot(jnp.where(incl, kq[j][t:] * lmask[j], 0.0), v_new[j]) for j in n]
    for j, (i, h) in enumerate(items):
        g_last = gc[j][t - 1:t, :]
        s_scr[i, h] = s[j] * jnp.exp(g_last) + _dot_tn(k[j] * jnp.exp(g_last - gc[j]), v_new[j])
    for j, (i, h) in enumerate(items):
        o_ref[i, :, h * DV_C:(h + 1) * DV_C] = (_rms(o[j], onorm_ref[...])
                                                * _silu(z[rs(i), h * DV_C:(h + 1) * DV_C]))

    @pl.when(c == pl.num_programs(1) - 1)
    def _():
        sfin_ref[...] = s_scr[...]


def _gdn_gate_params(a_log, dt_bias, n_heads):
    pad = lambda x: jnp.zeros((1, LANES), F32).at[0, n_heads:2 * n_heads].set(x)
    return pad(a_log), pad(dt_bias)


def gdn_chunked(qkvz, gates, conv0, s0, cw, a_log, dt_bias, onorm, batch, seq):
    m = qkvz.shape[0]
    n_heads = s0.shape[1]
    dconv = cw.shape[1]
    dvv = n_heads * DV_C
    assert qkvz.shape[1] == dconv + dvv and dconv % dvv == 0
    t = CHUNK
    ns = SEQS_PER_STEP if batch % SEQS_PER_STEP == 0 else 1
    assert seq % t == 0 and t & (t - 1) == 0
    nc = seq // t
    alog_row, dtb_row = _gdn_gate_params(a_log, dt_bias, n_heads)
    const = lambda b, c: (0, 0)
    qkvz3 = qkvz.reshape(batch, seq, dconv + dvv)
    o, s_fin = pl.pallas_call(
        functools.partial(_gdn_chunk_kernel, n_heads=n_heads),
        grid=(batch // ns, nc),
        in_specs=[pl.BlockSpec((ns, t, dconv), lambda b, c: (b, c, 0)),
                  pl.BlockSpec((ns, t, dvv), lambda b, c: (b, c, dconv // dvv)),
                  pl.BlockSpec((ns, t, LANES), lambda b, c: (b, c, 0)),
                  pl.BlockSpec((ns, DN_CONV_W - 1, dconv), lambda b, c: (b, 0, 0)),
                  pl.BlockSpec((ns, n_heads, DK_C, DV_C), lambda b, c: (b, 0, 0, 0)),
                  pl.BlockSpec((DN_CONV_W, dconv), const),
                  pl.BlockSpec((1, LANES), const),
                  pl.BlockSpec((1, LANES), const),
                  pl.BlockSpec((1, DV_C), const)],
        out_specs=[pl.BlockSpec((ns, t, dvv), lambda b, c: (b, c, 0)),
                   pl.BlockSpec((ns, n_heads, DK_C, DV_C), lambda b, c: (b, 0, 0, 0))],
        out_shape=[jax.ShapeDtypeStruct((batch, seq, dvv), F32),
                   jax.ShapeDtypeStruct((batch, n_heads, DK_C, DV_C), F32)],
        scratch_shapes=[pltpu.VMEM((ns, SUBLANES + t, dconv), F32),
                        pltpu.VMEM((ns, n_heads, DK_C, DV_C), F32)],
        compiler_params=_cparams("arbitrary", "arbitrary"),
        name="gdn_chunked",
    )(qkvz3, qkvz3, gates.reshape(batch, seq, LANES), conv0, s0, cw, alog_row, dtb_row,
      onorm.reshape(1, DV_C))
    return o.reshape(m, dvv), s_fin


def _gdn_step_kernel(qkv_ref, z_ref, gates_ref, conv0_ref, s0_ref, cw_ref, alog_ref, dtb_ref, onorm_ref,
                     o_ref, s_ref, *, n_heads):
    dqk = n_heads * DK_C
    x = qkv_ref[0]
    y = cw_ref[DN_CONV_W - 1:DN_CONV_W, :] * x
    for i in range(DN_CONV_W - 1):
        y = y + cw_ref[i:i + 1, :] * conv0_ref[0, i:i + 1, :]
    y = _silu(y)
    beta_all, g_all = _gdn_gates(gates_ref[0], alog_ref[...], dtb_ref[...])
    z = z_ref[0]
    eye = (_iota2((DK_C, DK_C), 0) == _iota2((DK_C, DK_C), 1)).astype(F32)
    for h in range(n_heads):
        q = _l2(y[:, h * DK_C:(h + 1) * DK_C]) * (DK_C ** -0.5)
        k = _l2(y[:, dqk + h * DK_C:dqk + (h + 1) * DK_C])
        v = y[:, 2 * dqk + h * DV_C:2 * dqk + (h + 1) * DV_C]
        beta = beta_all[:, h:h + 1]
        decay = jnp.exp(g_all[:, n_heads + h:n_heads + h + 1])
        s = s0_ref[0, h]
        k_col = _col_from_row(k, eye)
        q_col = _col_from_row(q, eye)
        v_new = beta * v - jnp.sum((k_col * (beta * decay)) * s, axis=0, keepdims=True)
        o = (jnp.sum((q_col * decay) * s, axis=0, keepdims=True)
             + jnp.sum(q * k, axis=-1, keepdims=True) * v_new)
        s_ref[0, h] = s * decay + k_col * v_new
        o_ref[0, :, h * DV_C:(h + 1) * DV_C] = (_rms(o, onorm_ref[...])
                                                * _silu(z[:, h * DV_C:(h + 1) * DV_C]))


def gdn_step(qkvz, gates, conv0, s0, cw, a_log, dt_bias, onorm):
    b = qkvz.shape[0]
    n_heads = s0.shape[1]
    dconv = cw.shape[1]
    dvv = n_heads * DV_C
    alog_row, dtb_row = _gdn_gate_params(a_log, dt_bias, n_heads)
    qkvz3 = qkvz.reshape(b, 1, dconv + dvv)
    const = lambda i: (0, 0)
    o, s = pl.pallas_call(
        functools.partial(_gdn_step_kernel, n_heads=n_heads),
        grid=(b,),
        in_specs=[pl.BlockSpec((1, 1, dconv), lambda i: (i, 0, 0)),
                  pl.BlockSpec((1, 1, dvv), lambda i: (i, 0, dconv // dvv)),
                  pl.BlockSpec((1, 1, LANES), lambda i: (i, 0, 0)),
                  pl.BlockSpec((1, DN_CONV_W - 1, dconv), lambda i: (i, 0, 0)),
                  pl.BlockSpec((1, n_heads, DK_C, DV_C), lambda i: (i, 0, 0, 0)),
                  pl.BlockSpec((DN_CONV_W, dconv), const),
                  pl.BlockSpec((1, LANES), const),
                  pl.BlockSpec((1, LANES), const),
                  pl.BlockSpec((1, DV_C), const)],
        out_specs=[pl.BlockSpec((1, 1, dvv), lambda i: (i, 0, 0)),
                   pl.BlockSpec((1, n_heads, DK_C, DV_C), lambda i: (i, 0, 0, 0))],
        out_shape=[jax.ShapeDtypeStruct((b, 1, dvv), F32),
                   jax.ShapeDtypeStruct((b, n_heads, DK_C, DV_C), F32)],
        compiler_params=_cparams("arbitrary"),
        name="gdn_step",
    )(qkvz3, qkvz3, gates.reshape(b, 1, LANES), conv0, s0, cw, alog_row, dtb_row,
      onorm.reshape(1, DV_C))
    return o.reshape(b, dvv), s


def _row_tile(m, want):
    return want if m % want == 0 else m


def _trunk(x, kv, wkv0, shift0, dn0, dnconv0, ffnconv0, p):
    b, t, d = x.shape
    m = b * t
    depth = p['norm_ffn'].shape[0]
    d_a = p['sb_bias'].shape[1] * DH_A
    nb = p['mu_b'].shape[1]
    dconv = p['conv_dn'].shape[2]
    n_heads_c = p['a_log_dn'].shape[1]
    dvv = n_heads_c * DV_C
    tm = _row_tile(m, PROJ_ROWS)
    tm_out = _row_tile(m, OUT_ROWS)
    new_k, new_v, new_wkv, new_shift, new_dn, new_dnconv, new_ffn = [], [], [], [], [], [], []
    x2 = x.reshape(m, d)
    for l in range(depth):
        i = l // 2
        if l % 2 == 0:
            qkv, pb = norm_matmul(x2, p['norm_mix_even'][i], p['w_in_even'][i], (3 * d_a, nb), tm)
            rp = dict(mu=p['mu_b'][i], w0=p['w0_b'][i], w_up=p['w_up_b'][i], a0=p['a0_b'][i],
                      a_up=p['a_up_b'][i], g_up=p['g_up_b'][i], k_k=p['kk_b'][i], k_a=p['ka_b'][i],
                      r_k=p['rk_b'][i], ln_w=p['lnx_w_b'][i], ln_b=p['lnx_b_b'][i])
            if kv is None:
                o_a, kn_t, v_t = sb_prompt(qkv, p['qnorm_a'][i], p['knorm_a'][i], p['sb_bias'][i], b, t)
                o_b, wkv = rwkv_chunked(pb, shift0[i], wkv0[i], rp, b, t)
                shift = pb.reshape(b, t, nb)[:, t - 1]
                new_k.append(jnp.transpose(kn_t, (0, 3, 1, 2)))
                new_v.append(jnp.transpose(v_t, (0, 3, 1, 2)))
            else:
                o_a, kn = sb_decode(qkv, kv[0], kv[1], kv[2], i, p['qnorm_a'][i], p['knorm_a'][i],
                                    p['sb_bias'][i])
                o_b, wkv = rwkv_step(pb, shift0[i], wkv0[i], rp)
                shift = pb
                new_k.append(kn.reshape(b, t, d_a // DH_A, DH_A))
                new_v.append(qkv[:, 2 * d_a:].reshape(b, t, d_a // DH_A, DH_A))
            w_out = p['w_out_even'][i]
            mix, w_mix = [o_a, o_b], [w_out[:d_a], w_out[d_a:]]
            new_wkv.append(wkv)
            new_shift.append(shift)
        else:
            qkvz, gates = norm_matmul(x2, p['norm_mix_odd'][i], p['w_in_odd'][i], (dconv + dvv, LANES), tm)
            if kv is None:
                o_c, s_fin = gdn_chunked(qkvz, gates, dnconv0[i], dn0[i], p['conv_dn'][i], p['a_log_dn'][i],
                                         p['dt_bias_dn'][i], p['onorm_dn'][i], b, t)
                cbuf = qkvz.reshape(b, t, dconv + dvv)[:, t - (DN_CONV_W - 1):, :dconv]
            else:
                o_c, s_fin = gdn_step(qkvz, gates, dnconv0[i], dn0[i], p['conv_dn'][i], p['a_log_dn'][i],
                                      p['dt_bias_dn'][i], p['onorm_dn'][i])
                cbuf = jnp.concatenate([dnconv0[i][:, 1:], qkvz[:, None, :dconv]], axis=1)
            mix, w_mix = [o_c], [p['w_out_odd'][i]]
            new_dn.append(s_fin)
            new_dnconv.append(cbuf)
        ffn_args = (p['norm_ffn'][l], p['w_gate'][l], p['w_up'][l], p['conv_ffn'][l], p['conv_ffn_b'][l],
                    p['w_down'][l])
        if kv is None:
            x3, fbuf = conv_ffn([o.reshape(b, t, -1) for o in mix], w_mix, x2.reshape(b, t, d),
                                ffnconv0[l], *ffn_args, tm=_row_tile(t, FFN_ROWS))
            x2 = x3.reshape(m, d)
        else:
            x2 = matmul_res(mix, w_mix, x2, tm_out)
            x2, fbuf = conv_ffn_step(x2, ffnconv0[l], *ffn_args)
        new_ffn.append(fbuf)
    states = tuple(jnp.stack(s) for s in (new_k, new_v, new_wkv, new_shift, new_dn, new_dnconv, new_ffn))
    return x2.reshape(b, t, d), states


def kernel(x_prompt, x_sample, cache_k, cache_v, page_table, state_wkv, state_shift, state_dn,
           state_dn_conv, state_ffn_conv, norm_mix_even, w_in_even, qnorm_a, knorm_a, sb_bias,
           mu_b, w0_b, w_up_b, a0_b, a_up_b, g_up_b, kk_b, ka_b, rk_b, lnx_w_b, lnx_b_b,
           w_out_even, norm_mix_odd, w_in_odd, conv_dn, a_log_dn, dt_bias_dn, onorm_dn, w_out_odd,
           norm_ffn, w_gate, w_up, conv_ffn, conv_ffn_b, w_down):
    n_heads_c = a_log_dn.shape[1]
    n_gate_cols = 2 * n_heads_c
    assert x_sample.shape[1] == 1 and n_gate_cols <= LANES
    w_in_odd_p = jnp.pad(w_in_odd, ((0, 0), (0, 0), (0, LANES - n_gate_cols)))
    bf = lambda w: w.astype(BF16)
    p = dict(norm_mix_even=norm_mix_even, w_in_even=bf(w_in_even), qnorm_a=qnorm_a, knorm_a=knorm_a,
             sb_bias=sb_bias, mu_b=mu_b, w0_b=w0_b, w_up_b=bf(w_up_b), a0_b=a0_b, a_up_b=bf(a_up_b),
             g_up_b=bf(g_up_b), kk_b=kk_b, ka_b=ka_b, rk_b=rk_b, lnx_w_b=lnx_w_b, lnx_b_b=lnx_b_b,
             w_out_even=bf(w_out_even), norm_mix_odd=norm_mix_odd, w_in_odd=bf(w_in_odd_p),
             conv_dn=conv_dn, a_log_dn=a_log_dn, dt_bias_dn=dt_bias_dn, onorm_dn=onorm_dn,
             w_out_odd=bf(w_out_odd), norm_ffn=norm_ffn, w_gate=bf(w_gate), w_up=bf(w_up),
             conv_ffn=conv_ffn, conv_ffn_b=conv_ffn_b, w_down=bf(w_down))
    b = x_prompt.shape[0]
    n_ab = state_wkv.shape[0]
    n_c = state_dn.shape[0]
    depth = norm_ffn.shape[0]
    zeros = lambda like, lead: jnp.zeros((lead, b) + like.shape[2:], F32)
    y_prompt, ps = _trunk(x_prompt, None, zeros(state_wkv, n_ab), zeros(state_shift, n_ab),
                          zeros(state_dn, n_c), zeros(state_dn_conv, n_c), zeros(state_ffn_conv, depth), p)
    y_sample, ss = _trunk(x_sample, (cache_k, cache_v, page_table), state_wkv, state_shift, state_dn,
                          state_dn_conv, state_ffn_conv, p)
    return (y_prompt, y_sample, ps[0], ps[1], ss[0], ss[1], ps[2], ss[2], ps[3], ss[3],
            ps[4], ss[4], ps[5], ss[5], ps[6], ss[6])
```

```python
import functools

import jax
import jax.numpy as jnp
from jax import lax
from jax.experimental import pallas as pl
from jax.experimental.pallas import tpu as pltpu

F32 = jnp.float32
BF16 = jnp.bfloat16

RMS_EPS = 1e-6
L2_EPS = 1e-6
GN_EPS = 64e-5

DH_A = 64
N_B = 64
R_W, R_A, R_G = 64, 64, 128
DK_C = 128
DV_C = 128
DN_CONV_W = 4
FFN_CONV_W = 3

SB_BLOCK = 128
SB_GROUP = 3
CHUNK = 64
SEQS_PER_STEP = 4
SUBLANES = 8
LANES = 128
VMEM_LIMIT_BYTES = 56 * 1024 * 1024
PROJ_ROWS = 512
OUT_ROWS = 1024
FFN_ROWS = 512
FFN_COLS = 256


def _cparams(*sem):
    return pltpu.CompilerParams(dimension_semantics=sem, vmem_limit_bytes=VMEM_LIMIT_BYTES)


def _dot(a, b):
    return jnp.dot(a.astype(BF16), b.astype(BF16), preferred_element_type=F32)


def _dot_nt(a, b):
    return lax.dot_general(a.astype(BF16), b.astype(BF16), (((1,), (1,)), ((), ())),
                           preferred_element_type=F32)


def _dot_tn(a, b):
    return lax.dot_general(a.astype(BF16), b.astype(BF16), (((0,), (0,)), ((), ())),
                           preferred_element_type=F32)


def _split2(a):
    hi = a.astype(BF16)
    lo = (a - hi.astype(F32)).astype(BF16)
    return hi, lo


def _split3(a):
    hi = a.astype(BF16)
    r1 = a - hi.astype(F32)
    mid = r1.astype(BF16)
    lo = (r1 - mid.astype(F32)).astype(BF16)
    return hi, mid, lo


def _dot_mask_lhs(mask_bf16, x):
    d = functools.partial(jnp.dot, preferred_element_type=F32)
    h, m, l = _split3(x)
    return d(mask_bf16, h) + (d(mask_bf16, m) + d(mask_bf16, l))


def _softplus(x):
    return jnp.maximum(x, 0.0) + jnp.log(1.0 + jnp.exp(-jnp.abs(x)))


def _sigmoid(x):
    return 1.0 / (1.0 + jnp.exp(-x))


def _silu(x):
    return x * _sigmoid(x)


def _rms(x, g):
    return x * lax.rsqrt(jnp.mean(x * x, axis=-1, keepdims=True) + RMS_EPS) * g


def _l2(x):
    return x * lax.rsqrt(jnp.sum(x * x, axis=-1, keepdims=True) + L2_EPS)


def _iota2(shape, dim):
    return lax.broadcasted_iota(jnp.int32, shape, dim)


def _shift_rows(cur, carry, d):
    rolled = pltpu.roll(cur, d, 0)
    crolled = pltpu.roll(carry, d, 0)
    first = jnp.where(_iota2(carry.shape, 0) < d, crolled, rolled[:SUBLANES])
    if cur.shape[0] == SUBLANES:
        return first
    return jnp.concatenate([first, rolled[SUBLANES:]], axis=0)


def _tri_inv_multi(a_list):
    n = a_list[0].shape[0]
    row = _iota2((n, n), 0)
    col = _iota2((n, n), 1)
    eye = (row == col).astype(F32)
    d = functools.partial(jnp.dot, preferred_element_type=F32)
    ts = [eye - jnp.where((row >> 1) == (col >> 1), a, 0.0) for a in a_list]
    s = 2
    while s < n:
        sh = s.bit_length() - 1
        sub = ((row >> (sh + 1)) == (col >> (sh + 1))) & ((row >> sh) != (col >> sh))
        tb = [t.astype(BF16) for t in ts]
        tes = [d(t, jnp.where(sub, a, 0.0).astype(BF16)).astype(BF16) for t, a in zip(tb, a_list)]
        ts = [t - d(te, t16) for t, te, t16 in zip(ts, tes, tb)]
        s *= 2
    return ts


def _norm_matmul_kernel(x_ref, g_ref, w_ref, *o_refs, splits):
    h = _rms(x_ref[...], g_ref[...]).astype(BF16)
    y = jnp.dot(h, w_ref[...], preferred_element_type=F32)
    off = 0
    for o_ref, n in zip(o_refs, splits):
        o_ref[...] = y[:, off:off + n]
        off += n


def norm_matmul(x, g, w, splits, tm):
    m, k = x.shape
    n = w.shape[1]
    assert sum(splits) == n and m % tm == 0
    return pl.pallas_call(
        functools.partial(_norm_matmul_kernel, splits=tuple(splits)),
        grid=(m // tm,),
        in_specs=[pl.BlockSpec((tm, k), lambda i: (i, 0)),
                  pl.BlockSpec((1, k), lambda i: (0, 0)),
                  pl.BlockSpec((k, n), lambda i: (0, 0))],
        out_specs=[pl.BlockSpec((tm, s), lambda i: (i, 0)) for s in splits],
        out_shape=[jax.ShapeDtypeStruct((m, s), F32) for s in splits],
        compiler_params=_cparams("arbitrary"),
        name="norm_matmul",
    )(x, g.reshape(1, k), w)


def _matmul_res_kernel(*refs, n_in):
    a_refs = refs[:n_in]
    w_refs = refs[n_in:2 * n_in]
    res_ref = refs[2 * n_in]
    o_ref = refs[2 * n_in + 1]
    acc = res_ref[...]
    for a_ref, w_ref in zip(a_refs, w_refs):
        acc = acc + jnp.dot(a_ref[...].astype(BF16), w_ref[...], preferred_element_type=F32)
    o_ref[...] = acc


def matmul_res(a_list, w_list, res, tm):
    m, n = res.shape
    assert m % tm == 0
    n_in = len(a_list)
    in_specs = ([pl.BlockSpec((tm, a.shape[1]), lambda i: (i, 0)) for a in a_list]
                + [pl.BlockSpec(w.shape, lambda i: (0, 0)) for w in w_list]
                + [pl.BlockSpec((tm, n), lambda i: (i, 0))])
    return pl.pallas_call(
        functools.partial(_matmul_res_kernel, n_in=n_in),
        grid=(m // tm,),
        in_specs=in_specs,
        out_specs=pl.BlockSpec((tm, n), lambda i: (i, 0)),
        out_shape=jax.ShapeDtypeStruct((m, n), F32),
        compiler_params=_cparams("arbitrary"),
        name="matmul_res",
    )(*a_list, *w_list, res)


def _ffn_kernel(*refs, n_mix):
    mix_refs = refs[:n_mix]
    wmix_refs = refs[n_mix:2 * n_mix]
    (x_ref, buf_ref, g_ref, wg_ref, wu_ref, cw_ref, cb_ref, wd_ref,
     y_ref, tail_ref, carry_scr, act_scr) = refs[2 * n_mix:]
    ti = pl.program_id(1)
    x = x_ref[0]
    for o_ref, w_ref in zip(mix_refs, wmix_refs):
        x = x + jnp.dot(o_ref[0].astype(BF16), w_ref[...], preferred_element_type=F32)
    tm = x.shape[0]
    f = wg_ref.shape[1]
    h = _rms(x, g_ref[...]).astype(BF16)

    @pl.when(ti == 0)
    def _():
        carry_scr[...] = jnp.zeros_like(carry_scr)
        carry_scr[SUBLANES - 2:SUBLANES, :] = buf_ref[0]

    for c0 in range(0, f, FFN_COLS):
        cols = slice(c0, min(c0 + FFN_COLS, f))
        gate = jnp.dot(h, wg_ref[:, cols], preferred_element_type=F32)
        up = jnp.dot(h, wu_ref[:, cols], preferred_element_type=F32)
        carry = carry_scr[:, cols]
        conv = (cw_ref[2:3, cols] * gate + cw_ref[1:2, cols] * _shift_rows(gate, carry, 1)
                + cw_ref[0:1, cols] * _shift_rows(gate, carry, 2))
        carry_scr[:, cols] = gate[tm - SUBLANES:tm]
        tail_ref[0, :, cols] = gate[tm - 2:tm]
        act_scr[:, cols] = (_silu(conv + cb_ref[:, cols]) * up).astype(BF16)
    y_ref[0] = x + jnp.dot(act_scr[...], wd_ref[...], preferred_element_type=F32)


def conv_ffn(mix, w_mix, x, buf, g, wg, wu, cw, cb, wd, tm):
    b, t, d = x.shape
    f = wg.shape[1]
    assert t % tm == 0 and tm % SUBLANES == 0
    const = lambda bi, ti: (0, 0)
    return pl.pallas_call(
        functools.partial(_ffn_kernel, n_mix=len(mix)),
        grid=(b, t // tm),
        in_specs=[pl.BlockSpec((1, tm, o.shape[2]), lambda bi, ti: (bi, ti, 0)) for o in mix]
                 + [pl.BlockSpec(w.shape, const) for w in w_mix]
                 + [pl.BlockSpec((1, tm, d), lambda bi, ti: (bi, ti, 0)),
                  pl.BlockSpec((1, 2, f), lambda bi, ti: (bi, 0, 0)),
                  pl.BlockSpec((1, d), const),
                  pl.BlockSpec((d, f), const),
                  pl.BlockSpec((d, f), const),
                  pl.BlockSpec((FFN_CONV_W, f), const),
                  pl.BlockSpec((1, f), const),
                  pl.BlockSpec((f, d), const)],
        out_specs=[pl.BlockSpec((1, tm, d), lambda bi, ti: (bi, ti, 0)),
                   pl.BlockSpec((1, 2, f), lambda bi, ti: (bi, 0, 0))],
        out_shape=[jax.ShapeDtypeStruct((b, t, d), F32),
                   jax.ShapeDtypeStruct((b, 2, f), F32)],
        scratch_shapes=[pltpu.VMEM((SUBLANES, f), F32), pltpu.VMEM((tm, f), BF16)],
        compiler_params=_cparams("arbitrary", "arbitrary"),
        name="conv_ffn",
    )(*mix, *w_mix, x, buf, g.reshape(1, d), wg, wu, cw, cb.reshape(1, f), wd)


def _ffn_step_kernel(x_ref, buf0_ref, buf1_ref, g_ref, wg_ref, wu_ref, cw_ref, cb_ref, wd_ref,
                     y_ref, gate_ref):
    x = x_ref[...]
    h = _rms(x, g_ref[...]).astype(BF16)
    gate = jnp.dot(h, wg_ref[...], preferred_element_type=F32)
    up = jnp.dot(h, wu_ref[...], preferred_element_type=F32)
    conv = cw_ref[2:3, :] * gate + cw_ref[1:2, :] * buf1_ref[...] + cw_ref[0:1, :] * buf0_ref[...]
    act = (_silu(conv + cb_ref[...]) * up).astype(BF16)
    y_ref[...] = x + jnp.dot(act, wd_ref[...], preferred_element_type=F32)
    gate_ref[...] = gate


def conv_ffn_step(x, buf, g, wg, wu, cw, cb, wd):
    b, d = x.shape
    f = wg.shape[1]
    full = lambda shape: pl.BlockSpec(shape, lambda i: (0,) * len(shape))
    y, gate = pl.pallas_call(
        _ffn_step_kernel,
        grid=(1,),
        in_specs=[full((b, d)), full((b, f)), full((b, f)), full((1, d)), full((d, f)), full((d, f)),
                  full((FFN_CONV_W, f)), full((1, f)), full((f, d))],
        out_specs=[full((b, d)), full((b, f))],
        out_shape=[jax.ShapeDtypeStruct((b, d), F32), jax.ShapeDtypeStruct((b, f), F32)],
        compiler_params=_cparams("arbitrary"),
        name="conv_ffn_step",
    )(x, buf[:, 0], buf[:, 1], g.reshape(1, d), wg, wu, cw, cb.reshape(1, f), wd)
    return y, jnp.stack([buf[:, 1], gate], axis=1)


def _pair_rms(x, w_row):
    left = _iota2(x.shape, 1) < DH_A
    x2 = x * x
    ms_l = jnp.sum(jnp.where(left, x2, 0.0), axis=-1, keepdims=True) * (1.0 / DH_A)
    ms_r = jnp.sum(jnp.where(left, 0.0, x2), axis=-1, keepdims=True) * (1.0 / DH_A)
    inv = jnp.where(left, lax.rsqrt(ms_l + RMS_EPS), lax.rsqrt(ms_r + RMS_EPS))
    return x * inv * w_row


def _sb_prompt_kernel(bias_ref, q_ref, k_ref, v_ref, qw_ref, kw_ref, o_ref, knt_ref, vt_ref,
                      kn_scr, v_scr, q_scr, later_scr, o_scr, tail_scr, *, n_heads):
    qi = pl.program_id(1)
    blk = q_ref.shape[0]
    n_pairs = n_heads // 2
    pw = 2 * DH_A
    scale = DH_A ** -0.5
    row0 = pl.multiple_of(qi * blk, blk)

    k = k_ref[...]
    v = v_ref[...]
    kn = jnp.concatenate([_pair_rms(k[:, p * pw:(p + 1) * pw], kw_ref[...]) for p in range(n_pairs)],
                         axis=-1)
    for h in range(n_heads):
        knt_ref[0, h] = kn[:, h * DH_A:(h + 1) * DH_A].T
        vt_ref[0, h] = v[:, h * DH_A:(h + 1) * DH_A].T
    kn_scr[pl.ds(row0, blk), :] = kn.astype(BF16)
    v_scr[pl.ds(row0, blk), :] = v.astype(BF16)

    q = q_ref[...]
    left = _iota2((blk, pw), 1) < DH_A
    for p in range(n_pairs):
        qn = _pair_rms(q[:, p * pw:(p + 1) * pw], qw_ref[...]) * scale
        q_scr[p] = jnp.concatenate([jnp.where(left, qn, 0.0), jnp.where(left, 0.0, qn)],
                                   axis=0).astype(BF16)
    krow = _iota2((2 * blk, 2 * blk), 0)
    krow = jnp.where(krow >= blk, krow - blk, krow)
    kcol = _iota2((2 * blk, 2 * blk), 1)
    later_scr[...] = jnp.where((kcol >= blk) | (krow > kcol), 1.0, 0.0).astype(BF16)
    o_scr[...] = jnp.zeros_like(o_scr)
    tail_scr[...] = jnp.zeros_like(tail_scr)

    qrow = _iota2((2 * blk, blk), 0)
    causal = _iota2((2 * blk, blk), 1) < jnp.where(qrow >= blk, qrow - blk, qrow)
    vleft = _iota2((blk, pw), 1) < DH_A

    def blocks(offs, diag):
        items = [(j, p) for j in range(len(offs)) for p in range(n_pairs)]
        zs = []
        for j, p in items:
            z2 = lax.dot_general(q_scr[p], kn_scr[pl.ds(offs[j], blk), p * pw:(p + 1) * pw],
                                 (((1,), (1,)), ((), ())), preferred_element_type=F32)
            zs.append(jnp.concatenate([z2[:blk] + bias_ref[2 * p], z2[blk:] + bias_ref[2 * p + 1]],
                                      axis=0))
        sps = [_softplus(z) for z in zs]
        if diag:
            sps = [jnp.where(causal, sp, 0.0) if n < n_pairs else sp for n, sp in enumerate(sps)]
        his = [sp.astype(BF16) for sp in sps]
        los = [(sp - hi.astype(F32)).astype(BF16) for sp, hi in zip(sps, his)]
        lo_mat = later_scr[...]
        cums = [jnp.dot(jnp.concatenate([hi, lo], axis=1), lo_mat, preferred_element_type=F32)
                for hi, lo in zip(his, los)]
        ws = []
        for p in range(n_pairs):
            tail = tail_scr[p]
            for j in range(len(offs)):
                n = j * n_pairs + p
                w = jnp.exp((zs[n] - sps[n]) - (cums[n][:, :blk] + tail))
                if diag and j == 0:
                    w = jnp.where(causal, w, 0.0)
                tail = tail + cums[n][:, blk:]
                ws.append(w.astype(BF16))
            tail_scr[p] = tail
        for p in range(n_pairs):
            acc = o_scr[:, p * pw:(p + 1) * pw]
            for j in range(len(offs)):
                w = ws[p * len(offs) + j]
                vj = v_scr[pl.ds(offs[j], blk), p * pw:(p + 1) * pw]
                zero = jnp.zeros_like(vj)
                v_bd = jnp.concatenate([jnp.where(vleft, vj, zero), jnp.where(vleft, zero, vj)], axis=0)
                acc = acc + jnp.dot(jnp.concatenate([w[:blk], w[blk:]], axis=1), v_bd,
                                    preferred_element_type=F32)
            o_scr[:, p * pw:(p + 1) * pw] = acc

    @pl.when(qi == 0)
    def _():
        blocks([row0], True)

    @pl.when(qi > 0)
    def _():
        blocks([row0, pl.multiple_of(row0 - blk, blk)], True)

    n_rest = jnp.maximum(qi - 1, 0)

    def body(i, carry):
        hi_off = pl.multiple_of((qi - 2 - SB_GROUP * i) * blk, blk)
        blocks([pl.multiple_of(hi_off - g * blk, blk) for g in range(SB_GROUP)], False)
        return carry

    lax.fori_loop(0, n_rest // SB_GROUP, body, 0)
    for left in range(1, SB_GROUP):
        @pl.when(n_rest % SB_GROUP == left)
        def _(left=left):
            blocks([(left - 1 - g) * blk for g in range(left)], False)

    o_ref[...] = o_scr[...]


def sb_prompt(qkv, qw, kw, bias, batch, seq):
    m, n3 = qkv.shape
    d_a = n3 // 3
    n_heads = d_a // DH_A
    blk = SB_BLOCK
    assert seq % blk == 0 and n_heads % 2 == 0 and 2 * DH_A == LANES
    nq = seq // blk
    rowblk = lambda col: (lambda b, qi: (b * nq + qi, col))
    pair_w = lambda w: jnp.tile(w, 2).reshape(1, 2 * DH_A)
    head_t = pl.BlockSpec((1, n_heads, DH_A, blk), lambda b, qi: (b, 0, 0, qi))
    head_t_shape = jax.ShapeDtypeStruct((batch, n_heads, DH_A, seq), F32)
    return pl.pallas_call(
        functools.partial(_sb_prompt_kernel, n_heads=n_heads),
        grid=(batch, nq),
        in_specs=[pl.BlockSpec(memory_space=pltpu.SMEM),
                  pl.BlockSpec((blk, d_a), rowblk(0)),
                  pl.BlockSpec((blk, d_a), rowblk(1)),
                  pl.BlockSpec((blk, d_a), rowblk(2)),
                  pl.BlockSpec((1, 2 * DH_A), lambda b, qi: (0, 0)),
                  pl.BlockSpec((1, 2 * DH_A), lambda b, qi: (0, 0))],
        out_specs=[pl.BlockSpec((blk, d_a), rowblk(0)), head_t, head_t],
        out_shape=[jax.ShapeDtypeStruct((m, d_a), F32), head_t_shape, head_t_shape],
        scratch_shapes=[pltpu.VMEM((seq, d_a), BF16), pltpu.VMEM((seq, d_a), BF16),
                        pltpu.VMEM((n_heads // 2, 2 * blk, 2 * DH_A), BF16),
                        pltpu.VMEM((2 * blk, 2 * blk), BF16),
                        pltpu.VMEM((blk, d_a), F32),
                        pltpu.VMEM((n_heads // 2, 2 * blk, blk), F32)],
        compiler_params=_cparams("arbitrary", "arbitrary"),
        name="sb_prompt",
    )(bias, qkv, qkv, qkv, pair_w(qw), pair_w(kw))


DECODE_BANK = 4
DECODE_SLOTS = 16


def _sb_decode_kernel(pt_ref, q_ref, k_ref, ck_hbm, cv_hbm, qw_ref, kw_ref, bias_ref,
                      o_ref, kn_ref, kbuf, vbuf, sem, *, n_heads, n_pages, layer):
    b = pl.program_id(0)
    n_seq = pl.num_programs(0)
    page = kbuf.shape[3]
    scale = DH_A ** -0.5
    nt = lambda x, y: lax.dot_general(x, y, (((1,), (1,)), ((), ())), preferred_element_type=F32)

    def copies(seq, i, slot):
        phys = pt_ref[seq, n_pages - 1 - i]
        return (pltpu.make_async_copy(ck_hbm.at[layer, phys], kbuf.at[slot], sem.at[0, slot]),
                pltpu.make_async_copy(cv_hbm.at[layer, phys], vbuf.at[slot], sem.at[1, slot]))

    @pl.when(b == 0)
    def _():
        for j in range(DECODE_SLOTS):
            for cp in copies(0, j, j):
                cp.start()

    q = q_ref[0]
    k = k_ref[0]
    hrow = _iota2((n_heads, DH_A), 0)
    qe, kn = [], []
    for h in range(n_heads):
        lanes = slice(h * DH_A, (h + 1) * DH_A)
        qn = _rms(q[:, lanes], qw_ref[...]) * scale
        qe.append(jnp.where(hrow == h, jnp.broadcast_to(qn, (n_heads, DH_A)), 0.0).astype(BF16))
        kn.append(_rms(k[:, lanes], kw_ref[...]))
    kn_ref[0] = jnp.concatenate(kn, axis=-1)

    krow = _iota2((page, 2 * page), 0)
    kcol = _iota2((page, 2 * page), 1)
    later_ones = jnp.where((kcol >= page) | (krow > kcol), 1.0, 0.0).astype(BF16)
    bias = bias_ref[...]

    def bank_pages(first, bank, tail, acc):
        pages = range(DECODE_BANK)
        slots = [bank * DECODE_BANK + j for j in pages]
        for j in pages:
            for cp in copies(b, first + j, slots[j]):
                cp.wait()
        zs = []
        for j in pages:
            z = bias
            for h in range(n_heads):
                z = z + jnp.dot(qe[h], kbuf[slots[j], h].astype(BF16), preferred_element_type=F32)
            zs.append(z)
        sps = [_softplus(z) for z in zs]
        his = [sp.astype(BF16) for sp in sps]
        los = [(sp - hi.astype(F32)).astype(BF16) for sp, hi in zip(sps, his)]
        cums = [jnp.dot(hi, later_ones, preferred_element_type=F32)
                + jnp.dot(lo, later_ones, preferred_element_type=F32) for hi, lo in zip(his, los)]
        ws = []
        for j in pages:
            ws.append(jnp.exp((zs[j] - sps[j]) - (cums[j][:, :page] + tail)).astype(BF16))
            tail = tail + cums[j][:, page:]
        for j in pages:
            acc = [acc[h] + nt(ws[j], vbuf[slots[j], h].astype(BF16)) for h in range(n_heads)]

        more = first + DECODE_SLOTS < n_pages

        @pl.when(more)
        def _():
            for j in pages:
                for cp in copies(b, first + DECODE_SLOTS + j, slots[j]):
                    cp.start()

        @pl.when(jnp.logical_not(more) & (b + 1 < n_seq))
        def _():
            for j in pages:
                for cp in copies(b + 1, slots[j], slots[j]):
                    cp.start()

        return tail, acc

    def group(gi, carry):
        tail, acc = carry
        for bank in range(DECODE_SLOTS // DECODE_BANK):
            tail, acc = bank_pages(gi * DECODE_SLOTS + bank * DECODE_BANK, bank, tail, acc)
        return tail, acc

    zero = jnp.zeros((n_heads, DH_A), F32)
    _, acc = lax.fori_loop(0, n_pages // DECODE_SLOTS, group,
                           (jnp.zeros((n_heads, page), F32), [zero] * n_heads))
    o_ref[0] = jnp.concatenate([acc[h][h:h + 1, :] for h in range(n_heads)], axis=-1)


def sb_decode(qkv, cache_k, cache_v, page_table, layer, qw, kw, bias):
    b, n3 = qkv.shape
    d_a = n3 // 3
    n_heads = d_a // DH_A
    n_layers, n_phys, page = cache_k.shape[:3]
    n_pages = page_table.shape[1]
    assert n_pages % DECODE_SLOTS == 0 and cache_k.shape[3] == n_heads
    ck_t = jnp.transpose(cache_k, (0, 1, 3, 4, 2))
    cv_t = jnp.transpose(cache_v, (0, 1, 3, 4, 2))
    qkv3 = qkv.reshape(b, 1, n3)
    grid_spec = pltpu.PrefetchScalarGridSpec(
        num_scalar_prefetch=1,
        grid=(b,),
        in_specs=[pl.BlockSpec((1, 1, d_a), lambda bi, pt: (bi, 0, 0)),
                  pl.BlockSpec((1, 1, d_a), lambda bi, pt: (bi, 0, 1)),
                  pl.BlockSpec(memory_space=pl.ANY),
                  pl.BlockSpec(memory_space=pl.ANY),
                  pl.BlockSpec((1, DH_A), lambda bi, pt: (0, 0)),
                  pl.BlockSpec((1, DH_A), lambda bi, pt: (0, 0)),
                  pl.BlockSpec((n_heads, 1), lambda bi, pt: (0, 0))],
        out_specs=[pl.BlockSpec((1, 1, d_a), lambda bi, pt: (bi, 0, 0)),
                   pl.BlockSpec((1, 1, d_a), lambda bi, pt: (bi, 0, 0))],
        scratch_shapes=[pltpu.VMEM((DECODE_SLOTS, n_heads, DH_A, page), F32),
                        pltpu.VMEM((DECODE_SLOTS, n_heads, DH_A, page), F32),
                        pltpu.SemaphoreType.DMA((2, DECODE_SLOTS))],
    )
    o, kn = pl.pallas_call(
        functools.partial(_sb_decode_kernel, n_heads=n_heads, n_pages=n_pages, layer=layer),
        grid_spec=grid_spec,
        out_shape=[jax.ShapeDtypeStruct((b, 1, d_a), F32), jax.ShapeDtypeStruct((b, 1, d_a), F32)],
        compiler_params=_cparams("arbitrary"),
        name="sb_decode",
    )(page_table, qkv3, qkv3, ck_t, cv_t, qw.reshape(1, DH_A), kw.reshape(1, DH_A),
      bias.reshape(n_heads, 1))
    return o.reshape(b, d_a), kn.reshape(b, d_a)


def _rwkv_tokens(pb, prev, mu, w0, w_up, a0, a_up, g_up, k_k, k_a, d_b):
    xs = pb + (prev - pb) * mu
    r = xs[:, 0:d_b]
    k = xs[:, d_b:2 * d_b]
    v = xs[:, 2 * d_b:3 * d_b]
    xw = xs[:, 3 * d_b:3 * d_b + R_W]
    xa = xs[:, 3 * d_b + R_W:3 * d_b + R_W + R_A]
    xg = xs[:, 3 * d_b + R_W + R_A:3 * d_b + R_W + R_A + R_G]
    w_log = -_softplus(-(w0 + _dot(jnp.tanh(xw), w_up))) - 0.5
    log_decay = -jnp.exp(w_log)
    a = _sigmoid(a0 + _dot(xa, a_up))
    g = _dot(_sigmoid(xg), g_up)
    kk_raw = k * k_k
    k_mod = k * (1.0 + (a - 1.0) * k_a)
    return r, k_mod, v, kk_raw, a, g, log_decay


def _rwkv_finish(o, r, k_mod, v, g, r_k, ln_w, ln_b):
    mean = jnp.mean(o, axis=-1, keepdims=True)
    var = jnp.mean(jnp.square(o - mean), axis=-1, keepdims=True)
    on = (o - mean) * lax.rsqrt(var + GN_EPS) * ln_w + ln_b
    bonus = jnp.sum(r * k_mod * r_k, axis=-1, keepdims=True) * v
    return (on + bonus) * g


def _rwkv_chunk_kernel(pb_ref, shift0_ref, s0_ref, mu_ref, w0_ref, wup_ref, a0_ref, aup_ref, gup_ref,
                       kk_ref, ka_ref, rk_ref, lnw_ref, lnb_ref, hsum_ref, o_ref, sfin_ref,
                       carry_scr, s_scr, o_scr, *, n_heads):
    c = pl.program_id(1)
    d_b = n_heads * N_B
    n_seq, t, nb = pb_ref.shape

    @pl.when(c == 0)
    def _():
        carry_scr[...] = jnp.zeros_like(carry_scr)
        carry_scr[:, SUBLANES - 1:SUBLANES, :] = shift0_ref[...]
        s_scr[...] = s0_ref[...]

    pbs = [pb_ref[i] for i in range(n_seq)]
    prev = jnp.concatenate([_shift_rows(pbs[i], carry_scr[i], 1) for i in range(n_seq)], axis=0)
    for i in range(n_seq):
        carry_scr[i] = pbs[i][t - SUBLANES:t]
    r, k_mod, v, kk_raw, a, g, log_decay = _rwkv_tokens(
        jnp.concatenate(pbs, axis=0), prev, mu_ref[...], w0_ref[...], wup_ref[...], a0_ref[...],
        aup_ref[...], gup_ref[...], kk_ref[...], ka_ref[...], d_b)

    row = _iota2((t, t), 0)
    col = _iota2((t, t), 1)
    incl = row >= col
    strict = row > col
    rows = n_seq * t
    srow = _iota2((rows, rows), 0)
    scol = _iota2((rows, rows), 1)
    lg_t = t.bit_length() - 1
    same_seq = (srow >> lg_t) == (scol >> lg_t)
    cum = _dot_mask_lhs((same_seq & (srow >= scol)).astype(BF16), log_decay)
    cum_prev = cum - log_decay
    cum_last = jnp.concatenate(
        [jnp.broadcast_to(cum[(i + 1) * t - 1:(i + 1) * t, :], (t, d_b)) for i in range(n_seq)], axis=0)
    e_cum = jnp.exp(cum)
    e_prev = jnp.exp(cum_prev)
    e_neg = jnp.exp(-cum)
    e_rest = jnp.exp(cum_last - cum)
    e_last = jnp.exp(cum_last)

    items = [(i, h) for i in range(n_seq) for h in range(n_heads)]
    n = range(len(items))
    sub = lambda z, i, h: z[i * t:(i + 1) * t, h * N_B:(h + 1) * N_B]
    pick = lambda z: [sub(z, i, h) for i, h in items]
    def head_sum(z):
        z_hi, z_lo = _split2(z)
        return (jnp.dot(z_hi, hsum_ref[...], preferred_element_type=F32)
                + jnp.dot(z_lo, hsum_ref[...], preferred_element_type=F32))

    kk_all = kk_raw * lax.rsqrt(head_sum(kk_raw * kk_raw) + L2_EPS)
    ka_all = kk_all * a
    v_ = pick(v)
    x = [jnp.concatenate([p, q], axis=0) for p, q in zip(pick(kk_all * e_prev), pick(r * e_cum))]
    khat = pick(k_mod * e_neg)
    bhat = pick(ka_all * e_neg)
    k_rest = pick(k_mod * e_rest)
    b_rest = pick(ka_all * e_rest)
    elast_ = pick(e_last)
    s = [s_scr[i, h] for i, h in items]
    xb = [_dot_nt(x[j], bhat[j]) for j in n]
    xk = [_dot_nt(x[j], khat[j]) for j in n]
    xs = [_dot_nt(x[j], s[j]) for j in n]
    tinv = _tri_inv_multi([jnp.where(strict, xb[j][:t], 0.0) for j in n])
    rhs = [xs[j][:t] + _dot(jnp.where(strict, xk[j][:t], 0.0), v_[j]) for j in n]
    u = [_dot(tinv[j], rhs[j]) for j in n]
    o = [xs[j][t:] + _dot(jnp.where(incl, xk[j][t:], 0.0), v_[j])
         - _dot(jnp.where(incl, xb[j][t:], 0.0), u[j]) for j in n]
    for j, (i, h) in enumerate(items):
        s_scr[i, h] = s[j] * elast_[j][:1] + _dot_tn(
            jnp.concatenate([v_[j], -u[j]], axis=0),
            jnp.concatenate([k_rest[j], b_rest[j]], axis=0))
    for j, (i, h) in enumerate(items):
        o_scr[i * t:(i + 1) * t, h * N_B:(h + 1) * N_B] = o[j]
    o_all = o_scr[...]
    inv_n = 1.0 / N_B
    cen = o_all - head_sum(o_all) * inv_n
    var = head_sum(cen * cen) * inv_n
    on = cen * lax.rsqrt(var + GN_EPS) * lnw_ref[...] + lnb_ref[...]
    out = (on + head_sum(r * k_mod * rk_ref[...]) * v) * g
    for i in range(n_seq):
        o_ref[i] = out[i * t:(i + 1) * t]

    @pl.when(c == pl.num_programs(1) - 1)
    def _():
        sfin_ref[...] = s_scr[...]


def _rwkv_params(p, d_b):
    row = lambda x: x.reshape(1, -1)
    return (row(p['mu']), row(p['w0']), p['w_up'], row(p['a0']), p['a_up'], p['g_up'],
            row(p['k_k']), row(p['k_a']), row(p['r_k']), row(p['ln_w']), row(p['ln_b']))


def rwkv_chunked(pb, shift0, s0, p, batch, seq):
    m, nb = pb.shape
    n_heads = s0.shape[1]
    d_b = n_heads * N_B
    t = CHUNK
    ns = SEQS_PER_STEP if batch % SEQS_PER_STEP == 0 else 1
    assert seq % t == 0 and t & (t - 1) == 0
    nc = seq // t
    head_sum = jnp.kron(jnp.eye(n_heads, dtype=F32), jnp.ones((N_B, N_B), F32)).astype(BF16)
    params = _rwkv_params(p, d_b) + (head_sum,)
    const = lambda b, c: (0, 0)
    o, s_fin = pl.pallas_call(
        functools.partial(_rwkv_chunk_kernel, n_heads=n_heads),
        grid=(batch // ns, nc),
        in_specs=[pl.BlockSpec((ns, t, nb), lambda b, c: (b, c, 0)),
                  pl.BlockSpec((ns, 1, nb), lambda b, c: (b, 0, 0)),
                  pl.BlockSpec((ns, n_heads, N_B, N_B), lambda b, c: (b, 0, 0, 0))]
                 + [pl.BlockSpec(x.shape, const) for x in params],
        out_specs=[pl.BlockSpec((ns, t, d_b), lambda b, c: (b, c, 0)),
                   pl.BlockSpec((ns, n_heads, N_B, N_B), lambda b, c: (b, 0, 0, 0))],
        out_shape=[jax.ShapeDtypeStruct((batch, seq, d_b), F32),
                   jax.ShapeDtypeStruct((batch, n_heads, N_B, N_B), F32)],
        scratch_shapes=[pltpu.VMEM((ns, SUBLANES, nb), F32), pltpu.VMEM((ns, n_heads, N_B, N_B), F32),
                        pltpu.VMEM((ns * t, d_b), F32)],
        compiler_params=_cparams("arbitrary", "arbitrary"),
        name="rwkv_chunked",
    )(pb.reshape(batch, seq, nb), shift0.reshape(batch, 1, nb), s0, *params)
    return o.reshape(m, d_b), s_fin


def _col_from_row(x_row, eye):
    return jnp.sum(eye * x_row, axis=-1, keepdims=True)


def _row_from_col(x_col, eye):
    return jnp.sum(eye * x_col, axis=0, keepdims=True)


def _rwkv_step_kernel(pb_ref, shift0_ref, s0_ref, mu_ref, w0_ref, wup_ref, a0_ref, aup_ref, gup_ref,
                      kk_ref, ka_ref, rk_ref, lnw_ref, lnb_ref, o_ref, s_ref, *, n_heads):
    d_b = n_heads * N_B
    r, k_mod, v, kk_raw, a, g, log_decay = _rwkv_tokens(
        pb_ref[0], shift0_ref[0], mu_ref[...], w0_ref[...], wup_ref[...], a0_ref[...], aup_ref[...],
        gup_ref[...], kk_ref[...], ka_ref[...], d_b)
    decay = jnp.exp(log_decay)
    eye = (_iota2((N_B, N_B), 0) == _iota2((N_B, N_B), 1)).astype(F32)
    for h in range(n_heads):
        lanes = slice(h * N_B, (h + 1) * N_B)
        kk = _l2(kk_raw[:, lanes])
        s = s0_ref[0, h]
        sa = jnp.sum(s * kk, axis=-1, keepdims=True)
        v_col = _col_from_row(v[:, lanes], eye)
        s = s * decay[:, lanes] - sa * (kk * a[:, lanes]) + v_col * k_mod[:, lanes]
        s_ref[0, h] = s
        o = _row_from_col(jnp.sum(s * r[:, lanes], axis=-1, keepdims=True), eye)
        o_ref[0, :, lanes] = _rwkv_finish(o, r[:, lanes], k_mod[:, lanes], v[:, lanes], g[:, lanes],
                                          rk_ref[:, lanes], lnw_ref[:, lanes], lnb_ref[:, lanes])


def rwkv_step(pb, shift0, s0, p):
    b, nb = pb.shape
    n_heads = s0.shape[1]
    d_b = n_heads * N_B
    params = _rwkv_params(p, d_b)
    o, s = pl.pallas_call(
        functools.partial(_rwkv_step_kernel, n_heads=n_heads),
        grid=(b,),
        in_specs=[pl.BlockSpec((1, 1, nb), lambda i: (i, 0, 0)),
                  pl.BlockSpec((1, 1, nb), lambda i: (i, 0, 0)),
                  pl.BlockSpec((1, n_heads, N_B, N_B), lambda i: (i, 0, 0, 0))]
                 + [pl.BlockSpec(x.shape, lambda i: (0, 0)) for x in params],
        out_specs=[pl.BlockSpec((1, 1, d_b), lambda i: (i, 0, 0)),
                   pl.BlockSpec((1, n_heads, N_B, N_B), lambda i: (i, 0, 0, 0))],
        out_shape=[jax.ShapeDtypeStruct((b, 1, d_b), F32),
                   jax.ShapeDtypeStruct((b, n_heads, N_B, N_B), F32)],
        compiler_params=_cparams("arbitrary"),
        name="rwkv_step",
    )(pb.reshape(b, 1, nb), shift0.reshape(b, 1, nb), s0, *params)
    return o.reshape(b, d_b), s


def _gdn_gates(gates, alog_row, dtb_row):
    beta = _sigmoid(gates)
    g = -jnp.exp(alog_row) * _softplus(gates + dtb_row)
    return beta, g


def _gdn_chunk_kernel(qkv_ref, z_ref, gates_ref, conv0_ref, s0_ref, cw_ref, alog_ref, dtb_ref, onorm_ref,
                      o_ref, sfin_ref, carry_scr, s_scr, *, n_heads):
    c = pl.program_id(1)
    n_seq, t, _ = qkv_ref.shape
    dqk = n_heads * DK_C

    @pl.when(c == 0)
    def _():
        carry_scr[:, 0:SUBLANES, :] = jnp.zeros((n_seq, SUBLANES, carry_scr.shape[2]), F32)
        carry_scr[:, SUBLANES - (DN_CONV_W - 1):SUBLANES, :] = conv0_ref[...]
        s_scr[...] = s0_ref[...]

    ys = []
    for i in range(n_seq):
        x = qkv_ref[i]
        carry_scr[i, SUBLANES:SUBLANES + t, :] = x
        yi = cw_ref[DN_CONV_W - 1:DN_CONV_W, :] * x
        for d in range(1, DN_CONV_W):
            yi = yi + (cw_ref[DN_CONV_W - 1 - d:DN_CONV_W - d, :]
                       * carry_scr[i, SUBLANES - d:SUBLANES - d + t, :])
        carry_scr[i, 0:SUBLANES, :] = x[t - SUBLANES:t]
        ys.append(yi)
    y = _silu(jnp.concatenate(ys, axis=0))

    row = _iota2((t, t), 0)
    col = _iota2((t, t), 1)
    incl = row >= col
    strict = row > col
    rows = n_seq * t
    srow = _iota2((rows, rows), 0)
    scol = _iota2((rows, rows), 1)
    lg_t = t.bit_length() - 1
    same_seq = (srow >> lg_t) == (scol >> lg_t)
    gates = jnp.concatenate([gates_ref[i] for i in range(n_seq)], axis=0)
    beta_all, g_all = _gdn_gates(gates, alog_ref[...], dtb_ref[...])
    gcum_all = _dot_mask_lhs((same_seq & (srow >= scol)).astype(BF16), g_all)
    g3 = _split3(gcum_all)
    lane = _iota2((t, LANES), 1)
    z = jnp.concatenate([z_ref[i] for i in range(n_seq)], axis=0)

    items = [(i, h) for i in range(n_seq) for h in range(n_heads)]
    n = range(len(items))
    rs = lambda i: slice(i * t, (i + 1) * t)
    q = [_l2(y[rs(i), h * DK_C:(h + 1) * DK_C]) * (DK_C ** -0.5) for i, h in items]
    k = [_l2(y[rs(i), dqk + h * DK_C:dqk + (h + 1) * DK_C]) for i, h in items]
    v = [y[rs(i), 2 * dqk + h * DV_C:2 * dqk + (h + 1) * DV_C] for i, h in items]
    beta = [beta_all[rs(i), h:h + 1] for i, h in items]
    gc = [gcum_all[rs(i), n_heads + h:n_heads + h + 1] for i, h in items]
    egc = [jnp.exp(x) for x in gc]
    nt = lambda a, b: lax.dot_general(a, b, (((1,), (1,)), ((), ())), preferred_element_type=F32)
    gc_row = []
    for i, h in items:
        pk = (lane == n_heads + h).astype(BF16)
        gc_row.append(nt(pk, g3[0][rs(i)]) + (nt(pk, g3[1][rs(i)]) + nt(pk, g3[2][rs(i)])))
    lmask = [jnp.where(incl, jnp.exp(jnp.where(incl, gc[j] - gc_row[j], 0.0)), 0.0) for j in n]
    kb = [k[j] * beta[j] for j in n]
    kq = [_dot_nt(jnp.concatenate([kb[j], q[j]], axis=0), k[j]) for j in n]
    tinv = _tri_inv_multi([jnp.where(strict, kq[j][:t] * lmask[j], 0.0) for j in n])
    uw = [_dot(tinv[j], jnp.concatenate([v[j] * beta[j], kb[j] * egc[j]], axis=1)) for j in n]
    s = [s_scr[i, h] for i, h in items]
    ws = [_dot(jnp.concatenate([uw[j][:, DV_C:], q[j] * egc[j]], axis=0), s[j]) for j in n]
    v_new = [uw[j][:, :DV_C] - ws[j][:t] for j in n]
    o = [ws[j][t:] + _dot(jnp.where(incl, kq[j][t:] * lmask[j], 0.0), v_new[j]) for j in n]
    for j, (i, h) in enumerate(items):
        g_last = gc[j][t - 1:t, :]
        s_scr[i, h] = s[j] * jnp.exp(g_last) + _dot_tn(k[j] * jnp.exp(g_last - gc[j]), v_new[j])
    for j, (i, h) in enumerate(items):
        o_ref[i, :, h * DV_C:(h + 1) * DV_C] = (_rms(o[j], onorm_ref[...])
                                                * _silu(z[rs(i), h * DV_C:(h + 1) * DV_C]))

    @pl.when(c == pl.num_programs(1) - 1)
    def _():
        sfin_ref[...] = s_scr[...]


def _gdn_gate_params(a_log, dt_bias, n_heads):
    pad = lambda x: jnp.zeros((1, LANES), F32).at[0, n_heads:2 * n_heads].set(x)
    return pad(a_log), pad(dt_bias)


def gdn_chunked(qkvz, gates, conv0, s0, cw, a_log, dt_bias, onorm, batch, seq):
    m = qkvz.shape[0]
    n_heads = s0.shape[1]
    dconv = cw.shape[1]
    dvv = n_heads * DV_C
    assert qkvz.shape[1] == dconv + dvv and dconv % dvv == 0
    t = CHUNK
    ns = SEQS_PER_STEP if batch % SEQS_PER_STEP == 0 else 1
    assert seq % t == 0 and t & (t - 1) == 0
    nc = seq // t
    alog_row, dtb_row = _gdn_gate_params(a_log, dt_bias, n_heads)
    const = lambda b, c: (0, 0)
    qkvz3 = qkvz.reshape(batch, seq, dconv + dvv)
    o, s_fin = pl.pallas_call(
        functools.partial(_gdn_chunk_kernel, n_heads=n_heads),
        grid=(batch // ns, nc),
        in_specs=[pl.BlockSpec((ns, t, dconv), lambda b, c: (b, c, 0)),
                  pl.BlockSpec((ns, t, dvv), lambda b, c: (b, c, dconv // dvv)),
                  pl.BlockSpec((ns, t, LANES), lambda b, c: (b, c, 0)),
                  pl.BlockSpec((ns, DN_CONV_W - 1, dconv), lambda b, c: (b, 0, 0)),
                  pl.BlockSpec((ns, n_heads, DK_C, DV_C), lambda b, c: (b, 0, 0, 0)),
                  pl.BlockSpec((DN_CONV_W, dconv), const),
                  pl.BlockSpec((1, LANES), const),
                  pl.BlockSpec((1, LANES), const),
                  pl.BlockSpec((1, DV_C), const)],
        out_specs=[pl.BlockSpec((ns, t, dvv), lambda b, c: (b, c, 0)),
                   pl.BlockSpec((ns, n_heads, DK_C, DV_C), lambda b, c: (b, 0, 0, 0))],
        out_shape=[jax.ShapeDtypeStruct((batch, seq, dvv), F32),
                   jax.ShapeDtypeStruct((batch, n_heads, DK_C, DV_C), F32)],
        scratch_shapes=[pltpu.VMEM((ns, SUBLANES + t, dconv), F32),
                        pltpu.VMEM((ns, n_heads, DK_C, DV_C), F32)],
        compiler_params=_cparams("arbitrary", "arbitrary"),
        name="gdn_chunked",
    )(qkvz3, qkvz3, gates.reshape(batch, seq, LANES), conv0, s0, cw, alog_row, dtb_row,
      onorm.reshape(1, DV_C))
    return o.reshape(m, dvv), s_fin


def _gdn_step_kernel(qkv_ref, z_ref, gates_ref, conv0_ref, s0_ref, cw_ref, alog_ref, dtb_ref, onorm_ref,
                     o_ref, s_ref, *, n_heads):
    dqk = n_heads * DK_C
    x = qkv_ref[0]
    y = cw_ref[DN_CONV_W - 1:DN_CONV_W, :] * x
    for i in range(DN_CONV_W - 1):
        y = y + cw_ref[i:i + 1, :] * conv0_ref[0, i:i + 1, :]
    y = _silu(y)
    beta_all, g_all = _gdn_gates(gates_ref[0], alog_ref[...], dtb_ref[...])
    z = z_ref[0]
    eye = (_iota2((DK_C, DK_C), 0) == _iota2((DK_C, DK_C), 1)).astype(F32)
    for h in range(n_heads):
        q = _l2(y[:, h * DK_C:(h + 1) * DK_C]) * (DK_C ** -0.5)
        k = _l2(y[:, dqk + h * DK_C:dqk + (h + 1) * DK_C])
        v = y[:, 2 * dqk + h * DV_C:2 * dqk + (h + 1) * DV_C]
        beta = beta_all[:, h:h + 1]
        decay = jnp.exp(g_all[:, n_heads + h:n_heads + h + 1])
        s = s0_ref[0, h]
        k_col = _col_from_row(k, eye)
        q_col = _col_from_row(q, eye)
        v_new = beta * v - jnp.sum((k_col * (beta * decay)) * s, axis=0, keepdims=True)
        o = (jnp.sum((q_col * decay) * s, axis=0, keepdims=True)
             + jnp.sum(q * k, axis=-1, keepdims=True) * v_new)
        s_ref[0, h] = s * decay + k_col * v_new
        o_ref[0, :, h * DV_C:(h + 1) * DV_C] = (_rms(o, onorm_ref[...])
                                                * _silu(z[:, h * DV_C:(h + 1) * DV_C]))


def gdn_step(qkvz, gates, conv0, s0, cw, a_log, dt_bias, onorm):
    b = qkvz.shape[0]
    n_heads = s0.shape[1]
    dconv = cw.shape[1]
    dvv = n_heads * DV_C
    alog_row, dtb_row = _gdn_gate_params(a_log, dt_bias, n_heads)
    qkvz3 = qkvz.reshape(b, 1, dconv + dvv)
    const = lambda i: (0, 0)
    o, s = pl.pallas_call(
        functools.partial(_gdn_step_kernel, n_heads=n_heads),
        grid=(b,),
        in_specs=[pl.BlockSpec((1, 1, dconv), lambda i: (i, 0, 0)),
                  pl.BlockSpec((1, 1, dvv), lambda i: (i, 0, dconv // dvv)),
                  pl.BlockSpec((1, 1, LANES), lambda i: (i, 0, 0)),
                  pl.BlockSpec((1, DN_CONV_W - 1, dconv), lambda i: (i, 0, 0)),
                  pl.BlockSpec((1, n_heads, DK_C, DV_C), lambda i: (i, 0, 0, 0)),
                  pl.BlockSpec((DN_CONV_W, dconv), const),
                  pl.BlockSpec((1, LANES), const),
                  pl.BlockSpec((1, LANES), const),
                  pl.BlockSpec((1, DV_C), const)],
        out_specs=[pl.BlockSpec((1, 1, dvv), lambda i: (i, 0, 0)),
                   pl.BlockSpec((1, n_heads, DK_C, DV_C), lambda i: (i, 0, 0, 0))],
        out_shape=[jax.ShapeDtypeStruct((b, 1, dvv), F32),
                   jax.ShapeDtypeStruct((b, n_heads, DK_C, DV_C), F32)],
        compiler_params=_cparams("arbitrary"),
        name="gdn_step",
    )(qkvz3, qkvz3, gates.reshape(b, 1, LANES), conv0, s0, cw, alog_row, dtb_row,
      onorm.reshape(1, DV_C))
    return o.reshape(b, dvv), s


def _row_tile(m, want):
    return want if m % want == 0 else m


def _trunk(x, kv, wkv0, shift0, dn0, dnconv0, ffnconv0, p):
    b, t, d = x.shape
    m = b * t
    depth = p['norm_ffn'].shape[0]
    d_a = p['sb_bias'].shape[1] * DH_A
    nb = p['mu_b'].shape[1]
    dconv = p['conv_dn'].shape[2]
    n_heads_c = p['a_log_dn'].shape[1]
    dvv = n_heads_c * DV_C
    tm = _row_tile(m, PROJ_ROWS)
    tm_out = _row_tile(m, OUT_ROWS)
    new_k, new_v, new_wkv, new_shift, new_dn, new_dnconv, new_ffn = [], [], [], [], [], [], []
    x2 = x.reshape(m, d)
    for l in range(depth):
        i = l // 2
        if l % 2 == 0:
            qkv, pb = norm_matmul(x2, p['norm_mix_even'][i], p['w_in_even'][i], (3 * d_a, nb), tm)
            rp = dict(mu=p['mu_b'][i], w0=p['w0_b'][i], w_up=p['w_up_b'][i], a0=p['a0_b'][i],
                      a_up=p['a_up_b'][i], g_up=p['g_up_b'][i], k_k=p['kk_b'][i], k_a=p['ka_b'][i],
                      r_k=p['rk_b'][i], ln_w=p['lnx_w_b'][i], ln_b=p['lnx_b_b'][i])
            if kv is None:
                o_a, kn_t, v_t = sb_prompt(qkv, p['qnorm_a'][i], p['knorm_a'][i], p['sb_bias'][i], b, t)
                o_b, wkv = rwkv_chunked(pb, shift0[i], wkv0[i], rp, b, t)
                shift = pb.reshape(b, t, nb)[:, t - 1]
                new_k.append(jnp.transpose(kn_t, (0, 3, 1, 2)))
                new_v.append(jnp.transpose(v_t, (0, 3, 1, 2)))
            else:
                o_a, kn = sb_decode(qkv, kv[0], kv[1], kv[2], i, p['qnorm_a'][i], p['knorm_a'][i],
                                    p['sb_bias'][i])
                o_b, wkv = rwkv_step(pb, shift0[i], wkv0[i], rp)
                shift = pb
                new_k.append(kn.reshape(b, t, d_a // DH_A, DH_A))
                new_v.append(qkv[:, 2 * d_a:].reshape(b, t, d_a // DH_A, DH_A))
            w_out = p['w_out_even'][i]
            mix, w_mix = [o_a, o_b], [w_out[:d_a], w_out[d_a:]]
            new_wkv.append(wkv)
            new_shift.append(shift)
        else:
            qkvz, gates = norm_matmul(x2, p['norm_mix_odd'][i], p['w_in_odd'][i], (dconv + dvv, LANES), tm)
            if kv is None:
                o_c, s_fin = gdn_chunked(qkvz, gates, dnconv0[i], dn0[i], p['conv_dn'][i], p['a_log_dn'][i],
                                         p['dt_bias_dn'][i], p['onorm_dn'][i], b, t)
                cbuf = qkvz.reshape(b, t, dconv + dvv)[:, t - (DN_CONV_W - 1):, :dconv]
            else:
                o_c, s_fin = gdn_step(qkvz, gates, dnconv0[i], dn0[i], p['conv_dn'][i], p['a_log_dn'][i],
                                      p['dt_bias_dn'][i], p['onorm_dn'][i])
                cbuf = jnp.concatenate([dnconv0[i][:, 1:], qkvz[:, None, :dconv]], axis=1)
            mix, w_mix = [o_c], [p['w_out_odd'][i]]
            new_dn.append(s_fin)
            new_dnconv.append(cbuf)
        ffn_args = (p['norm_ffn'][l], p['w_gate'][l], p['w_up'][l], p['conv_ffn'][l], p['conv_ffn_b'][l],
                    p['w_down'][l])
        if kv is None:
            x3, fbuf = conv_ffn([o.reshape(b, t, -1) for o in mix], w_mix, x2.reshape(b, t, d),
                                ffnconv0[l], *ffn_args, tm=_row_tile(t, FFN_ROWS))
            x2 = x3.reshape(m, d)
        else:
            x2 = matmul_res(mix, w_mix, x2, tm_out)
            x2, fbuf = conv_ffn_step(x2, ffnconv0[l], *ffn_args)
        new_ffn.append(fbuf)
    states = tuple(jnp.stack(s) for s in (new_k, new_v, new_wkv, new_shift, new_dn, new_dnconv, new_ffn))
    return x2.reshape(b, t, d), states


def kernel(x_prompt, x_sample, cache_k, cache_v, page_table, state_wkv, state_shift, state_dn,
           state_dn_conv, state_ffn_conv, norm_mix_even, w_in_even, qnorm_a, knorm_a, sb_bias,
           mu_b, w0_b, w_up_b, a0_b, a_up_b, g_up_b, kk_b, ka_b, rk_b, lnx_w_b, lnx_b_b,
           w_out_even, norm_mix_odd, w_in_odd, conv_dn, a_log_dn, dt_bias_dn, onorm_dn, w_out_odd,
           norm_ffn, w_gate, w_up, conv_ffn, conv_ffn_b, w_down):
    n_heads_c = a_log_dn.shape[1]
    n_gate_cols = 2 * n_heads_c
    assert x_sample.shape[1] == 1 and n_gate_cols <= LANES
    w_in_odd_p = jnp.pad(w_in_odd, ((0, 0), (0, 0), (0, LANES - n_gate_cols)))
    bf = lambda w: w.astype(BF16)
    p = dict(norm_mix_even=norm_mix_even, w_in_even=bf(w_in_even), qnorm_a=qnorm_a, knorm_a=knorm_a,
             sb_bias=sb_bias, mu_b=mu_b, w0_b=w0_b, w_up_b=bf(w_up_b), a0_b=a0_b, a_up_b=bf(a_up_b),
             g_up_b=bf(g_up_b), kk_b=kk_b, ka_b=ka_b, rk_b=rk_b, lnx_w_b=lnx_w_b, lnx_b_b=lnx_b_b,
             w_out_even=bf(w_out_even), norm_mix_odd=norm_mix_odd, w_in_odd=bf(w_in_odd_p),
             conv_dn=conv_dn, a_log_dn=a_log_dn, dt_bias_dn=dt_bias_dn, onorm_dn=onorm_dn,
             w_out_odd=bf(w_out_odd), norm_ffn=norm_ffn, w_gate=bf(w_gate), w_up=bf(w_up),
             conv_ffn=conv_ffn, conv_ffn_b=conv_ffn_b, w_down=bf(w_down))
    b = x_prompt.shape[0]
    n_ab = state_wkv.shape[0]
    n_c = state_dn.shape[0]
    depth = norm_ffn.shape[0]
    zeros = lambda like, lead: jnp.zeros((lead, b) + like.shape[2:], F32)
    y_prompt, ps = _trunk(x_prompt, None, zeros(state_wkv, n_ab), zeros(state_shift, n_ab),
                          zeros(state_dn, n_c), zeros(state_dn_conv, n_c), zeros(state_ffn_conv, depth), p)
    y_sample, ss = _trunk(x_sample, (cache_k, cache_v, page_table), state_wkv, state_shift, state_dn,
                          state_dn_conv, state_ffn_conv, p)
    return (y_prompt, y_sample, ps[0], ps[1], ss[0], ss[1], ps[2], ss[2], ps[3], ss[3],
            ps[4], ss[4], ps[5], ss[5], ps[6], ss[6])
```

```python
import functools

import jax
import jax.numpy as jnp
from jax import lax
from jax.experimental import pallas as pl
from jax.experimental.pallas import tpu as pltpu

F32 = jnp.float32
BF16 = jnp.bfloat16

RMS_EPS = 1e-6
L2_EPS = 1e-6
GN_EPS = 64e-5

DH_A = 64
N_B = 64
R_W, R_A, R_G = 64, 64, 128
DK_C = 128
DV_C = 128
DN_CONV_W = 4
FFN_CONV_W = 3

SB_BLOCK = 128
CHUNK = 64
SEQS_PER_STEP = 4
SUBLANES = 8
LANES = 128
VMEM_LIMIT_BYTES = 56 * 1024 * 1024
PROJ_ROWS = 512
OUT_ROWS = 1024
FFN_ROWS = 512
FFN_COLS = 256


def _cparams(*sem):
    return pltpu.CompilerParams(dimension_semantics=sem, vmem_limit_bytes=VMEM_LIMIT_BYTES)


def _dot(a, b):
    return jnp.dot(a.astype(BF16), b.astype(BF16), preferred_element_type=F32)


def _dot_nt(a, b):
    return lax.dot_general(a.astype(BF16), b.astype(BF16), (((1,), (1,)), ((), ())),
                           preferred_element_type=F32)


def _dot_tn(a, b):
    return lax.dot_general(a.astype(BF16), b.astype(BF16), (((0,), (0,)), ((), ())),
                           preferred_element_type=F32)


def _split2(a):
    hi = a.astype(BF16)
    lo = (a - hi.astype(F32)).astype(BF16)
    return hi, lo


def _split3(a):
    hi = a.astype(BF16)
    r1 = a - hi.astype(F32)
    mid = r1.astype(BF16)
    lo = (r1 - mid.astype(F32)).astype(BF16)
    return hi, mid, lo


def _dot_mask_lhs(mask_bf16, x):
    d = functools.partial(jnp.dot, preferred_element_type=F32)
    h, m, l = _split3(x)
    return d(mask_bf16, h) + (d(mask_bf16, m) + d(mask_bf16, l))


def _softplus(x):
    return jnp.maximum(x, 0.0) + jnp.log(1.0 + jnp.exp(-jnp.abs(x)))


def _sigmoid(x):
    return 1.0 / (1.0 + jnp.exp(-x))


def _silu(x):
    return x * _sigmoid(x)


def _rms(x, g):
    return x * lax.rsqrt(jnp.mean(x * x, axis=-1, keepdims=True) + RMS_EPS) * g


def _l2(x):
    return x * lax.rsqrt(jnp.sum(x * x, axis=-1, keepdims=True) + L2_EPS)


def _iota2(shape, dim):
    return lax.broadcasted_iota(jnp.int32, shape, dim)


def _shift_rows(cur, carry, d):
    rolled = pltpu.roll(cur, d, 0)
    crolled = pltpu.roll(carry, d, 0)
    first = jnp.where(_iota2(carry.shape, 0) < d, crolled, rolled[:SUBLANES])
    if cur.shape[0] == SUBLANES:
        return first
    return jnp.concatenate([first, rolled[SUBLANES:]], axis=0)


def _tri_inv_multi(a_list):
    n = a_list[0].shape[0]
    row = _iota2((n, n), 0)
    col = _iota2((n, n), 1)
    eye = (row == col).astype(F32)
    d = functools.partial(jnp.dot, preferred_element_type=F32)
    ts = [eye - jnp.where((row >> 1) == (col >> 1), a, 0.0) for a in a_list]
    s = 2
    while s < n:
        sh = s.bit_length() - 1
        sub = ((row >> (sh + 1)) == (col >> (sh + 1))) & ((row >> sh) != (col >> sh))
        tb = [t.astype(BF16) for t in ts]
        tes = [d(t, jnp.where(sub, a, 0.0).astype(BF16)).astype(BF16) for t, a in zip(tb, a_list)]
        ts = [t - d(te, t16) for t, te, t16 in zip(ts, tes, tb)]
        s *= 2
    return ts


def _norm_matmul_kernel(x_ref, g_ref, w_ref, *o_refs, splits):
    h = _rms(x_ref[...], g_ref[...]).astype(BF16)
    y = jnp.dot(h, w_ref[...], preferred_element_type=F32)
    off = 0
    for o_ref, n in zip(o_refs, splits):
        o_ref[...] = y[:, off:off + n]
        off += n


def norm_matmul(x, g, w, splits, tm):
    m, k = x.shape
    n = w.shape[1]
    assert sum(splits) == n and m % tm == 0
    return pl.pallas_call(
        functools.partial(_norm_matmul_kernel, splits=tuple(splits)),
        grid=(m // tm,),
        in_specs=[pl.BlockSpec((tm, k), lambda i: (i, 0)),
                  pl.BlockSpec((1, k), lambda i: (0, 0)),
                  pl.BlockSpec((k, n), lambda i: (0, 0))],
        out_specs=[pl.BlockSpec((tm, s), lambda i: (i, 0)) for s in splits],
        out_shape=[jax.ShapeDtypeStruct((m, s), F32) for s in splits],
        compiler_params=_cparams("arbitrary"),
        name="norm_matmul",
    )(x, g.reshape(1, k), w)


def _matmul_res_kernel(*refs, n_in):
    a_refs = refs[:n_in]
    w_refs = refs[n_in:2 * n_in]
    res_ref = refs[2 * n_in]
    o_ref = refs[2 * n_in + 1]
    acc = res_ref[...]
    for a_ref, w_ref in zip(a_refs, w_refs):
        acc = acc + jnp.dot(a_ref[...].astype(BF16), w_ref[...], preferred_element_type=F32)
    o_ref[...] = acc


def matmul_res(a_list, w_list, res, tm):
    m, n = res.shape
    assert m % tm == 0
    n_in = len(a_list)
    in_specs = ([pl.BlockSpec((tm, a.shape[1]), lambda i: (i, 0)) for a in a_list]
                + [pl.BlockSpec(w.shape, lambda i: (0, 0)) for w in w_list]
                + [pl.BlockSpec((tm, n), lambda i: (i, 0))])
    return pl.pallas_call(
        functools.partial(_matmul_res_kernel, n_in=n_in),
        grid=(m // tm,),
        in_specs=in_specs,
        out_specs=pl.BlockSpec((tm, n), lambda i: (i, 0)),
        out_shape=jax.ShapeDtypeStruct((m, n), F32),
        compiler_params=_cparams("arbitrary"),
        name="matmul_res",
    )(*a_list, *w_list, res)


def _ffn_kernel(*refs, n_mix):
    mix_refs = refs[:n_mix]
    wmix_refs = refs[n_mix:2 * n_mix]
    (x_ref, buf_ref, g_ref, wg_ref, wu_ref, cw_ref, cb_ref, wd_ref,
     y_ref, tail_ref, carry_scr, act_scr) = refs[2 * n_mix:]
    ti = pl.program_id(1)
    x = x_ref[0]
    for o_ref, w_ref in zip(mix_refs, wmix_refs):
        x = x + jnp.dot(o_ref[0].astype(BF16), w_ref[...], preferred_element_type=F32)
    tm = x.shape[0]
    f = wg_ref.shape[1]
    h = _rms(x, g_ref[...]).astype(BF16)

    @pl.when(ti == 0)
    def _():
        carry_scr[...] = jnp.zeros_like(carry_scr)
        carry_scr[SUBLANES - 2:SUBLANES, :] = buf_ref[0]

    for c0 in range(0, f, FFN_COLS):
        cols = slice(c0, min(c0 + FFN_COLS, f))
        gate = jnp.dot(h, wg_ref[:, cols], preferred_element_type=F32)
        up = jnp.dot(h, wu_ref[:, cols], preferred_element_type=F32)
        carry = carry_scr[:, cols]
        conv = (cw_ref[2:3, cols] * gate + cw_ref[1:2, cols] * _shift_rows(gate, carry, 1)
                + cw_ref[0:1, cols] * _shift_rows(gate, carry, 2))
        carry_scr[:, cols] = gate[tm - SUBLANES:tm]
        tail_ref[0, :, cols] = gate[tm - 2:tm]
        act_scr[:, cols] = (_silu(conv + cb_ref[:, cols]) * up).astype(BF16)
    y_ref[0] = x + jnp.dot(act_scr[...], wd_ref[...], preferred_element_type=F32)


def conv_ffn(mix, w_mix, x, buf, g, wg, wu, cw, cb, wd, tm):
    b, t, d = x.shape
    f = wg.shape[1]
    assert t % tm == 0 and tm % SUBLANES == 0
    const = lambda bi, ti: (0, 0)
    return pl.pallas_call(
        functools.partial(_ffn_kernel, n_mix=len(mix)),
        grid=(b, t // tm),
        in_specs=[pl.BlockSpec((1, tm, o.shape[2]), lambda bi, ti: (bi, ti, 0)) for o in mix]
                 + [pl.BlockSpec(w.shape, const) for w in w_mix]
                 + [pl.BlockSpec((1, tm, d), lambda bi, ti: (bi, ti, 0)),
                  pl.BlockSpec((1, 2, f), lambda bi, ti: (bi, 0, 0)),
                  pl.BlockSpec((1, d), const),
                  pl.BlockSpec((d, f), const),
                  pl.BlockSpec((d, f), const),
                  pl.BlockSpec((FFN_CONV_W, f), const),
                  pl.BlockSpec((1, f), const),
                  pl.BlockSpec((f, d), const)],
        out_specs=[pl.BlockSpec((1, tm, d), lambda bi, ti: (bi, ti, 0)),
                   pl.BlockSpec((1, 2, f), lambda bi, ti: (bi, 0, 0))],
        out_shape=[jax.ShapeDtypeStruct((b, t, d), F32),
                   jax.ShapeDtypeStruct((b, 2, f), F32)],
        scratch_shapes=[pltpu.VMEM((SUBLANES, f), F32), pltpu.VMEM((tm, f), BF16)],
        compiler_params=_cparams("arbitrary", "arbitrary"),
        name="conv_ffn",
    )(*mix, *w_mix, x, buf, g.reshape(1, d), wg, wu, cw, cb.reshape(1, f), wd)


def _ffn_step_kernel(x_ref, buf0_ref, buf1_ref, g_ref, wg_ref, wu_ref, cw_ref, cb_ref, wd_ref,
                     y_ref, gate_ref):
    x = x_ref[...]
    h = _rms(x, g_ref[...]).astype(BF16)
    gate = jnp.dot(h, wg_ref[...], preferred_element_type=F32)
    up = jnp.dot(h, wu_ref[...], preferred_element_type=F32)
    conv = cw_ref[2:3, :] * gate + cw_ref[1:2, :] * buf1_ref[...] + cw_ref[0:1, :] * buf0_ref[...]
    act = (_silu(conv + cb_ref[...]) * up).astype(BF16)
    y_ref[...] = x + jnp.dot(act, wd_ref[...], preferred_element_type=F32)
    gate_ref[...] = gate


def conv_ffn_step(x, buf, g, wg, wu, cw, cb, wd):
    b, d = x.shape
    f = wg.shape[1]
    full = lambda shape: pl.BlockSpec(shape, lambda i: (0,) * len(shape))
    y, gate = pl.pallas_call(
        _ffn_step_kernel,
        grid=(1,),
        in_specs=[full((b, d)), full((b, f)), full((b, f)), full((1, d)), full((d, f)), full((d, f)),
                  full((FFN_CONV_W, f)), full((1, f)), full((f, d))],
        out_specs=[full((b, d)), full((b, f))],
        out_shape=[jax.ShapeDtypeStruct((b, d), F32), jax.ShapeDtypeStruct((b, f), F32)],
        compiler_params=_cparams("arbitrary"),
        name="conv_ffn_step",
    )(x, buf[:, 0], buf[:, 1], g.reshape(1, d), wg, wu, cw, cb.reshape(1, f), wd)
    return y, jnp.stack([buf[:, 1], gate], axis=1)


def _pair_rms(x, w_row):
    left = _iota2(x.shape, 1) < DH_A
    x2 = x * x
    ms_l = jnp.sum(jnp.where(left, x2, 0.0), axis=-1, keepdims=True) * (1.0 / DH_A)
    ms_r = jnp.sum(jnp.where(left, 0.0, x2), axis=-1, keepdims=True) * (1.0 / DH_A)
    inv = jnp.where(left, lax.rsqrt(ms_l + RMS_EPS), lax.rsqrt(ms_r + RMS_EPS))
    return x * inv * w_row


def _sb_prompt_kernel(bias_ref, q_ref, k_ref, v_ref, qw_ref, kw_ref, o_ref, knt_ref, vt_ref,
                      kn_scr, v_scr, q_scr, later_scr, o_scr, tail_scr, *, n_heads):
    qi = pl.program_id(1)
    blk = q_ref.shape[0]
    n_pairs = n_heads // 2
    pw = 2 * DH_A
    scale = DH_A ** -0.5
    row0 = pl.multiple_of(qi * blk, blk)

    k = k_ref[...]
    v = v_ref[...]
    kn = jnp.concatenate([_pair_rms(k[:, p * pw:(p + 1) * pw], kw_ref[...]) for p in range(n_pairs)],
                         axis=-1)
    for h in range(n_heads):
        knt_ref[0, h] = kn[:, h * DH_A:(h + 1) * DH_A].T
        vt_ref[0, h] = v[:, h * DH_A:(h + 1) * DH_A].T
    kn_scr[pl.ds(row0, blk), :] = kn.astype(BF16)
    v_scr[pl.ds(row0, blk), :] = v.astype(BF16)

    q = q_ref[...]
    left = _iota2((blk, pw), 1) < DH_A
    for p in range(n_pairs):
        qn = _pair_rms(q[:, p * pw:(p + 1) * pw], qw_ref[...]) * scale
        q_scr[p] = jnp.concatenate([jnp.where(left, qn, 0.0), jnp.where(left, 0.0, qn)],
                                   axis=0).astype(BF16)
    krow = _iota2((2 * blk, 2 * blk), 0)
    krow = jnp.where(krow >= blk, krow - blk, krow)
    kcol = _iota2((2 * blk, 2 * blk), 1)
    later_scr[...] = jnp.where((kcol >= blk) | (krow > kcol), 1.0, 0.0).astype(BF16)
    o_scr[...] = jnp.zeros_like(o_scr)
    tail_scr[...] = jnp.zeros_like(tail_scr)

    qrow = _iota2((2 * blk, blk), 0)
    causal = _iota2((2 * blk, blk), 1) < jnp.where(qrow >= blk, qrow - blk, qrow)
    vleft = _iota2((blk, pw), 1) < DH_A

    def blocks(offs, diag):
        items = [(j, p) for j in range(len(offs)) for p in range(n_pairs)]
        zs = []
        for j, p in items:
            z2 = lax.dot_general(q_scr[p], kn_scr[pl.ds(offs[j], blk), p * pw:(p + 1) * pw],
                                 (((1,), (1,)), ((), ())), preferred_element_type=F32)
            zs.append(jnp.concatenate([z2[:blk] + bias_ref[2 * p], z2[blk:] + bias_ref[2 * p + 1]],
                                      axis=0))
        sps = [_softplus(z) for z in zs]
        if diag:
            sps = [jnp.where(causal, sp, 0.0) for sp in sps]
        his = [sp.astype(BF16) for sp in sps]
        los = [(sp - hi.astype(F32)).astype(BF16) for sp, hi in zip(sps, his)]
        lo_mat = later_scr[...]
        cums = [jnp.dot(jnp.concatenate([hi, lo], axis=1), lo_mat, preferred_element_type=F32)
                for hi, lo in zip(his, los)]
        ws = []
        for p in range(n_pairs):
            tail = tail_scr[p]
            for j in range(len(offs)):
                n = j * n_pairs + p
                w = jnp.exp((zs[n] - sps[n]) - (cums[n][:, :blk] + tail))
                if diag:
                    w = jnp.where(causal, w, 0.0)
                tail = tail + cums[n][:, blk:]
                ws.append(w.astype(BF16))
            tail_scr[p] = tail
        for p in range(n_pairs):
            acc = o_scr[:, p * pw:(p + 1) * pw]
            for j in range(len(offs)):
                w = ws[p * len(offs) + j]
                vj = v_scr[pl.ds(offs[j], blk), p * pw:(p + 1) * pw]
                zero = jnp.zeros_like(vj)
                v_bd = jnp.concatenate([jnp.where(vleft, vj, zero), jnp.where(vleft, zero, vj)], axis=0)
                acc = acc + jnp.dot(jnp.concatenate([w[:blk], w[blk:]], axis=1), v_bd,
                                    preferred_element_type=F32)
            o_scr[:, p * pw:(p + 1) * pw] = acc

    blocks([row0], True)

    def body(i, carry):
        hi_off = pl.multiple_of((qi - 1 - 2 * i) * blk, blk)
        blocks([hi_off, pl.multiple_of(hi_off - blk, blk)], False)
        return carry

    lax.fori_loop(0, qi // 2, body, 0)

    @pl.when(qi % 2 == 1)
    def _():
        blocks([0], False)

    o_ref[...] = o_scr[...]


def sb_prompt(qkv, qw, kw, bias, batch, seq):
    m, n3 = qkv.shape
    d_a = n3 // 3
    n_heads = d_a // DH_A
    blk = SB_BLOCK
    assert seq % blk == 0 and n_heads % 2 == 0 and 2 * DH_A == LANES
    nq = seq // blk
    rowblk = lambda col: (lambda b, qi: (b * nq + qi, col))
    pair_w = lambda w: jnp.tile(w, 2).reshape(1, 2 * DH_A)
    head_t = pl.BlockSpec((1, n_heads, DH_A, blk), lambda b, qi: (b, 0, 0, qi))
    head_t_shape = jax.ShapeDtypeStruct((batch, n_heads, DH_A, seq), F32)
    return pl.pallas_call(
        functools.partial(_sb_prompt_kernel, n_heads=n_heads),
        grid=(batch, nq),
        in_specs=[pl.BlockSpec(memory_space=pltpu.SMEM),
                  pl.BlockSpec((blk, d_a), rowblk(0)),
                  pl.BlockSpec((blk, d_a), rowblk(1)),
                  pl.BlockSpec((blk, d_a), rowblk(2)),
                  pl.BlockSpec((1, 2 * DH_A), lambda b, qi: (0, 0)),
                  pl.BlockSpec((1, 2 * DH_A), lambda b, qi: (0, 0))],
        out_specs=[pl.BlockSpec((blk, d_a), rowblk(0)), head_t, head_t],
        out_shape=[jax.ShapeDtypeStruct((m, d_a), F32), head_t_shape, head_t_shape],
        scratch_shapes=[pltpu.VMEM((seq, d_a), BF16), pltpu.VMEM((seq, d_a), BF16),
                        pltpu.VMEM((n_heads // 2, 2 * blk, 2 * DH_A), BF16),
                        pltpu.VMEM((2 * blk, 2 * blk), BF16),
                        pltpu.VMEM((blk, d_a), F32),
                        pltpu.VMEM((n_heads // 2, 2 * blk, blk), F32)],
        compiler_params=_cparams("arbitrary", "arbitrary"),
        name="sb_prompt",
    )(bias, qkv, qkv, qkv, pair_w(qw), pair_w(kw))


DECODE_BANK = 8
DECODE_SLOTS = 32


def _sb_decode_kernel(pt_ref, q_ref, k_ref, ck_hbm, cv_hbm, qw_ref, kw_ref, bias_ref,
                      o_ref, kn_ref, kbuf, vbuf, sem, *, n_heads, n_pages, layer):
    b = pl.program_id(0)
    n_seq = pl.num_programs(0)
    page = kbuf.shape[3]
    scale = DH_A ** -0.5
    nt = lambda x, y: lax.dot_general(x, y, (((1,), (1,)), ((), ())), preferred_element_type=F32)

    def copies(seq, i, slot):
        phys = pt_ref[seq, n_pages - 1 - i]
        return (pltpu.make_async_copy(ck_hbm.at[layer, phys], kbuf.at[slot], sem.at[0, slot]),
                pltpu.make_async_copy(cv_hbm.at[layer, phys], vbuf.at[slot], sem.at[1, slot]))

    @pl.when(b == 0)
    def _():
        for j in range(DECODE_SLOTS):
            for cp in copies(0, j, j):
                cp.start()

    q = q_ref[0]
    k = k_ref[0]
    hrow = _iota2((n_heads, DH_A), 0)
    qe, kn = [], []
    for h in range(n_heads):
        lanes = slice(h * DH_A, (h + 1) * DH_A)
        qn = _rms(q[:, lanes], qw_ref[...]) * scale
        qe.append(jnp.where(hrow == h, jnp.broadcast_to(qn, (n_heads, DH_A)), 0.0).astype(BF16))
        kn.append(_rms(k[:, lanes], kw_ref[...]))
    kn_ref[0] = jnp.concatenate(kn, axis=-1)

    krow = _iota2((page, 2 * page), 0)
    kcol = _iota2((page, 2 * page), 1)
    later_ones = jnp.where((kcol >= page) | (krow > kcol), 1.0, 0.0).astype(BF16)
    bias = bias_ref[...]

    def bank_pages(first, bank, tail, acc):
        pages = range(DECODE_BANK)
        slots = [bank * DECODE_BANK + j for j in pages]
        for j in pages:
            for cp in copies(b, first + j, slots[j]):
                cp.wait()
        zs = []
        for j in pages:
            z = bias
            for h in range(n_heads):
                z = z + jnp.dot(qe[h], kbuf[slots[j], h].astype(BF16), preferred_element_type=F32)
            zs.append(z)
        sps = [_softplus(z) for z in zs]
        his = [sp.astype(BF16) for sp in sps]
        los = [(sp - hi.astype(F32)).astype(BF16) for sp, hi in zip(sps, his)]
        cums = [jnp.dot(hi, later_ones, preferred_element_type=F32)
                + jnp.dot(lo, later_ones, preferred_element_type=F32) for hi, lo in zip(his, los)]
        ws = []
        for j in pages:
            ws.append(jnp.exp((zs[j] - sps[j]) - (cums[j][:, :page] + tail)).astype(BF16))
            tail = tail + cums[j][:, page:]
        for j in pages:
            acc = [acc[h] + nt(ws[j], vbuf[slots[j], h].astype(BF16)) for h in range(n_heads)]

        more = first + DECODE_SLOTS < n_pages

        @pl.when(more)
        def _():
            for j in pages:
                for cp in copies(b, first + DECODE_SLOTS + j, slots[j]):
                    cp.start()

        @pl.when(jnp.logical_not(more) & (b + 1 < n_seq))
        def _():
            for j in pages:
                for cp in copies(b + 1, slots[j], slots[j]):
                    cp.start()

        return tail, acc

    def group(gi, carry):
        tail, acc = carry
        for bank in range(DECODE_SLOTS // DECODE_BANK):
            tail, acc = bank_pages(gi * DECODE_SLOTS + bank * DECODE_BANK, bank, tail, acc)
        return tail, acc

    zero = jnp.zeros((n_heads, DH_A), F32)
    _, acc = lax.fori_loop(0, n_pages // DECODE_SLOTS, group,
                           (jnp.zeros((n_heads, page), F32), [zero] * n_heads))
    o_ref[0] = jnp.concatenate([acc[h][h:h + 1, :] for h in range(n_heads)], axis=-1)


def sb_decode(qkv, cache_k, cache_v, page_table, layer, qw, kw, bias):
    b, n3 = qkv.shape
    d_a = n3 // 3
    n_heads = d_a // DH_A
    n_layers, n_phys, page = cache_k.shape[:3]
    n_pages = page_table.shape[1]
    assert n_pages % DECODE_SLOTS == 0 and cache_k.shape[3] == n_heads
    ck_t = jnp.transpose(cache_k, (0, 1, 3, 4, 2))
    cv_t = jnp.transpose(cache_v, (0, 1, 3, 4, 2))
    qkv3 = qkv.reshape(b, 1, n3)
    grid_spec = pltpu.PrefetchScalarGridSpec(
        num_scalar_prefetch=1,
        grid=(b,),
        in_specs=[pl.BlockSpec((1, 1, d_a), lambda bi, pt: (bi, 0, 0)),
                  pl.BlockSpec((1, 1, d_a), lambda bi, pt: (bi, 0, 1)),
                  pl.BlockSpec(memory_space=pl.ANY),
                  pl.BlockSpec(memory_space=pl.ANY),
                  pl.BlockSpec((1, DH_A), lambda bi, pt: (0, 0)),
                  pl.BlockSpec((1, DH_A), lambda bi, pt: (0, 0)),
                  pl.BlockSpec((n_heads, 1), lambda bi, pt: (0, 0))],
        out_specs=[pl.BlockSpec((1, 1, d_a), lambda bi, pt: (bi, 0, 0)),
                   pl.BlockSpec((1, 1, d_a), lambda bi, pt: (bi, 0, 0))],
        scratch_shapes=[pltpu.VMEM((DECODE_SLOTS, n_heads, DH_A, page), F32),
                        pltpu.VMEM((DECODE_SLOTS, n_heads, DH_A, page), F32),
                        pltpu.SemaphoreType.DMA((2, DECODE_SLOTS))],
    )
    o, kn = pl.pallas_call(
        functools.partial(_sb_decode_kernel, n_heads=n_heads, n_pages=n_pages, layer=layer),
        grid_spec=grid_spec,
        out_shape=[jax.ShapeDtypeStruct((b, 1, d_a), F32), jax.ShapeDtypeStruct((b, 1, d_a), F32)],
        compiler_params=_cparams("arbitrary"),
        name="sb_decode",
    )(page_table, qkv3, qkv3, ck_t, cv_t, qw.reshape(1, DH_A), kw.reshape(1, DH_A),
      bias.reshape(n_heads, 1))
    return o.reshape(b, d_a), kn.reshape(b, d_a)


def _rwkv_tokens(pb, prev, mu, w0, w_up, a0, a_up, g_up, k_k, k_a, d_b):
    xs = pb + (prev - pb) * mu
    r = xs[:, 0:d_b]
    k = xs[:, d_b:2 * d_b]
    v = xs[:, 2 * d_b:3 * d_b]
    xw = xs[:, 3 * d_b:3 * d_b + R_W]
    xa = xs[:, 3 * d_b + R_W:3 * d_b + R_W + R_A]
    xg = xs[:, 3 * d_b + R_W + R_A:3 * d_b + R_W + R_A + R_G]
    w_log = -_softplus(-(w0 + _dot(jnp.tanh(xw), w_up))) - 0.5
    log_decay = -jnp.exp(w_log)
    a = _sigmoid(a0 + _dot(xa, a_up))
    g = _dot(_sigmoid(xg), g_up)
    kk_raw = k * k_k
    k_mod = k * (1.0 + (a - 1.0) * k_a)
    return r, k_mod, v, kk_raw, a, g, log_decay


def _rwkv_finish(o, r, k_mod, v, g, r_k, ln_w, ln_b):
    mean = jnp.mean(o, axis=-1, keepdims=True)
    var = jnp.mean(jnp.square(o - mean), axis=-1, keepdims=True)
    on = (o - mean) * lax.rsqrt(var + GN_EPS) * ln_w + ln_b
    bonus = jnp.sum(r * k_mod * r_k, axis=-1, keepdims=True) * v
    return (on + bonus) * g


def _rwkv_chunk_kernel(pb_ref, shift0_ref, s0_ref, mu_ref, w0_ref, wup_ref, a0_ref, aup_ref, gup_ref,
                       kk_ref, ka_ref, rk_ref, lnw_ref, lnb_ref, hsum_ref, o_ref, sfin_ref,
                       carry_scr, s_scr, o_scr, *, n_heads):
    c = pl.program_id(1)
    d_b = n_heads * N_B
    n_seq, t, nb = pb_ref.shape

    @pl.when(c == 0)
    def _():
        carry_scr[...] = jnp.zeros_like(carry_scr)
        carry_scr[:, SUBLANES - 1:SUBLANES, :] = shift0_ref[...]
        s_scr[...] = s0_ref[...]

    pbs = [pb_ref[i] for i in range(n_seq)]
    prev = jnp.concatenate([_shift_rows(pbs[i], carry_scr[i], 1) for i in range(n_seq)], axis=0)
    for i in range(n_seq):
        carry_scr[i] = pbs[i][t - SUBLANES:t]
    r, k_mod, v, kk_raw, a, g, log_decay = _rwkv_tokens(
        jnp.concatenate(pbs, axis=0), prev, mu_ref[...], w0_ref[...], wup_ref[...], a0_ref[...],
        aup_ref[...], gup_ref[...], kk_ref[...], ka_ref[...], d_b)

    row = _iota2((t, t), 0)
    col = _iota2((t, t), 1)
    incl = row >= col
    strict = row > col
    rows = n_seq * t
    srow = _iota2((rows, rows), 0)
    scol = _iota2((rows, rows), 1)
    lg_t = t.bit_length() - 1
    same_seq = (srow >> lg_t) == (scol >> lg_t)
    cum = _dot_mask_lhs((same_seq & (srow >= scol)).astype(BF16), log_decay)
    cum_prev = cum - log_decay
    cum_last = jnp.concatenate(
        [jnp.broadcast_to(cum[(i + 1) * t - 1:(i + 1) * t, :], (t, d_b)) for i in range(n_seq)], axis=0)
    e_cum = jnp.exp(cum)
    e_prev = jnp.exp(cum_prev)
    e_neg = jnp.exp(-cum)
    e_rest = jnp.exp(cum_last - cum)
    e_last = jnp.exp(cum_last)

    items = [(i, h) for i in range(n_seq) for h in range(n_heads)]
    n = range(len(items))
    sub = lambda z, i, h: z[i * t:(i + 1) * t, h * N_B:(h + 1) * N_B]
    pick = lambda z: [sub(z, i, h) for i, h in items]
    def head_sum(z):
        z_hi, z_lo = _split2(z)
        return (jnp.dot(z_hi, hsum_ref[...], preferred_element_type=F32)
                + jnp.dot(z_lo, hsum_ref[...], preferred_element_type=F32))

    kk_all = kk_raw * lax.rsqrt(head_sum(kk_raw * kk_raw) + L2_EPS)
    ka_all = kk_all * a
    v_ = pick(v)
    x = [jnp.concatenate([p, q], axis=0) for p, q in zip(pick(kk_all * e_prev), pick(r * e_cum))]
    khat = pick(k_mod * e_neg)
    bhat = pick(ka_all * e_neg)
    k_rest = pick(k_mod * e_rest)
    b_rest = pick(ka_all * e_rest)
    elast_ = pick(e_last)
    s = [s_scr[i, h] for i, h in items]
    xb = [_dot_nt(x[j], bhat[j]) for j in n]
    xk = [_dot_nt(x[j], khat[j]) for j in n]
    xs = [_dot_nt(x[j], s[j]) for j in n]
    tinv = _tri_inv_multi([jnp.where(strict, xb[j][:t], 0.0) for j in n])
    rhs = [xs[j][:t] + _dot(jnp.where(strict, xk[j][:t], 0.0), v_[j]) for j in n]
    u = [_dot(tinv[j], rhs[j]) for j in n]
    o = [xs[j][t:] + _dot(jnp.where(incl, xk[j][t:], 0.0), v_[j])
         - _dot(jnp.where(incl, xb[j][t:], 0.0), u[j]) for j in n]
    for j, (i, h) in enumerate(items):
        s_scr[i, h] = s[j] * elast_[j][:1] + _dot_tn(
            jnp.concatenate([v_[j], -u[j]], axis=0),
            jnp.concatenate([k_rest[j], b_rest[j]], axis=0))
    for j, (i, h) in enumerate(items):
        o_scr[i * t:(i + 1) * t, h * N_B:(h + 1) * N_B] = o[j]
    o_all = o_scr[...]
    inv_n = 1.0 / N_B
    cen = o_all - head_sum(o_all) * inv_n
    var = head_sum(cen * cen) * inv_n
    on = cen * lax.rsqrt(var + GN_EPS) * lnw_ref[...] + lnb_ref[...]
    out = (on + head_sum(r * k_mod * rk_ref[...]) * v) * g
    for i in range(n_seq):
        o_ref[i] = out[i * t:(i + 1) * t]

    @pl.when(c == pl.num_programs(1) - 1)
    def _():
        sfin_ref[...] = s_scr[...]


def _rwkv_params(p, d_b):
    row = lambda x: x.reshape(1, -1)
    return (row(p['mu']), row(p['w0']), p['w_up'], row(p['a0']), p['a_up'], p['g_up'],
            row(p['k_k']), row(p['k_a']), row(p['r_k']), row(p['ln_w']), row(p['ln_b']))


def rwkv_chunked(pb, shift0, s0, p, batch, seq):
    m, nb = pb.shape
    n_heads = s0.shape[1]
    d_b = n_heads * N_B
    t = CHUNK
    ns = SEQS_PER_STEP if batch % SEQS_PER_STEP == 0 else 1
    assert seq % t == 0 and t & (t - 1) == 0
    nc = seq // t
    head_sum = jnp.kron(jnp.eye(n_heads, dtype=F32), jnp.ones((N_B, N_B), F32)).astype(BF16)
    params = _rwkv_params(p, d_b) + (head_sum,)
    const = lambda b, c: (0, 0)
    o, s_fin = pl.pallas_call(
        functools.partial(_rwkv_chunk_kernel, n_heads=n_heads),
        grid=(batch // ns, nc),
        in_specs=[pl.BlockSpec((ns, t, nb), lambda b, c: (b, c, 0)),
                  pl.BlockSpec((ns, 1, nb), lambda b, c: (b, 0, 0)),
                  pl.BlockSpec((ns, n_heads, N_B, N_B), lambda b, c: (b, 0, 0, 0))]
                 + [pl.BlockSpec(x.shape, const) for x in params],
        out_specs=[pl.BlockSpec((ns, t, d_b), lambda b, c: (b, c, 0)),
                   pl.BlockSpec((ns, n_heads, N_B, N_B), lambda b, c: (b, 0, 0, 0))],
        out_shape=[jax.ShapeDtypeStruct((batch, seq, d_b), F32),
                   jax.ShapeDtypeStruct((batch, n_heads, N_B, N_B), F32)],
        scratch_shapes=[pltpu.VMEM((ns, SUBLANES, nb), F32), pltpu.VMEM((ns, n_heads, N_B, N_B), F32),
                        pltpu.VMEM((ns * t, d_b), F32)],
        compiler_params=_cparams("arbitrary", "arbitrary"),
        name="rwkv_chunked",
    )(pb.reshape(batch, seq, nb), shift0.reshape(batch, 1, nb), s0, *params)
    return o.reshape(m, d_b), s_fin


def _col_from_row(x_row, eye):
    return jnp.sum(eye * x_row, axis=-1, keepdims=True)


def _row_from_col(x_col, eye):
    return jnp.sum(eye * x_col, axis=0, keepdims=True)


def _rwkv_step_kernel(pb_ref, shift0_ref, s0_ref, mu_ref, w0_ref, wup_ref, a0_ref, aup_ref, gup_ref,
                      kk_ref, ka_ref, rk_ref, lnw_ref, lnb_ref, o_ref, s_ref, *, n_heads):
    d_b = n_heads * N_B
    r, k_mod, v, kk_raw, a, g, log_decay = _rwkv_tokens(
        pb_ref[0], shift0_ref[0], mu_ref[...], w0_ref[...], wup_ref[...], a0_ref[...], aup_ref[...],
        gup_ref[...], kk_ref[...], ka_ref[...], d_b)
    decay = jnp.exp(log_decay)
    eye = (_iota2((N_B, N_B), 0) == _iota2((N_B, N_B), 1)).astype(F32)
    for h in range(n_heads):
        lanes = slice(h * N_B, (h + 1) * N_B)
        kk = _l2(kk_raw[:, lanes])
        s = s0_ref[0, h]
        sa = jnp.sum(s * kk, axis=-1, keepdims=True)
        v_col = _col_from_row(v[:, lanes], eye)
        s = s * decay[:, lanes] - sa * (kk * a[:, lanes]) + v_col * k_mod[:, lanes]
        s_ref[0, h] = s
        o = _row_from_col(jnp.sum(s * r[:, lanes], axis=-1, keepdims=True), eye)
        o_ref[0, :, lanes] = _rwkv_finish(o, r[:, lanes], k_mod[:, lanes], v[:, lanes], g[:, lanes],
                                          rk_ref[:, lanes], lnw_ref[:, lanes], lnb_ref[:, lanes])


def rwkv_step(pb, shift0, s0, p):
    b, nb = pb.shape
    n_heads = s0.shape[1]
    d_b = n_heads * N_B
    params = _rwkv_params(p, d_b)
    o, s = pl.pallas_call(
        functools.partial(_rwkv_step_kernel, n_heads=n_heads),
        grid=(b,),
        in_specs=[pl.BlockSpec((1, 1, nb), lambda i: (i, 0, 0)),
                  pl.BlockSpec((1, 1, nb), lambda i: (i, 0, 0)),
                  pl.BlockSpec((1, n_heads, N_B, N_B), lambda i: (i, 0, 0, 0))]
                 + [pl.BlockSpec(x.shape, lambda i: (0, 0)) for x in params],
        out_specs=[pl.BlockSpec((1, 1, d_b), lambda i: (i, 0, 0)),
                   pl.BlockSpec((1, n_heads, N_B, N_B), lambda i: (i, 0, 0, 0))],
        out_shape=[jax.ShapeDtypeStruct((b, 1, d_b), F32),
                   jax.ShapeDtypeStruct((b, n_heads, N_B, N_B), F32)],
        compiler_params=_cparams("arbitrary"),
        name="rwkv_step",
    )(pb.reshape(b, 1, nb), shift0.reshape(b, 1, nb), s0, *params)
    return o.reshape(b, d_b), s


def _gdn_gates(gates, alog_row, dtb_row):
    beta = _sigmoid(gates)
    g = -jnp.exp(alog_row) * _softplus(gates + dtb_row)
    return beta, g


def _gdn_chunk_kernel(qkv_ref, z_ref, gates_ref, conv0_ref, s0_ref, cw_ref, alog_ref, dtb_ref, onorm_ref,
                      o_ref, sfin_ref, carry_scr, s_scr, *, n_heads):
    c = pl.program_id(1)
    n_seq, t, _ = qkv_ref.shape
    dqk = n_heads * DK_C

    @pl.when(c == 0)
    def _():
        carry_scr[:, 0:SUBLANES, :] = jnp.zeros((n_seq, SUBLANES, carry_scr.shape[2]), F32)
        carry_scr[:, SUBLANES - (DN_CONV_W - 1):SUBLANES, :] = conv0_ref[...]
        s_scr[...] = s0_ref[...]

    ys = []
    for i in range(n_seq):
        x = qkv_ref[i]
        carry_scr[i, SUBLANES:SUBLANES + t, :] = x
        yi = cw_ref[DN_CONV_W - 1:DN_CONV_W, :] * x
        for d in range(1, DN_CONV_W):
            yi = yi + (cw_ref[DN_CONV_W - 1 - d:DN_CONV_W - d, :]
                       * carry_scr[i, SUBLANES - d:SUBLANES - d + t, :])
        carry_scr[i, 0:SUBLANES, :] = x[t - SUBLANES:t]
        ys.append(yi)
    y = _silu(jnp.concatenate(ys, axis=0))

    row = _iota2((t, t), 0)
    col = _iota2((t, t), 1)
    incl = row >= col
    strict = row > col
    rows = n_seq * t
    srow = _iota2((rows, rows), 0)
    scol = _iota2((rows, rows), 1)
    lg_t = t.bit_length() - 1
    same_seq = (srow >> lg_t) == (scol >> lg_t)
    gates = jnp.concatenate([gates_ref[i] for i in range(n_seq)], axis=0)
    beta_all, g_all = _gdn_gates(gates, alog_ref[...], dtb_ref[...])
    gcum_all = _dot_mask_lhs((same_seq & (srow >= scol)).astype(BF16), g_all)
    g3 = _split3(gcum_all)
    lane = _iota2((t, LANES), 1)
    z = jnp.concatenate([z_ref[i] for i in range(n_seq)], axis=0)

    items = [(i, h) for i in range(n_seq) for h in range(n_heads)]
    n = range(len(items))
    rs = lambda i: slice(i * t, (i + 1) * t)
    q = [_l2(y[rs(i), h * DK_C:(h + 1) * DK_C]) * (DK_C ** -0.5) for i, h in items]
    k = [_l2(y[rs(i), dqk + h * DK_C:dqk + (h + 1) * DK_C]) for i, h in items]
    v = [y[rs(i), 2 * dqk + h * DV_C:2 * dqk + (h + 1) * DV_C] for i, h in items]
    beta = [beta_all[rs(i), h:h + 1] for i, h in items]
    gc = [gcum_all[rs(i), n_heads + h:n_heads + h + 1] for i, h in items]
    egc = [jnp.exp(x) for x in gc]
    nt = lambda a, b: lax.dot_general(a, b, (((1,), (1,)), ((), ())), preferred_element_type=F32)
    gc_row = []
    for i, h in items:
        pk = (lane == n_heads + h).astype(BF16)
        gc_row.append(nt(pk, g3[0][rs(i)]) + (nt(pk, g3[1][rs(i)]) + nt(pk, g3[2][rs(i)])))
    lmask = [jnp.where(incl, jnp.exp(jnp.where(incl, gc[j] - gc_row[j], 0.0)), 0.0) for j in n]
    kb = [k[j] * beta[j] for j in n]
    kq = [_dot_nt(jnp.concatenate([kb[j], q[j]], axis=0), k[j]) for j in n]
    tinv = _tri_inv_multi([jnp.where(strict, kq[j][:t] * lmask[j], 0.0) for j in n])
    uw = [_dot(tinv[j], jnp.concatenate([v[j] * beta[j], kb[j] * egc[j]], axis=1)) for j in n]
    s = [s_scr[i, h] for i, h in items]
    ws = [_dot(jnp.concatenate([uw[j][:, DV_C:], q[j] * egc[j]], axis=0), s[j]) for j in n]
    v_new = [uw[j][:, :DV_C] - ws[j][:t] for j in n]
    o = [ws[j][t:] + _dot(jnp.where(incl, kq[j][t:] * lmask[j], 0.0), v_new[j]) for j in n]
    for j, (i, h) in enumerate(items):
        g_last = gc[j][t - 1:t, :]
        s_scr[i, h] = s[j] * jnp.exp(g_last) + _dot_tn(k[j] * jnp.exp(g_last - gc[j]), v_new[j])
    for j, (i, h) in enumerate(items):
        o_ref[i, :, h * DV_C:(h + 1) * DV_C] = (_rms(o[j], onorm_ref[...])
                                                * _silu(z[rs(i), h * DV_C:(h + 1) * DV_C]))

    @pl.when(c == pl.num_programs(1) - 1)
    def _():
        sfin_ref[...] = s_scr[...]


def _gdn_gate_params(a_log, dt_bias, n_heads):
    pad = lambda x: jnp.zeros((1, LANES), F32).at[0, n_heads:2 * n_heads].set(x)
    return pad(a_log), pad(dt_bias)


def gdn_chunked(qkvz, gates, conv0, s0, cw, a_log, dt_bias, onorm, batch, seq):
    m = qkvz.shape[0]
    n_heads = s0.shape[1]
    dconv = cw.shape[1]
    dvv = n_heads * DV_C
    assert qkvz.shape[1] == dconv + dvv and dconv % dvv == 0
    t = CHUNK
    ns = SEQS_PER_STEP if batch % SEQS_PER_STEP == 0 else 1
    assert seq % t == 0 and t & (t - 1) == 0
    nc = seq // t
    alog_row, dtb_row = _gdn_gate_params(a_log, dt_bias, n_heads)
    const = lambda b, c: (0, 0)
    qkvz3 = qkvz.reshape(batch, seq, dconv + dvv)
    o, s_fin = pl.pallas_call(
        functools.partial(_gdn_chunk_kernel, n_heads=n_heads),
        grid=(batch // ns, nc),
        in_specs=[pl.BlockSpec((ns, t, dconv), lambda b, c: (b, c, 0)),
                  pl.BlockSpec((ns, t, dvv), lambda b, c: (b, c, dconv // dvv)),
                  pl.BlockSpec((ns, t, LANES), lambda b, c: (b, c, 0)),
                  pl.BlockSpec((ns, DN_CONV_W - 1, dconv), lambda b, c: (b, 0, 0)),
                  pl.BlockSpec((ns, n_heads, DK_C, DV_C), lambda b, c: (b, 0, 0, 0)),
                  pl.BlockSpec((DN_CONV_W, dconv), const),
                  pl.BlockSpec((1, LANES), const),
                  pl.BlockSpec((1, LANES), const),
                  pl.BlockSpec((1, DV_C), const)],
        out_specs=[pl.BlockSpec((ns, t, dvv), lambda b, c: (b, c, 0)),
                   pl.BlockSpec((ns, n_heads, DK_C, DV_C), lambda b, c: (b, 0, 0, 0))],
        out_shape=[jax.ShapeDtypeStruct((batch, seq, dvv), F32),
                   jax.ShapeDtypeStruct((batch, n_heads, DK_C, DV_C), F32)],
        scratch_shapes=[pltpu.VMEM((ns, SUBLANES + t, dconv), F32),
                        pltpu.VMEM((ns, n_heads, DK_C, DV_C), F32)],
        compiler_params=_cparams("arbitrary", "arbitrary"),
        name="gdn_chunked",
    )(qkvz3, qkvz3, gates.reshape(batch, seq, LANES), conv0, s0, cw, alog_row, dtb_row,
      onorm.reshape(1, DV_C))
    return o.reshape(m, dvv), s_fin


def _gdn_step_kernel(qkv_ref, z_ref, gates_ref, conv0_ref, s0_ref, cw_ref, alog_ref, dtb_ref, onorm_ref,
                     o_ref, s_ref, *, n_heads):
    dqk = n_heads * DK_C
    x = qkv_ref[0]
    y = cw_ref[DN_CONV_W - 1:DN_CONV_W, :] * x
    for i in range(DN_CONV_W - 1):
        y = y + cw_ref[i:i + 1, :] * conv0_ref[0, i:i + 1, :]
    y = _silu(y)
    beta_all, g_all = _gdn_gates(gates_ref[0], alog_ref[...], dtb_ref[...])
    z = z_ref[0]
    eye = (_iota2((DK_C, DK_C), 0) == _iota2((DK_C, DK_C), 1)).astype(F32)
    for h in range(n_heads):
        q = _l2(y[:, h * DK_C:(h + 1) * DK_C]) * (DK_C ** -0.5)
        k = _l2(y[:, dqk + h * DK_C:dqk + (h + 1) * DK_C])
        v = y[:, 2 * dqk + h * DV_C:2 * dqk + (h + 1) * DV_C]
        beta = beta_all[:, h:h + 1]
        decay = jnp.exp(g_all[:, n_heads + h:n_heads + h + 1])
        s = s0_ref[0, h]
        k_col = _col_from_row(k, eye)
        q_col = _col_from_row(q, eye)
        v_new = beta * v - jnp.sum((k_col * (beta * decay)) * s, axis=0, keepdims=True)
        o = (jnp.sum((q_col * decay) * s, axis=0, keepdims=True)
             + jnp.sum(q * k, axis=-1, keepdims=True) * v_new)
        s_ref[0, h] = s * decay + k_col * v_new
        o_ref[0, :, h * DV_C:(h + 1) * DV_C] = (_rms(o, onorm_ref[...])
                                                * _silu(z[:, h * DV_C:(h + 1) * DV_C]))


def gdn_step(qkvz, gates, conv0, s0, cw, a_log, dt_bias, onorm):
    b = qkvz.shape[0]
    n_heads = s0.shape[1]
    dconv = cw.shape[1]
    dvv = n_heads * DV_C
    alog_row, dtb_row = _gdn_gate_params(a_log, dt_bias, n_heads)
    qkvz3 = qkvz.reshape(b, 1, dconv + dvv)
    const = lambda i: (0, 0)
    o, s = pl.pallas_call(
        functools.partial(_gdn_step_kernel, n_heads=n_heads),
        grid=(b,),
        in_specs=[pl.BlockSpec((1, 1, dconv), lambda i: (i, 0, 0)),
                  pl.BlockSpec((1, 1, dvv), lambda i: (i, 0, dconv // dvv)),
                  pl.BlockSpec((1, 1, LANES), lambda i: (i, 0, 0)),
                  pl.BlockSpec((1, DN_CONV_W - 1, dconv), lambda i: (i, 0, 0)),
                  pl.BlockSpec((1, n_heads, DK_C, DV_C), lambda i: (i, 0, 0, 0)),
                  pl.BlockSpec((DN_CONV_W, dconv), const),
                  pl.BlockSpec((1, LANES), const),
                  pl.BlockSpec((1, LANES), const),
                  pl.BlockSpec((1, DV_C), const)],
        out_specs=[pl.BlockSpec((1, 1, dvv), lambda i: (i, 0, 0)),
                   pl.BlockSpec((1, n_heads, DK_C, DV_C), lambda i: (i, 0, 0, 0))],
        out_shape=[jax.ShapeDtypeStruct((b, 1, dvv), F32),
                   jax.ShapeDtypeStruct((b, n_heads, DK_C, DV_C), F32)],
        compiler_params=_cparams("arbitrary"),
        name="gdn_step",
    )(qkvz3, qkvz3, gates.reshape(b, 1, LANES), conv0, s0, cw, alog_row, dtb_row,
      onorm.reshape(1, DV_C))
    return o.reshape(b, dvv), s


def _row_tile(m, want):
    return want if m % want == 0 else m


def _trunk(x, kv, wkv0, shift0, dn0, dnconv0, ffnconv0, p):
    b, t, d = x.shape
    m = b * t
    depth = p['norm_ffn'].shape[0]
    d_a = p['sb_bias'].shape[1] * DH_A
    nb = p['mu_b'].shape[1]
    dconv = p['conv_dn'].shape[2]
    n_heads_c = p['a_log_dn'].shape[1]
    dvv = n_heads_c * DV_C
    tm = _row_tile(m, PROJ_ROWS)
    tm_out = _row_tile(m, OUT_ROWS)
    new_k, new_v, new_wkv, new_shift, new_dn, new_dnconv, new_ffn = [], [], [], [], [], [], []
    x2 = x.reshape(m, d)
    for l in range(depth):
        i = l // 2
        if l % 2 == 0:
            qkv, pb = norm_matmul(x2, p['norm_mix_even'][i], p['w_in_even'][i], (3 * d_a, nb), tm)
            rp = dict(mu=p['mu_b'][i], w0=p['w0_b'][i], w_up=p['w_up_b'][i], a0=p['a0_b'][i],
                      a_up=p['a_up_b'][i], g_up=p['g_up_b'][i], k_k=p['kk_b'][i], k_a=p['ka_b'][i],
                      r_k=p['rk_b'][i], ln_w=p['lnx_w_b'][i], ln_b=p['lnx_b_b'][i])
            if kv is None:
                o_a, kn_t, v_t = sb_prompt(qkv, p['qnorm_a'][i], p['knorm_a'][i], p['sb_bias'][i], b, t)
                o_b, wkv = rwkv_chunked(pb, shift0[i], wkv0[i], rp, b, t)
                shift = pb.reshape(b, t, nb)[:, t - 1]
                new_k.append(jnp.transpose(kn_t, (0, 3, 1, 2)))
                new_v.append(jnp.transpose(v_t, (0, 3, 1, 2)))
            else:
                o_a, kn = sb_decode(qkv, kv[0], kv[1], kv[2], i, p['qnorm_a'][i], p['knorm_a'][i],
                                    p['sb_bias'][i])
                o_b, wkv = rwkv_step(pb, shift0[i], wkv0[i], rp)
                shift = pb
                new_k.append(kn.reshape(b, t, d_a // DH_A, DH_A))
                new_v.append(qkv[:, 2 * d_a:].reshape(b, t, d_a // DH_A, DH_A))
            w_out = p['w_out_even'][i]
            mix, w_mix = [o_a, o_b], [w_out[:d_a], w_out[d_a:]]
            new_wkv.append(wkv)
            new_shift.append(shift)
        else:
            qkvz, gates = norm_matmul(x2, p['norm_mix_odd'][i], p['w_in_odd'][i], (dconv + dvv, LANES), tm)
            if kv is None:
                o_c, s_fin = gdn_chunked(qkvz, gates, dnconv0[i], dn0[i], p['conv_dn'][i], p['a_log_dn'][i],
                                         p['dt_bias_dn'][i], p['onorm_dn'][i], b, t)
                cbuf = qkvz.reshape(b, t, dconv + dvv)[:, t - (DN_CONV_W - 1):, :dconv]
            else:
                o_c, s_fin = gdn_step(qkvz, gates, dnconv0[i], dn0[i], p['conv_dn'][i], p['a_log_dn'][i],
                                      p['dt_bias_dn'][i], p['onorm_dn'][i])
                cbuf = jnp.concatenate([dnconv0[i][:, 1:], qkvz[:, None, :dconv]], axis=1)
            mix, w_mix = [o_c], [p['w_out_odd'][i]]
            new_dn.append(s_fin)
            new_dnconv.append(cbuf)
        ffn_args = (p['norm_ffn'][l], p['w_gate'][l], p['w_up'][l], p['conv_ffn'][l], p['conv_ffn_b'][l],
                    p['w_down'][l])
        if kv is None:
            x3, fbuf = conv_ffn([o.reshape(b, t, -1) for o in mix], w_mix, x2.reshape(b, t, d),
                                ffnconv0[l], *ffn_args, tm=_row_tile(t, FFN_ROWS))
            x2 = x3.reshape(m, d)
        else:
            x2 = matmul_res(mix, w_mix, x2, tm_out)
            x2, fbuf = conv_ffn_step(x2, ffnconv0[l], *ffn_args)
        new_ffn.append(fbuf)
    states = tuple(jnp.stack(s) for s in (new_k, new_v, new_wkv, new_shift, new_dn, new_dnconv, new_ffn))
    return x2.reshape(b, t, d), states


def kernel(x_prompt, x_sample, cache_k, cache_v, page_table, state_wkv, state_shift, state_dn,
           state_dn_conv, state_ffn_conv, norm_mix_even, w_in_even, qnorm_a, knorm_a, sb_bias,
           mu_b, w0_b, w_up_b, a0_b, a_up_b, g_up_b, kk_b, ka_b, rk_b, lnx_w_b, lnx_b_b,
           w_out_even, norm_mix_odd, w_in_odd, conv_dn, a_log_dn, dt_bias_dn, onorm_dn, w_out_odd,
           norm_ffn, w_gate, w_up, conv_ffn, conv_ffn_b, w_down):
    n_heads_c = a_log_dn.shape[1]
    n_gate_cols = 2 * n_heads_c
    assert x_sample.shape[1] == 1 and n_gate_cols <= LANES
    w_in_odd_p = jnp.pad(w_in_odd, ((0, 0), (0, 0), (0, LANES - n_gate_cols)))
    bf = lambda w: w.astype(BF16)
    p = dict(norm_mix_even=norm_mix_even, w_in_even=bf(w_in_even), qnorm_a=qnorm_a, knorm_a=knorm_a,
             sb_bias=sb_bias, mu_b=mu_b, w0_b=w0_b, w_up_b=bf(w_up_b), a0_b=a0_b, a_up_b=bf(a_up_b),
             g_up_b=bf(g_up_b), kk_b=kk_b, ka_b=ka_b, rk_b=rk_b, lnx_w_b=lnx_w_b, lnx_b_b=lnx_b_b,
             w_out_even=bf(w_out_even), norm_mix_odd=norm_mix_odd, w_in_odd=bf(w_in_odd_p),
             conv_dn=conv_dn, a_log_dn=a_log_dn, dt_bias_dn=dt_bias_dn, onorm_dn=onorm_dn,
             w_out_odd=bf(w_out_odd), norm_ffn=norm_ffn, w_gate=bf(w_gate), w_up=bf(w_up),
             conv_ffn=conv_ffn, conv_ffn_b=conv_ffn_b, w_down=bf(w_down))
    b = x_prompt.shape[0]
    n_ab = state_wkv.shape[0]
    n_c = state_dn.shape[0]
    depth = norm_ffn.shape[0]
    zeros = lambda like, lead: jnp.zeros((lead, b) + like.shape[2:], F32)
    y_prompt, ps = _trunk(x_prompt, None, zeros(state_wkv, n_ab), zeros(state_shift, n_ab),
                          zeros(state_dn, n_c), zeros(state_dn_conv, n_c), zeros(state_ffn_conv, depth), p)
    y_sample, ss = _trunk(x_sample, (cache_k, cache_v, page_table), state_wkv, state_shift, state_dn,
                          state_dn_conv, state_ffn_conv, p)
    return (y_prompt, y_sample, ps[0], ps[1], ss[0], ss[1], ps[2], ss[2], ps[3], ss[3],
            ps[4], ss[4], ps[5], ss[5], ps[6], ss[6])
```
